```python
import math
import jax, jax.numpy as jnp
from jax import lax
import numpy as np

D_MODEL = 2048
BATCH = 2
SEQ = 4096
DEPTH = 2

GRID_W = 64
CTX_LEN = 256
EPS = 1e-6
NEG_INF = -1e30

NA_HEADS = 8
NA_HEAD_DIM = 128
NA_WIN_H = 8
NA_WIN_W = 16
NA_WIDTH = NA_HEADS * NA_HEAD_DIM
ML_HEADS = 8
ML_DK = 128
ML_DV = 128
ML_WIDTH = ML_HEADS * ML_DV
ML_CONV = 5
ML_CHUNK = 64
EVEN_IN_SIZES = (NA_WIDTH, NA_WIDTH, NA_WIDTH, 2 * ML_HEADS * ML_DK, ML_WIDTH, ML_WIDTH, 4 * ML_HEADS)
EVEN_IN_WIDTH = sum(EVEN_IN_SIZES)

DA_HEADS = 8
DA_DIM = 128
DA_VDIM = 2 * DA_DIM
DA_QK_WIDTH = DA_HEADS * 2 * DA_DIM
DA_V_WIDTH = DA_HEADS * DA_VDIM
Q_BLOCK = 128
ROPE_THETA = 10000.0

N_EXPERTS = 64
N_GROUPS = 8
TOPK_GROUPS = 4
TOP_K = 6
D_EXPERT = 512
ROUTED_SCALE = 2.5

kernel_name = 'hybrid_na_mlstm_diffattn_moe_dit'


def rmsnorm(x, gain):
    x32 = x.astype(jnp.float32)
    y = x32 * lax.rsqrt(jnp.mean(x32 * x32, axis=-1, keepdims=True) + EPS)
    return (y * gain.astype(jnp.float32)).astype(x.dtype)


def modulate(h, shift, scale):
    return h * (1.0 + scale) + shift


def axial_rope_tables(seq_len, dim):
    t = jnp.arange(seq_len)
    row = (t // GRID_W).astype(jnp.float32)
    col = (t % GRID_W).astype(jnp.float32)
    n_freq = dim // 4
    inv_freq = ROPE_THETA ** (-jnp.arange(n_freq, dtype=jnp.float32) / n_freq)
    ang = jnp.concatenate([row[:, None] * inv_freq, col[:, None] * inv_freq], axis=-1)
    ang = jnp.concatenate([ang, ang], axis=-1)
    return jnp.cos(ang), jnp.sin(ang)


def apply_rope(x, cos, sin):
    x1, x2 = jnp.split(x, 2, axis=-1)
    rot = jnp.concatenate([-x2, x1], axis=-1)
    return (x.astype(jnp.float32) * cos + rot.astype(jnp.float32) * sin).astype(x.dtype)


def centred_dwconv(t, w, b):
    y = lax.conv_general_dilated(t, w[:, None, :].astype(t.dtype), window_strides=(1,), padding='SAME',
                                 dimension_numbers=('NWC', 'WIO', 'NWC'), feature_group_count=t.shape[-1])
    return y + b


def neighbourhood_attention(q, k, v, q_c, k_c, v_c, rpb):
    bsz, n_h, seq, d = q.shape
    rows = seq // GRID_W
    kh = min(NA_WIN_H, rows)
    scale = d ** -0.5
    qg = q.reshape(bsz, n_h, rows, GRID_W, d)
    kg = k.reshape(bsz, n_h, rows, GRID_W, d)
    vg = v.reshape(bsz, n_h, rows, GRID_W, d)
    r = jnp.arange(rows)
    key_rows = jnp.clip(r - kh // 2, 0, rows - kh)[:, None] + jnp.arange(kh)[None, :]
    k_blk = kg[:, :, key_rows]
    v_blk = vg[:, :, key_rows]
    cols = jnp.arange(GRID_W)
    col_start = jnp.clip(cols - NA_WIN_W // 2, 0, GRID_W - NA_WIN_W)
    col_ok = (cols[None, :] >= col_start[:, None]) & (cols[None, :] < col_start[:, None] + NA_WIN_W)
    row_idx = (key_rows - r[:, None])[:, None, :, None] + (NA_WIN_H - 1)
    col_idx = jnp.clip(cols[None, :] - cols[:, None], 1 - NA_WIN_W, NA_WIN_W - 1)[None, :, None, :] + (NA_WIN_W - 1)
    bias = rpb.astype(jnp.float32)[:, row_idx, col_idx]
    bias = jnp.where(col_ok[None, None, :, None, :], bias, NEG_INF)
    s_loc = jnp.einsum('bhrwd,bhrjcd->bhrwjc', qg, k_blk).astype(jnp.float32) * scale + bias[None]
    s_ctx = jnp.einsum('bhrwd,bhnd->bhrwn', qg, k_c).astype(jnp.float32) * scale
    n_loc = kh * GRID_W
    p = jax.nn.softmax(jnp.concatenate([s_loc.reshape(bsz, n_h, rows, GRID_W, n_loc), s_ctx], axis=-1), axis=-1)
    p = p.astype(v.dtype)
    p_loc = p[..., :n_loc].reshape(bsz, n_h, rows, GRID_W, kh, GRID_W)
    o = jnp.einsum('bhrwjc,bhrjcd->bhrwd', p_loc, v_blk) + jnp.einsum('bhrwn,bhnd->bhrwd', p[..., n_loc:], v_c)
    s_cc = jnp.einsum('bhmd,bhnd->bhmn', q_c, k_c).astype(jnp.float32) * scale
    o_c = jnp.einsum('bhmn,bhnd->bhmd', jax.nn.softmax(s_cc, axis=-1).astype(v.dtype), v_c)
    return o.reshape(bsz, n_h, seq, d), o_c


def mlstm_scan(q, k, v, log_i, log_f, state):
    bsz, n_h, seq, _ = q.shape
    n_chunks = seq // ML_CHUNK

    def to_chunks(t):
        return jnp.moveaxis(t.reshape(t.shape[:2] + (n_chunks, ML_CHUNK) + t.shape[3:]), 2, 0)

    tri = jnp.tril(jnp.ones((ML_CHUNK, ML_CHUNK), dtype=bool))

    def step(carry, inp):
        c_mat, n_vec, m = carry
        qc, kc, vc, ic, fc = inp
        b = jnp.cumsum(fc, axis=-1)
        d_mat = jnp.where(tri, b[..., :, None] - b[..., None, :] + ic[..., None, :], -jnp.inf)
        inter = b + m[..., None]
        m_t = jnp.maximum(inter, jnp.max(d_mat, axis=-1))
        w_inter = jnp.exp(inter - m_t)
        qk = jnp.einsum('bhtd,bhsd->bhts', qc, kc) * jnp.exp(d_mat - m_t[..., None])
        num = w_inter[..., None] * jnp.einsum('bhtd,bhde->bhte', qc, c_mat) + jnp.einsum('bhts,bhse->bhte', qk, vc)
        den = w_inter * jnp.einsum('bhtd,bhd->bht', qc, n_vec) + jnp.sum(qk, axis=-1)
        h = num / jnp.maximum(jnp.abs(den), jnp.exp(-m_t))[..., None]
        b_last = b[..., -1]
        g = b_last[..., None] - b + ic
        m_new = jnp.maximum(b_last + m, jnp.max(g, axis=-1))
        decay = jnp.exp(b_last + m - m_new)
        wg = jnp.exp(g - m_new[..., None])
        c_new = decay[..., None, None] * c_mat + jnp.einsum('bhs,bhsd,bhse->bhde', wg, kc, vc)
        n_new = decay[..., None] * n_vec + jnp.einsum('bhs,bhsd->bhd', wg, kc)
        return (c_new, n_new, m_new), h

    xs = tuple(to_chunks(t) for t in (q, k, v, log_i, log_f))
    state, hs = lax.scan(step, state, xs)
    h = jnp.moveaxis(hs, 0, 2).reshape(bsz, n_h, seq, v.shape[-1])
    return h, state


def mlstm_prefix(q, k, v, log_i, log_f, n_ctx, reverse):
    def seg(t):
        a, b = t[:, :, :n_ctx], t[:, :, n_ctx:]
        if reverse:
            a, b = jnp.flip(a, axis=2), jnp.flip(b, axis=2)
        return a, b

    (qc, qx), (kc, kx), (vc, vx), (ic, ix), (fc, fx) = [seg(t) for t in (q, k, v, log_i, log_f)]
    bsz, n_h, _, dk = q.shape
    st = (jnp.zeros((bsz, n_h, dk, v.shape[-1]), jnp.float32), jnp.zeros((bsz, n_h, dk), jnp.float32),
          jnp.zeros((bsz, n_h), jnp.float32))
    h_c, st = mlstm_scan(qc, kc, vc, ic, fc, st)
    h_x, _ = mlstm_scan(qx, kx, vx, ix, fx, st)
    if reverse:
        h_c, h_x = jnp.flip(h_c, axis=2), jnp.flip(h_x, axis=2)
    return jnp.concatenate([h_c, h_x], axis=2)


def even_mixer(hx, hc, w_in, na_qnorm, na_knorm, na_rpb, ml_conv_w, ml_conv_b, ml_gate_b, ml_hnorm, w_out):
    bsz, seq, _ = hx.shape
    n_ctx = hc.shape[1]
    tot = n_ctx + seq
    h = jnp.concatenate([hc, hx], axis=1)
    cuts = [int(v) for v in np.cumsum(EVEN_IN_SIZES)[:-1]]
    na_q, na_k, na_v, ml_qk, ml_v, ml_o, ml_g = jnp.split(h @ w_in, cuts, axis=-1)

    def heads(t, n):
        return t.reshape(bsz, t.shape[1], n, -1).transpose(0, 2, 1, 3)

    q = rmsnorm(heads(na_q, NA_HEADS), na_qnorm)
    k = rmsnorm(heads(na_k, NA_HEADS), na_knorm)
    v = heads(na_v, NA_HEADS)
    na_x, na_c = neighbourhood_attention(q[:, :, n_ctx:], k[:, :, n_ctx:], v[:, :, n_ctx:],
                                         q[:, :, :n_ctx], k[:, :, :n_ctx], v[:, :, :n_ctx], na_rpb)
    na_h = jnp.concatenate([na_c, na_x], axis=2).transpose(0, 2, 1, 3).reshape(bsz, tot, NA_WIDTH)

    qk = jnp.concatenate([centred_dwconv(ml_qk[:, :n_ctx], ml_conv_w, ml_conv_b),
                          centred_dwconv(ml_qk[:, n_ctx:], ml_conv_w, ml_conv_b)], axis=1)
    mq, mk = jnp.split(jax.nn.silu(qk).astype(jnp.float32), 2, axis=-1)
    mq = heads(mq, ML_HEADS)
    mk = heads(mk, ML_HEADS) * (ML_DK ** -0.5)
    mv = heads(ml_v.astype(jnp.float32), ML_HEADS)
    g = ml_g.astype(jnp.float32).reshape(bsz, tot, 2, 2, ML_HEADS) + ml_gate_b.astype(jnp.float32)
    g = g.transpose(2, 3, 0, 4, 1)
    log_i = g[:, 0]
    log_f = jax.nn.log_sigmoid(g[:, 1])
    hm = (mlstm_prefix(mq, mk, mv, log_i[0], log_f[0], n_ctx, False)
          + mlstm_prefix(mq, mk, mv, log_i[1], log_f[1], n_ctx, True))
    ml_h = rmsnorm(hm.transpose(0, 2, 1, 3), ml_hnorm.reshape(ML_HEADS, ML_DV)).reshape(bsz, tot, ML_WIDTH)
    ml_h = ml_h.astype(hx.dtype) * jax.nn.sigmoid(ml_o)

    out = jnp.concatenate([na_h, ml_h], axis=-1) @ w_out
    return out[:, n_ctx:], out[:, :n_ctx]


def odd_mixer(hx, hc, w_in, qnorm, knorm, lam, subln, w_out, lambda_init, cos, sin, ctx_queries):
    bsz, seq, _ = hx.shape

    def qk_heads(t, gain):
        t = t.reshape(bsz, t.shape[1], DA_HEADS, 2, DA_DIM).transpose(0, 2, 3, 1, 4)
        return rmsnorm(t, gain)

    def v_heads(t):
        return t.reshape(bsz, t.shape[1], DA_HEADS, DA_VDIM).transpose(0, 2, 1, 3)

    q_x, k_x, v_x = jnp.split(hx @ w_in, [DA_QK_WIDTH, 2 * DA_QK_WIDTH], axis=-1)
    q_x = apply_rope(qk_heads(q_x, qnorm), cos, sin)
    k_x = apply_rope(qk_heads(k_x, knorm), cos, sin)
    v_x = v_heads(v_x)
    if ctx_queries:
        q_c, k_c, v_c = jnp.split(hc @ w_in, [DA_QK_WIDTH, 2 * DA_QK_WIDTH], axis=-1)
        q_c = qk_heads(q_c, qnorm)
    else:
        k_c, v_c = jnp.split(hc @ w_in[:, DA_QK_WIDTH:], [DA_QK_WIDTH], axis=-1)
    k_c = qk_heads(k_c, knorm)
    v_c = v_heads(v_c)

    lam32 = lam.astype(jnp.float32)
    lam_full = jnp.exp(jnp.sum(lam32[0] * lam32[1])) - jnp.exp(jnp.sum(lam32[2] * lam32[3])) + lambda_init

    def attend(q, k, v):
        s = jnp.einsum('bhiqd,bhikd->bhiqk', q, k).astype(jnp.float32) * (DA_DIM ** -0.5)
        p = jax.nn.softmax(s, axis=-1)
        a = p[:, :, 0] - lam_full * p[:, :, 1]
        return jnp.einsum('bhqk,bhke->bhqe', a.astype(v.dtype), v)

    def finish(o):
        o = rmsnorm(o, subln) * (1.0 - lambda_init)
        return o.transpose(0, 2, 1, 3).reshape(bsz, o.shape[2], DA_V_WIDTH) @ w_out

    k_all = jnp.concatenate([k_x, k_c], axis=3)
    v_all = jnp.concatenate([v_x, v_c], axis=2)
    n_blocks = seq // Q_BLOCK
    q_blocks = jnp.moveaxis(q_x.reshape(bsz, DA_HEADS, 2, n_blocks, Q_BLOCK, DA_DIM), 3, 0)
    o_x = lax.map(lambda qb: attend(qb, k_all, v_all), q_blocks)
    o_x = jnp.moveaxis(o_x, 0, 2).reshape(bsz, DA_HEADS, seq, DA_VDIM)
    out_x = finish(o_x)
    out_c = finish(attend(q_c, k_c, v_c)) if ctx_queries else None
    return out_x, out_c


def moe(h, router_w, router_b, exp_gate, exp_up, exp_down, sh_gate, sh_up, sh_down):
    shape = h.shape
    tok = h.reshape(-1, shape[-1])
    scores = jax.nn.sigmoid((tok @ router_w).astype(jnp.float32))
    sel = scores + router_b.astype(jnp.float32)
    per_group = N_EXPERTS // N_GROUPS
    grp_score = jnp.sum(lax.top_k(sel.reshape(-1, N_GROUPS, per_group), 2)[0], axis=-1)
    _, grp_idx = lax.top_k(grp_score, TOPK_GROUPS)
    grp_keep = jnp.sum(jax.nn.one_hot(grp_idx, N_GROUPS, dtype=jnp.float32), axis=-2) > 0
    sel = jnp.where(jnp.repeat(grp_keep, per_group, axis=-1), sel, NEG_INF)
    _, idx = lax.top_k(sel, TOP_K)
    w = jnp.take_along_axis(scores, idx, axis=-1)
    w = w / jnp.sum(w, axis=-1, keepdims=True) * ROUTED_SCALE
    gates = jnp.sum(jax.nn.one_hot(idx, N_EXPERTS, dtype=jnp.float32) * w[..., None], axis=-2)

    def add_expert(acc, e):
        wg, wu, wd, g = e
        y = (jax.nn.silu(tok @ wg) * (tok @ wu)) @ wd
        return acc + g[:, None].astype(y.dtype) * y, None

    routed, _ = lax.scan(add_expert, jnp.zeros_like(tok), (exp_gate, exp_up, exp_down, gates.T))
    shared = (jax.nn.silu(tok @ sh_gate) * (tok @ sh_up)) @ sh_down
    return (routed + shared).reshape(shape)


def diff_lambda_init(layer):
    return 0.8 - 0.6 * math.exp(-0.3 * layer)


def setup_inputs(seed: int = 0) -> dict:
    key = jax.random.key(seed)
    keys = iter(jax.random.split(key, 128))

    def normal(shape, scale):
        return scale * jax.random.normal(next(keys), shape, jnp.float32)

    def gain(shape):
        return 1.0 + 0.05 * jax.random.normal(next(keys), shape, jnp.float32)

    d = D_MODEL
    inp = {
        'x': normal((BATCH, SEQ, d), 1.0),
        'c': normal((BATCH, d), 1.0),
        'ctx': normal((BATCH, CTX_LEN, d), 1.0),
        'c_ctx': normal((d,), 1.0),
    }
    for layer in range(DEPTH):
        p = 'l%d_' % layer
        inp[p + 'ada_w'] = normal((d, 6 * d), 0.5 * d ** -0.5)
        inp[p + 'ada_b'] = normal((6 * d,), 0.02)
        inp[p + 'norm1'] = gain((d,))
        inp[p + 'norm2'] = gain((d,))
        if layer % 2 == 0:
            inp[p + 'w_in'] = normal((d, EVEN_IN_WIDTH), d ** -0.5)
            inp[p + 'na_qnorm'] = gain((NA_HEAD_DIM,))
            inp[p + 'na_knorm'] = gain((NA_HEAD_DIM,))
            inp[p + 'na_rpb'] = normal((NA_HEADS, 2 * NA_WIN_H - 1, 2 * NA_WIN_W - 1), 0.1)
            inp[p + 'ml_conv_w'] = normal((ML_CONV, 2 * ML_HEADS * ML_DK), ML_CONV ** -0.5)
            inp[p + 'ml_conv_b'] = normal((2 * ML_HEADS * ML_DK,), 0.02)
            f_bias = jnp.linspace(3.0, 6.0, ML_HEADS, dtype=jnp.float32)
            inp[p + 'ml_gate_b'] = normal((2, 2, ML_HEADS), 0.1).at[:, 1].add(f_bias)
            inp[p + 'ml_hnorm'] = gain((ML_WIDTH,))
            inp[p + 'w_out'] = normal((NA_WIDTH + ML_WIDTH, d), (NA_WIDTH + ML_WIDTH) ** -0.5)
        else:
            inp[p + 'w_in'] = normal((d, 2 * DA_QK_WIDTH + DA_V_WIDTH), d ** -0.5)
            inp[p + 'qnorm'] = gain((DA_DIM,))
            inp[p + 'knorm'] = gain((DA_DIM,))
            inp[p + 'lambda'] = normal((4, DA_DIM), 0.1)
            inp[p + 'subln'] = gain((DA_VDIM,))
            inp[p + 'w_out'] = normal((DA_V_WIDTH, d), DA_V_WIDTH ** -0.5)
        inp[p + 'router_w'] = normal((d, N_EXPERTS), d ** -0.5)
        inp[p + 'router_b'] = normal((N_EXPERTS,), 0.01)
        inp[p + 'exp_gate'] = normal((N_EXPERTS, d, D_EXPERT), d ** -0.5)
        inp[p + 'exp_up'] = normal((N_EXPERTS, d, D_EXPERT), d ** -0.5)
        inp[p + 'exp_down'] = normal((N_EXPERTS, D_EXPERT, d), D_EXPERT ** -0.5)
        inp[p + 'sh_gate'] = normal((d, D_EXPERT), d ** -0.5)
        inp[p + 'sh_up'] = normal((d, D_EXPERT), d ** -0.5)
        inp[p + 'sh_down'] = normal((D_EXPERT, d), D_EXPERT ** -0.5)
    return inp


def reference(x, c, ctx, c_ctx,
              l0_ada_w, l0_ada_b, l0_norm1, l0_norm2, l0_w_in, l0_na_qnorm, l0_na_knorm, l0_na_rpb,
              l0_ml_conv_w, l0_ml_conv_b, l0_ml_gate_b, l0_ml_hnorm, l0_w_out,
              l0_router_w, l0_router_b, l0_exp_gate, l0_exp_up, l0_exp_down, l0_sh_gate, l0_sh_up, l0_sh_down,
              l1_ada_w, l1_ada_b, l1_norm1, l1_norm2, l1_w_in, l1_qnorm, l1_knorm, l1_lambda, l1_subln, l1_w_out,
              l1_router_w, l1_router_b, l1_exp_gate, l1_exp_up, l1_exp_down, l1_sh_gate, l1_sh_up, l1_sh_down):
    seq = x.shape[1]
    n_ctx = ctx.shape[1]
    cos, sin = axial_rope_tables(seq, DA_DIM)
    adaln = [(l0_ada_w, l0_ada_b, l0_norm1, l0_norm2), (l1_ada_w, l1_ada_b, l1_norm1, l1_norm2)]
    mixers = [(l0_w_in, l0_na_qnorm, l0_na_knorm, l0_na_rpb, l0_ml_conv_w, l0_ml_conv_b, l0_ml_gate_b, l0_ml_hnorm, l0_w_out),
              (l1_w_in, l1_qnorm, l1_knorm, l1_lambda, l1_subln, l1_w_out)]
    moes = [(l0_router_w, l0_router_b, l0_exp_gate, l0_exp_up, l0_exp_down, l0_sh_gate, l0_sh_up, l0_sh_down),
            (l1_router_w, l1_router_b, l1_exp_gate, l1_exp_up, l1_exp_down, l1_sh_gate, l1_sh_up, l1_sh_down)]
    xc = ctx
    for layer in range(DEPTH):
        ada_w, ada_b, norm1, norm2 = adaln[layer]
        last = layer == DEPTH - 1
        mx = jnp.split(jax.nn.silu(c) @ ada_w + ada_b, 6, axis=-1)
        mc = jnp.split(jax.nn.silu(c_ctx) @ ada_w + ada_b, 6, axis=-1)
        hx = modulate(rmsnorm(x, norm1), mx[0][:, None], mx[1][:, None])
        hc = modulate(rmsnorm(xc, norm1), mc[0], mc[1])
        if layer % 2 == 0:
            ox, oc = even_mixer(hx, hc, *mixers[layer])
        else:
            ox, oc = odd_mixer(hx, hc, *mixers[layer], diff_lambda_init(layer), cos, sin, not last)
        x = x + mx[2][:, None] * ox
        hx = modulate(rmsnorm(x, norm2), mx[3][:, None], mx[4][:, None])
        if last:
            x = x + mx[5][:, None] * moe(hx, *moes[layer])
        else:
            xc = xc + mc[2] * oc
            hc = modulate(rmsnorm(xc, norm2), mc[3], mc[4])
            y = moe(jnp.concatenate([hc, hx], axis=1), *moes[layer])
            xc = xc + mc[5] * y[:, :n_ctx]
            x = x + mx[5][:, None] * y[:, n_ctx:]
    return x
```

```python
import functools
import math

import jax
import jax.numpy as jnp
import numpy as np
from jax import lax
from jax.experimental import pallas as pl
from jax.experimental.pallas import tpu as pltpu

F32 = jnp.float32
BF16 = jnp.bfloat16

D = 2048
BATCH = 2
SEQ = 4096
CTX = 256
NX = BATCH * SEQ
NT = NX + BATCH * CTX
GRID_W = 64
EPS = 1e-6
NEG_INF = -1e30

NA_HEADS = 8
HEAD_DIM = 128
NA_WIDTH = NA_HEADS * HEAD_DIM
NA_WIN_H = 8
NA_WIN_W = 16
NA_QROWS = 4
NA_KROWS = 12
ML_HEADS = 8
ML_WIDTH = ML_HEADS * HEAD_DIM
ML_CONV = 5
ML_CHUNK = 256

DA_HEADS = 8
DA_VDIM = 256
ROPE_THETA = 10000.0

N_EXPERTS = 64
N_GROUPS = 8
TOPK_GROUPS = 4
TOP_K = 6
D_EXPERT = 512
ROUTED_SCALE = 2.5

TOK_TILE = 256
UNIT = 16
UNITS_PER_TILE = (TOK_TILE * TOP_K + N_EXPERTS * (UNIT - 1)) // UNIT + 1
UNITS_PER_TILE = -(-UNITS_PER_TILE // 32) * 32
SLOTS = UNITS_PER_TILE * UNIT
FFN_TM = 256
FFN_UNITS = FFN_TM // UNIT

V7X_VMEM_BYTES = 64 * 1024 * 1024


def _cp(sem, vmem_mb=48):
    assert vmem_mb * 1024 * 1024 < V7X_VMEM_BYTES
    return pltpu.CompilerParams(dimension_semantics=sem, vmem_limit_bytes=vmem_mb * 1024 * 1024)


def _silu(x):
    return x * jax.nn.sigmoid(x)


def _seg_of_tile(i, tm):
    return (i * tm) // SEQ


def _ada_kernel(c_ref, w_ref, b_ref, o_ref):
    s = _silu(c_ref[...]).astype(BF16)
    o_ref[...] = jnp.dot(s, w_ref[...].astype(BF16), preferred_element_type=F32) + b_ref[...]


def adaln(cvec, w, b):
    n = w.shape[1]
    tn = 1024
    out = pl.pallas_call(
        _ada_kernel,
        grid=(n // tn,),
        in_specs=[
            pl.BlockSpec((8, D), lambda j: (0, 0)),
            pl.BlockSpec((D, tn), lambda j: (0, j)),
            pl.BlockSpec((1, tn), lambda j: (0, j)),
        ],
        out_specs=pl.BlockSpec((8, tn), lambda j: (0, j)),
        out_shape=jax.ShapeDtypeStruct((8, n), F32),
        compiler_params=_cp(("arbitrary",)),
        name="adaln",
    )(cvec, w, b.reshape(1, n))
    return out[:3].reshape(18, 1, D)


def _normed(x_ref, g_ref, sh_ref, sc_ref):
    x = x_ref[...]
    ms = jnp.mean(x * x, axis=-1, keepdims=True)
    y = x * lax.rsqrt(ms + EPS) * g_ref[...]
    return y * (1.0 + sc_ref[0]) + sh_ref[0]


def _norm_mod_kernel(x_ref, g_ref, sh_ref, sc_ref, o_ref):
    o_ref[...] = _normed(x_ref, g_ref, sh_ref, sc_ref).astype(o_ref.dtype)


def _mod_spec(which, tm):
    return pl.BlockSpec((1, 1, D), lambda i: (_seg_of_tile(i, tm) * 6 + which, 0, 0))


def norm_mod(x, gain, mods, which_shift, n_rows, tm=256):
    return pl.pallas_call(
        _norm_mod_kernel,
        grid=(n_rows // tm,),
        in_specs=[
            pl.BlockSpec((tm, D), lambda i: (i, 0)),
            pl.BlockSpec((1, D), lambda i: (0, 0)),
            _mod_spec(which_shift, tm),
            _mod_spec(which_shift + 1, tm),
        ],
        out_specs=pl.BlockSpec((tm, D), lambda i: (i, 0)),
        out_shape=jax.ShapeDtypeStruct((n_rows, D), BF16),
        compiler_params=_cp(("parallel",)),
        name="norm_mod",
    )(x, gain.reshape(1, D), mods, mods)


def _route(logits, bias_col):
    tm = logits.shape[1]
    per_group = N_EXPERTS // N_GROUPS
    scores = jax.nn.sigmoid(logits)
    sel = scores + bias_col
    row8 = lax.broadcasted_iota(jnp.int32, (per_group, tm), 0)
    grp = jnp.zeros((N_GROUPS, tm), F32)
    for g in range(N_GROUPS):
        slab = sel[g * per_group:(g + 1) * per_group, :]
        m1 = jnp.max(slab, axis=0, keepdims=True)
        first = jnp.min(jnp.where(slab == m1, row8, per_group), axis=0, keepdims=True)
        m2 = jnp.max(jnp.where(row8 == first, -jnp.inf, slab), axis=0, keepdims=True)
        grp = jnp.where(row8 == g, m1 + m2, grp)
    rank = jnp.zeros((N_GROUPS, tm), jnp.int32)
    for g in range(N_GROUPS):
        vg = grp[g:g + 1, :]
        beats = (vg > grp) | ((vg == grp) & (g < row8))
        rank = rank + jnp.where(beats, 1, 0)
    keep = jnp.where(rank < TOPK_GROUPS, 1.0, 0.0)
    cur = jnp.concatenate(
        [jnp.where(keep[g:g + 1, :] > 0.5, sel[g * per_group:(g + 1) * per_group, :], NEG_INF) for g in range(N_GROUPS)],
        axis=0)
    e_iota = lax.broadcasted_iota(jnp.int32, (N_EXPERTS, tm), 0)
    picked = jnp.zeros((N_EXPERTS, tm), F32)
    for _ in range(TOP_K):
        m = jnp.max(cur, axis=0, keepdims=True)
        idx = jnp.min(jnp.where(cur == m, e_iota, N_EXPERTS), axis=0, keepdims=True)
        hit = e_iota == idx
        picked = jnp.where(hit, 1.0, picked)
        cur = jnp.where(hit, -jnp.inf, cur)
    w = scores * picked
    return w / jnp.sum(w, axis=0, keepdims=True) * ROUTED_SCALE


def _norm_route_kernel(x_ref, g_ref, sh_ref, sc_ref, rwh_ref, rwl_ref, rb_ref, o_ref, gates_ref):
    h = _normed(x_ref, g_ref, sh_ref, sc_ref)
    h_hi = h.astype(BF16)
    o_ref[...] = h_hi
    h_lo = (h - h_hi.astype(F32)).astype(BF16)
    nt = (((1,), (1,)), ((), ()))
    logits = (lax.dot_general(rwh_ref[...], h_hi, nt, preferred_element_type=F32)
              + lax.dot_general(rwh_ref[...], h_lo, nt, preferred_element_type=F32)
              + lax.dot_general(rwl_ref[...], h_hi, nt, preferred_element_type=F32))
    gates_ref[...] = _route(logits, rb_ref[...])


def norm_route(x, gain, mods, router_w, router_b, n_rows):
    tm = TOK_TILE
    rwt = router_w.T
    rw_hi = rwt.astype(BF16)
    rw_lo = (rwt - rw_hi.astype(F32)).astype(BF16)
    return pl.pallas_call(
        _norm_route_kernel,
        grid=(n_rows // tm,),
        in_specs=[
            pl.BlockSpec((tm, D), lambda i: (i, 0)),
            pl.BlockSpec((1, D), lambda i: (0, 0)),
            _mod_spec(3, tm),
            _mod_spec(4, tm),
            pl.BlockSpec((N_EXPERTS, D), lambda i: (0, 0)),
            pl.BlockSpec((N_EXPERTS, D), lambda i: (0, 0)),
            pl.BlockSpec((N_EXPERTS, 1), lambda i: (0, 0)),
        ],
        out_specs=[pl.BlockSpec((tm, D), lambda i: (i, 0)), pl.BlockSpec((N_EXPERTS, tm), lambda i: (0, i))],
        out_shape=[jax.ShapeDtypeStruct((n_rows, D), BF16), jax.ShapeDtypeStruct((N_EXPERTS, n_rows), F32)],
        compiler_params=_cp(("parallel",)),
        name="norm_route",
    )(x, gain.reshape(1, D), mods, mods, rw_hi, rw_lo, router_b.reshape(N_EXPERTS, 1))


def _head_norm(acc, gain, g):
    a = acc[:, g * HEAD_DIM:(g + 1) * HEAD_DIM]
    ms = jnp.mean(a * a, axis=-1, keepdims=True)
    return a * lax.rsqrt(ms + EPS) * gain[:, g * HEAD_DIM:(g + 1) * HEAD_DIM]


def _proj_kernel(*refs, n_norm_tiles, rope):
    if rope:
        a_ref, w_ref, gain_ref, cos_ref, sin_ref, o_ref = refs
    else:
        a_ref, w_ref, gain_ref, o_ref = refs
    j = pl.program_id(1)
    acc = jnp.dot(a_ref[...], w_ref[...], preferred_element_type=F32)
    tn = acc.shape[1]

    @pl.when(j < n_norm_tiles)
    def _():
        gain = gain_ref[...]
        for g in range(tn // HEAD_DIM):
            y = _head_norm(acc, gain, g)
            if rope:
                y = y * cos_ref[...] + pltpu.roll(y, HEAD_DIM // 2, axis=1) * sin_ref[...]
            o_ref[:, g * HEAD_DIM:(g + 1) * HEAD_DIM] = y.astype(o_ref.dtype)

    @pl.when(j >= n_norm_tiles)
    def _():
        o_ref[...] = acc.astype(o_ref.dtype)


def proj(a, w, out_dtype, *, gain=None, n_norm_cols=0, cos=None, sin=None, tm=512, tn=512):
    m, k = a.shape
    n = w.shape[1]
    rope = cos is not None
    if gain is None:
        gain = jnp.ones((1, n), F32)
    in_specs = [
        pl.BlockSpec((tm, k), lambda i, j: (i, 0)),
        pl.BlockSpec((k, tn), lambda i, j: (0, j)),
        pl.BlockSpec((1, tn), lambda i, j: (0, j)),
    ]
    args = [a, w, gain]
    if rope:
        in_specs += [pl.BlockSpec((tm, HEAD_DIM), lambda i, j: (i, 0))] * 2
        args += [cos, sin]
    return pl.pallas_call(
        functools.partial(_proj_kernel, n_norm_tiles=n_norm_cols // tn, rope=rope),
        grid=(m // tm, n // tn),
        in_specs=in_specs,
        out_specs=pl.BlockSpec((tm, tn), lambda i, j: (i, j)),
        out_shape=jax.ShapeDtypeStruct((m, n), out_dtype),
        compiler_params=_cp(("parallel", "arbitrary")),
        name="proj",
    )(*args)


def _out_proj_kernel(*refs, n_a):
    a_refs = refs[:n_a]
    w_refs = refs[n_a:2 * n_a]
    res_ref, gm_ref, o_ref = refs[2 * n_a:]
    acc = jnp.dot(a_refs[0][...], w_refs[0][...], preferred_element_type=F32)
    for a_ref, w_ref in zip(a_refs[1:], w_refs[1:]):
        acc = acc + jnp.dot(a_ref[...], w_ref[...], preferred_element_type=F32)
    o_ref[...] = res_ref[...] + gm_ref[0] * acc


def out_proj(a_list, w_list, res, mods, which_gate, n_rows, tm=512, tn=512):
    n_a = len(a_list)
    n = w_list[0].shape[1]
    in_specs = [pl.BlockSpec((tm, a.shape[1]), lambda i, j: (i, 0)) for a in a_list]
    in_specs += [pl.BlockSpec((w.shape[0], tn), lambda i, j: (0, j)) for w in w_list]
    in_specs += [
        pl.BlockSpec((tm, tn), lambda i, j: (i, j)),
        pl.BlockSpec((1, 1, tn), lambda i, j: (_seg_of_tile(i, tm) * 6 + which_gate, 0, j)),
    ]
    return pl.pallas_call(
        functools.partial(_out_proj_kernel, n_a=n_a),
        grid=(n_rows // tm, n // tn),
        in_specs=in_specs,
        out_specs=pl.BlockSpec((tm, tn), lambda i, j: (i, j)),
        out_shape=jax.ShapeDtypeStruct((n_rows, n), F32),
        compiler_params=_cp(("parallel", "arbitrary")),
        name="out_proj",
    )(*a_list, *w_list, res, mods)


def na_bias_table(rpb):
    rows = SEQ // GRID_W
    cols = np.arange(GRID_W)
    col_start = np.clip(cols - NA_WIN_W // 2, 0, GRID_W - NA_WIN_W)
    col_ok = (cols[None, :] >= col_start[:, None]) & (cols[None, :] < col_start[:, None] + NA_WIN_W)
    col_idx = np.clip(cols[None, :] - cols[:, None], 1 - NA_WIN_W, NA_WIN_W - 1) + (NA_WIN_W - 1)
    tables = []
    for r0 in (0, 2 * NA_QROWS, rows - NA_QROWS):
        kstart = int(np.clip(r0 - NA_WIN_H // 2, 0, rows - NA_KROWS))
        r = r0 + np.arange(NA_QROWS)
        kr = kstart + np.arange(NA_KROWS)
        win = np.clip(r - NA_WIN_H // 2, 0, rows - NA_WIN_H)
        row_ok = (kr[None, :] >= win[:, None]) & (kr[None, :] < win[:, None] + NA_WIN_H)
        row_idx = np.clip(kr[None, :] - r[:, None] + (NA_WIN_H - 1), 0, 2 * NA_WIN_H - 2)
        ok = row_ok[:, None, :, None] & col_ok[None, :, None, :]
        b = rpb.astype(F32)[:, row_idx[:, None, :, None], col_idx[None, :, None, :]]
        b = jnp.where(ok[None], b, NEG_INF)
        tables.append(b.reshape(NA_HEADS, NA_QROWS * GRID_W, NA_KROWS * GRID_W))
    return jnp.stack(tables, axis=1)


def _softmax_pv(pieces):
    m = functools.reduce(jnp.maximum, [jnp.max(s, axis=-1, keepdims=True) for s, _ in pieces])
    ps = [jnp.exp(s - m) for s, _ in pieces]
    l = functools.reduce(lambda a, b: a + b, [jnp.sum(p, axis=-1, keepdims=True) for p in ps])
    o = functools.reduce(lambda a, b: a + b,
                         [jnp.dot(p.astype(BF16), v, preferred_element_type=F32) for p, (_, v) in zip(ps, pieces)])
    return o / l


_NT_DIMS = (((1,), (1,)), ((), ()))


def _na_kernel(q_ref, k_ref, v_ref, kc_ref, vc_ref, bias_ref, o_ref):
    qb = pl.program_id(2)
    rows = SEQ // GRID_W
    kstart = pl.multiple_of(jnp.clip(qb * NA_QROWS - NA_WIN_H // 2, 0, rows - NA_KROWS) * GRID_W, GRID_W)
    q = q_ref[...]
    kw = k_ref[pl.ds(kstart, NA_KROWS * GRID_W), :]
    vw = v_ref[pl.ds(kstart, NA_KROWS * GRID_W), :]
    s_loc = lax.dot_general(q, kw, _NT_DIMS, preferred_element_type=F32) + bias_ref[0, 0]
    s_ctx = lax.dot_general(q, kc_ref[...], _NT_DIMS, preferred_element_type=F32)
    o_ref[...] = _softmax_pv([(s_loc, vw), (s_ctx, vc_ref[...])]).astype(o_ref.dtype)


def _ctx_attn_kernel(q_ref, k_ref, v_ref, o_ref):
    s = lax.dot_general(q_ref[...], k_ref[...], _NT_DIMS, preferred_element_type=F32)
    o_ref[...] = _softmax_pv([(s, v_ref[...])]).astype(o_ref.dtype)


def neighbourhood_attention(qkv, bias):
    nqb = SEQ // (NA_QROWS * GRID_W)
    tq = NA_QROWS * GRID_W
    ctx_blk = NX // CTX
    lat = pl.pallas_call(
        _na_kernel,
        grid=(BATCH, NA_HEADS, nqb),
        in_specs=[
            pl.BlockSpec((tq, HEAD_DIM), lambda b, h, i: (b * nqb + i, h)),
            pl.BlockSpec((SEQ, HEAD_DIM), lambda b, h, i: (b, NA_HEADS + h)),
            pl.BlockSpec((SEQ, HEAD_DIM), lambda b, h, i: (b, 2 * NA_HEADS + h)),
            pl.BlockSpec((CTX, HEAD_DIM), lambda b, h, i: (ctx_blk + b, NA_HEADS + h)),
            pl.BlockSpec((CTX, HEAD_DIM), lambda b, h, i: (ctx_blk + b, 2 * NA_HEADS + h)),
            pl.BlockSpec((1, 1, tq, NA_KROWS * GRID_W),
                         lambda b, h, i: (h, jnp.where(i == 0, 0, jnp.where(i == nqb - 1, 2, 1)), 0, 0)),
        ],
        out_specs=pl.BlockSpec((tq, HEAD_DIM), lambda b, h, i: (b * nqb + i, h)),
        out_shape=jax.ShapeDtypeStruct((NX, NA_WIDTH), BF16),
        compiler_params=_cp(("parallel", "parallel", "arbitrary")),
        name="na_attn",
    )(qkv, qkv, qkv, qkv, qkv, bias)
    ctx = pl.pallas_call(
        _ctx_attn_kernel,
        grid=(BATCH, NA_HEADS),
        in_specs=[
            pl.BlockSpec((CTX, HEAD_DIM), lambda b, h: (ctx_blk + b, h)),
            pl.BlockSpec((CTX, HEAD_DIM), lambda b, h: (ctx_blk + b, NA_HEADS + h)),
            pl.BlockSpec((CTX, HEAD_DIM), lambda b, h: (ctx_blk + b, 2 * NA_HEADS + h)),
        ],
        out_specs=pl.BlockSpec((CTX, HEAD_DIM), lambda b, h: (b, h)),
        out_shape=jax.ShapeDtypeStruct((BATCH * CTX, NA_WIDTH), BF16),
        compiler_params=_cp(("parallel", "parallel")),
        name="na_ctx_attn",
    )(qkv, qkv, qkv)
    return jnp.concatenate([lat, ctx], axis=0)


_CONV_HALO = 8


def _conv_kernel(prev_ref, cur_ref, next_ref, w_ref, b_ref, cs_ref, o_ref, buf_ref):
    i = pl.program_id(0)
    tm = cur_ref.shape[0]
    tiles_per_seq = SEQ // tm
    n_lat = NX // tm
    first = (i % tiles_per_seq == 0) | (i >= n_lat)
    last = (i % tiles_per_seq == tiles_per_seq - 1) | (i >= n_lat)
    buf_ref[0:_CONV_HALO, :] = prev_ref[...] * jnp.where(first, 0.0, 1.0)
    buf_ref[_CONV_HALO:_CONV_HALO + tm, :] = cur_ref[...]
    buf_ref[_CONV_HALO + tm:, :] = next_ref[...] * jnp.where(last, 0.0, 1.0)
    acc = jnp.zeros(cur_ref.shape, F32) + b_ref[...]
    for j in range(ML_CONV):
        off = _CONV_HALO + j - ML_CONV // 2
        acc = acc + buf_ref[off:off + tm, :] * w_ref[j:j + 1, :]
    o_ref[...] = (_silu(acc) * cs_ref[...]).astype(o_ref.dtype)


def conv_silu(t, w, b, col_scale):
    tm = CTX
    c = t.shape[1]
    hb = tm // _CONV_HALO
    n_halo_blocks = NT // _CONV_HALO
    wp = jnp.zeros((8, c), F32).at[:ML_CONV].set(w)
    return pl.pallas_call(
        _conv_kernel,
        grid=(NT // tm,),
        in_specs=[
            pl.BlockSpec((_CONV_HALO, c), lambda i: (jnp.maximum(i * hb - 1, 0), 0)),
            pl.BlockSpec((tm, c), lambda i: (i, 0)),
            pl.BlockSpec((_CONV_HALO, c), lambda i: (jnp.minimum((i + 1) * hb, n_halo_blocks - 1), 0)),
            pl.BlockSpec((8, c), lambda i: (0, 0)),
            pl.BlockSpec((1, c), lambda i: (0, 0)),
            pl.BlockSpec((1, c), lambda i: (0, 0)),
        ],
        out_specs=pl.BlockSpec((tm, c), lambda i: (i, 0)),
        out_shape=jax.ShapeDtypeStruct((NT, c), BF16),
        scratch_shapes=[pltpu.VMEM((tm + 2 * _CONV_HALO, c), F32)],
        compiler_params=_cp(("parallel",)),
        name="conv_silu",
    )(t, t, t, wp, b.reshape(1, c), col_scale.reshape(1, c))


def _split3(x):
    hi = x.astype(BF16)
    r = x - hi.astype(F32)
    mid = r.astype(BF16)
    lo = (r - mid.astype(F32)).astype(BF16)
    return hi, mid, lo


def _log_sigmoid(x):
    return jnp.minimum(x, 0.0) - jnp.log1p(jnp.exp(-jnp.abs(x)))


def _mlstm_chunk(reverse, q, kt, v_ext, gc, gr, gb_col, gb_row, c_ref, m):
    ln = q.shape[0]
    d = 1 if reverse else 0
    i_col = gc[:, 2 * d:2 * d + 1] + gb_col[:, 2 * d:2 * d + 1]
    f_col = _log_sigmoid(gc[:, 2 * d + 1:2 * d + 2] + gb_col[:, 2 * d + 1:2 * d + 2])
    i_row = gr[2 * d:2 * d + 1, :] + gb_row[2 * d:2 * d + 1, :]
    f_row = _log_sigmoid(gr[2 * d + 1:2 * d + 2, :] + gb_row[2 * d + 1:2 * d + 2, :])
    t_idx = lax.broadcasted_iota(jnp.int32, (ln, ln), 0)
    s_idx = lax.broadcasted_iota(jnp.int32, (ln, ln), 1)
    causal = (s_idx >= t_idx) if reverse else (s_idx <= t_idx)
    tri = jnp.where(causal, 1.0, 0.0).astype(BF16)
    f_col_w = jnp.broadcast_to(f_col, (ln, HEAD_DIM))
    b_col = functools.reduce(lambda a, b: a + b,
                             [jnp.dot(tri, p, preferred_element_type=F32) for p in _split3(f_col_w)])[:, 0:1]
    f_row_w = jnp.broadcast_to(f_row, (16, ln))
    b_row = functools.reduce(lambda a, b: a + b,
                             [lax.dot_general(p, tri, _NT_DIMS, preferred_element_type=F32) for p in _split3(f_row_w)])[0:1, :]
    total = jnp.sum(f_col, axis=0, keepdims=True)

    dmat = jnp.where(causal, b_col - b_row + i_row, -jnp.inf)
    inter = b_col + m
    m_t = jnp.maximum(inter, jnp.max(dmat, axis=-1, keepdims=True))
    w_inter = jnp.exp(inter - m_t)
    s = jnp.dot(q, kt, preferred_element_type=F32) * jnp.exp(dmat - m_t)
    numden = (w_inter * jnp.dot(q, c_ref[...].astype(BF16), preferred_element_type=F32)
              + jnp.dot(s.astype(BF16), v_ext, preferred_element_type=F32))
    den = numden[:, HEAD_DIM:HEAD_DIM + 1]
    h = numden[:, :HEAD_DIM] / jnp.maximum(jnp.abs(den), jnp.exp(-m_t))

    g = total - b_col + i_col
    m_new = jnp.maximum(total + m, jnp.max(g, axis=0, keepdims=True))
    decay = jnp.exp(total + m - m_new)
    wg = jnp.exp(g - m_new)
    upd = jnp.dot(kt, (wg * v_ext.astype(F32)).astype(BF16), preferred_element_type=F32)
    c_ref[...] = decay * c_ref[...] + upd
    return h, m_new


def _mlstm_kernel(q_ref, kt_ref, v_ref, o_ref, qc_ref, ktc_ref, vc_ref, oc_ref, gc_ref, gr_ref, gbc_ref, gbr_ref,
                  hn_ref, out_ref, outc_ref, cf_ref, cb_ref, hf_ref, hb_ref):
    ln = ML_CHUNK
    n_chunks = SEQ // ln
    ones_col = jnp.where(lax.broadcasted_iota(jnp.int32, (ln, HEAD_DIM), 1) == 0, 1.0, 0.0).astype(BF16)
    gb_col = gbc_ref[0]
    gb_row = gbr_ref[0]

    def v_ext(v):
        return jnp.concatenate([v, ones_col], axis=1)

    cf_ref[...] = jnp.zeros(cf_ref.shape, F32)
    cb_ref[...] = jnp.zeros(cb_ref.shape, F32)
    m0 = jnp.zeros((1, 1), F32)

    gc = gc_ref[0, 0:ln, :]
    gr = gr_ref[0, :, 0:ln]
    vx = v_ext(vc_ref[...])
    hf, mf = _mlstm_chunk(False, qc_ref[...], ktc_ref[...], vx, gc, gr, gb_col, gb_row, cf_ref, m0)
    hb, mb = _mlstm_chunk(True, qc_ref[...], ktc_ref[...], vx, gc, gr, gb_col, gb_row, cb_ref, m0)

    def finish(h, o_gate, hn):
        ms = jnp.mean(h * h, axis=-1, keepdims=True)
        return (h * lax.rsqrt(ms + EPS) * hn * jax.nn.sigmoid(o_gate.astype(F32))).astype(BF16)

    hn = hn_ref[...]
    outc_ref[...] = finish(hf + hb, oc_ref[...], hn)

    def body(c, carry):
        mf, mb = carry
        for reverse, c_ref, h_ref in ((False, cf_ref, hf_ref), (True, cb_ref, hb_ref)):
            cc = (n_chunks - 1 - c) if reverse else c
            r0 = pl.multiple_of(cc * ln, ln)
            g0 = pl.multiple_of(cc * ln + CTX, ln)
            h, m_new = _mlstm_chunk(reverse, q_ref[pl.ds(r0, ln), :], kt_ref[:, pl.ds(r0, ln)],
                                    v_ext(v_ref[pl.ds(r0, ln), :]), gc_ref[0, pl.ds(g0, ln), :],
                                    gr_ref[0, :, pl.ds(g0, ln)], gb_col, gb_row, c_ref, mb if reverse else mf)
            h_ref[pl.ds(r0, ln), :] = h
            if reverse:
                mb = m_new
            else:
                mf = m_new
        return mf, mb

    lax.fori_loop(0, n_chunks, body, (mf, mb))

    def fin_body(c, _):
        r0 = pl.multiple_of(c * ln, ln)
        out_ref[pl.ds(r0, ln), :] = finish(hf_ref[pl.ds(r0, ln), :] + hb_ref[pl.ds(r0, ln), :],
                                           o_ref[pl.ds(r0, ln), :], hn)
        return 0

    lax.fori_loop(0, n_chunks, fin_body, 0)


def mlstm(qk, kt, vo, gates_col, gates_row, gate_b, hnorm):
    hh = ML_HEADS
    ctx_blk = NX // CTX
    tot = CTX + SEQ
    gb = gate_b.astype(F32).transpose(2, 0, 1).reshape(hh, 4)
    gb_col = jnp.zeros((hh, 1, HEAD_DIM), F32).at[:, 0, :4].set(gb)
    gb_row = jnp.zeros((hh, 8, 1), F32).at[:, :4, 0].set(gb)
    lat, ctx = pl.pallas_call(
        _mlstm_kernel,
        grid=(BATCH, hh),
        in_specs=[
            pl.BlockSpec((SEQ, HEAD_DIM), lambda b, h: (b, h)),
            pl.BlockSpec((HEAD_DIM, SEQ), lambda b, h: (h, b)),
            pl.BlockSpec((SEQ, HEAD_DIM), lambda b, h: (b, h)),
            pl.BlockSpec((SEQ, HEAD_DIM), lambda b, h: (b, hh + h)),
            pl.BlockSpec((CTX, HEAD_DIM), lambda b, h: (ctx_blk + b, h)),
            pl.BlockSpec((HEAD_DIM, CTX), lambda b, h: (h, ctx_blk + b)),
            pl.BlockSpec((CTX, HEAD_DIM), lambda b, h: (ctx_blk + b, h)),
            pl.BlockSpec((CTX, HEAD_DIM), lambda b, h: (ctx_blk + b, hh + h)),
            pl.BlockSpec((1, tot, HEAD_DIM), lambda b, h: (b * hh + h, 0, 0)),
            pl.BlockSpec((1, 8, tot), lambda b, h: (b * hh + h, 0, 0)),
            pl.BlockSpec((1, 1, HEAD_DIM), lambda b, h: (h, 0, 0)),
            pl.BlockSpec((1, 8, 1), lambda b, h: (h, 0, 0)),
            pl.BlockSpec((1, HEAD_DIM), lambda b, h: (0, h)),
        ],
        out_specs=[pl.BlockSpec((SEQ, HEAD_DIM), lambda b, h: (b, h)), pl.BlockSpec((CTX, HEAD_DIM), lambda b, h: (b, h))],
        out_shape=[jax.ShapeDtypeStruct((NX, ML_WIDTH), BF16), jax.ShapeDtypeStruct((BATCH * CTX, ML_WIDTH), BF16)],
        scratch_shapes=[
            pltpu.VMEM((HEAD_DIM, 2 * HEAD_DIM), F32),
            pltpu.VMEM((HEAD_DIM, 2 * HEAD_DIM), F32),
            pltpu.VMEM((SEQ, HEAD_DIM), F32),
            pltpu.VMEM((SEQ, HEAD_DIM), F32),
        ],
        compiler_params=_cp(("parallel", "parallel")),
        name="mlstm",
    )(qk, kt, vo, vo, qk, kt, vo, vo, gates_col, gates_row, gb_col, gb_row, hnorm.reshape(1, ML_WIDTH))
    return jnp.concatenate([lat, ctx], axis=0)


def _diff_attn_kernel(q0_ref, q1_ref, k0_ref, k1_ref, k0c_ref, k1c_ref, v_ref, vc_ref, lam_ref, sub_ref, o_ref,
                      *, lambda_init):
    lam = lam_ref[...]
    lam_full = (jnp.exp(jnp.sum(lam[0:1] * lam[1:2], axis=-1, keepdims=True))
                - jnp.exp(jnp.sum(lam[2:3] * lam[3:4], axis=-1, keepdims=True)) + lambda_init)

    def probs(q_ref, k_ref, kc_ref):
        q = q_ref[...]
        s_lat = lax.dot_general(q, k_ref[...], _NT_DIMS, preferred_element_type=F32)
        s_ctx = lax.dot_general(q, kc_ref[...], _NT_DIMS, preferred_element_type=F32)
        m = jnp.maximum(jnp.max(s_lat, axis=-1, keepdims=True), jnp.max(s_ctx, axis=-1, keepdims=True))
        p_lat = jnp.exp(s_lat - m)
        p_ctx = jnp.exp(s_ctx - m)
        inv = 1.0 / (jnp.sum(p_lat, axis=-1, keepdims=True) + jnp.sum(p_ctx, axis=-1, keepdims=True))
        return p_lat, p_ctx, inv

    p0_lat, p0_ctx, inv0 = probs(q0_ref, k0_ref, k0c_ref)
    p1_lat, p1_ctx, inv1 = probs(q1_ref, k1_ref, k1c_ref)
    c1 = lam_full * inv1
    a_lat = (p0_lat * inv0 - p1_lat * c1).astype(BF16)
    a_ctx = (p0_ctx * inv0 - p1_ctx * c1).astype(BF16)
    o = (jnp.dot(a_lat, v_ref[...], preferred_element_type=F32)
         + jnp.dot(a_ctx, vc_ref[...], preferred_element_type=F32))
    ms = jnp.mean(o * o, axis=-1, keepdims=True)
    o_ref[...] = (o * lax.rsqrt(ms + EPS) * sub_ref[...] * (1.0 - lambda_init)).astype(o_ref.dtype)


def diff_attention(qkv, lam, subln, lambda_init, tq=256):
    nq = SEQ // tq
    ctx_blk = NX // CTX
    kcol = 2 * DA_HEADS
    vcol = 2 * DA_HEADS
    return pl.pallas_call(
        functools.partial(_diff_attn_kernel, lambda_init=lambda_init),
        grid=(BATCH, DA_HEADS, nq),
        in_specs=[
            pl.BlockSpec((tq, HEAD_DIM), lambda b, h, i: (b * nq + i, 2 * h)),
            pl.BlockSpec((tq, HEAD_DIM), lambda b, h, i: (b * nq + i, 2 * h + 1)),
            pl.BlockSpec((SEQ, HEAD_DIM), lambda b, h, i: (b, kcol + 2 * h)),
            pl.BlockSpec((SEQ, HEAD_DIM), lambda b, h, i: (b, kcol + 2 * h + 1)),
            pl.BlockSpec((CTX, HEAD_DIM), lambda b, h, i: (ctx_blk + b, kcol + 2 * h)),
            pl.BlockSpec((CTX, HEAD_DIM), lambda b, h, i: (ctx_blk + b, kcol + 2 * h + 1)),
            pl.BlockSpec((SEQ, DA_VDIM), lambda b, h, i: (b, vcol + h)),
            pl.BlockSpec((CTX, DA_VDIM), lambda b, h, i: (ctx_blk + b, vcol + h)),
            pl.BlockSpec((4, HEAD_DIM), lambda b, h, i: (0, 0)),
            pl.BlockSpec((1, DA_VDIM), lambda b, h, i: (0, 0)),
        ],
        out_specs=pl.BlockSpec((tq, DA_VDIM), lambda b, h, i: (b * nq + i, h)),
        out_shape=jax.ShapeDtypeStruct((NX, DA_HEADS * DA_VDIM), BF16),
        compiler_params=_cp(("parallel", "parallel", "arbitrary"), vmem_mb=56),
        name="diff_attn",
    )(qkv, qkv, qkv, qkv, qkv, qkv, qkv, qkv, lam.astype(F32), subln.reshape(1, DA_VDIM).astype(F32))


def dispatch_tables(gates_t, n_rows):
    nt = n_rows // TOK_TILE
    max_units = (n_rows * TOP_K + nt * N_EXPERTS * (UNIT - 1)) // UNIT
    n_ffn_tiles = (max_units + N_EXPERTS * (FFN_UNITS - 1)) // FFN_UNITS + 1
    gates = gates_t.T.reshape(nt, TOK_TILE, N_EXPERTS)
    sel = gates > 0
    seli = sel.astype(jnp.int32)
    cnt = seli.sum(axis=1)
    nun = (cnt + UNIT - 1) // UNIT
    loc_off = jnp.cumsum(nun, axis=1) - nun
    rank = jnp.cumsum(seli, axis=1) - seli
    slot = loc_off[:, None, :] * UNIT + rank
    kr = jnp.cumsum(seli, axis=2) - seli
    slot6, w6 = [], []
    for k in range(TOP_K):
        hit = sel & (kr == k)
        slot6.append(jnp.where(hit.any(axis=2), jnp.sum(jnp.where(hit, slot, 0), axis=2), -1))
        w6.append(jnp.sum(jnp.where(hit, gates, 0.0), axis=2))
    pad = [jnp.full((nt, TOK_TILE), -1, jnp.int32)] * (8 - TOP_K)
    slot6 = jnp.stack(slot6 + pad, axis=-1)
    w6 = jnp.stack(w6 + [jnp.zeros((nt, TOK_TILE), F32)] * (8 - TOP_K), axis=-1)

    seg_un = nun.sum(axis=0)
    seg_pad = (seg_un + FFN_UNITS - 1) // FFN_UNITS * FFN_UNITS
    seg_end = jnp.cumsum(seg_pad)
    seg_start = seg_end - seg_pad
    gstart = seg_start[None, :] + jnp.cumsum(nun, axis=0) - nun
    u = jnp.arange(UNITS_PER_TILE, dtype=jnp.int32)
    loc_end = loc_off + nun
    ue = (loc_end[:, None, :] <= u[None, :, None]).sum(axis=-1)
    onehot = ue[:, :, None] == jnp.arange(N_EXPERTS)[None, None, :]
    dst = jnp.sum(jnp.where(onehot, (gstart - loc_off)[:, None, :], 0), axis=-1) + u[None, :]
    dst = jnp.where(ue < N_EXPERTS, dst, -1).astype(jnp.int32)

    n_units_total = n_ffn_tiles * FFN_UNITS
    flat_dst = dst.reshape(-1)
    src = jnp.full((n_units_total + 1,), -1, jnp.int32).at[jnp.where(flat_dst >= 0, flat_dst, n_units_total)].set(
        jnp.arange(nt * UNITS_PER_TILE, dtype=jnp.int32))[:n_units_total]
    mt = jnp.arange(n_ffn_tiles, dtype=jnp.int32) * FFN_UNITS
    tile_expert = jnp.minimum((seg_end[None, :] <= mt[:, None]).sum(axis=-1), N_EXPERTS - 1).astype(jnp.int32)
    n_used = (seg_end[-1] // FFN_UNITS).astype(jnp.int32).reshape(1)
    return dict(slot6=slot6, w6=w6, slot6_t=slot6.transpose(0, 2, 1), dst=dst.reshape(-1), src=src,
                tile_expert=tile_expert, n_used=n_used, n_ffn_tiles=n_ffn_tiles, nt=nt)


def _moe_gather_kernel(h_ref, slot_ref, o_ref):
    s_iota = lax.broadcasted_iota(jnp.int32, (SLOTS, TOK_TILE), 0)
    slots = slot_ref[0]
    p = jnp.zeros((SLOTS, TOK_TILE), F32)
    for k in range(TOP_K):
        p = jnp.where(s_iota == slots[k:k + 1, :], 1.0, p)
    p = p.astype(BF16)
    o_ref[0] = jnp.dot(p, h_ref[...], preferred_element_type=F32).astype(BF16)


def moe_gather(h, slot6_t, nt):
    return pl.pallas_call(
        _moe_gather_kernel,
        grid=(nt,),
        in_specs=[pl.BlockSpec((TOK_TILE, D), lambda i: (i, 0)), pl.BlockSpec((1, 8, TOK_TILE), lambda i: (i, 0, 0))],
        out_specs=pl.BlockSpec((1, SLOTS, D), lambda i: (i, 0, 0)),
        out_shape=jax.ShapeDtypeStruct((nt, SLOTS, D), BF16),
        compiler_params=_cp(("parallel",)),
        name="moe_gather",
    )(h, slot6_t)


def _unit_copy(src_hbm, buf_ref, sem_ref, slot, src_unit, j):
    return pltpu.make_async_copy(src_hbm.at[pl.ds(pl.multiple_of(src_unit * UNIT, UNIT), UNIT)],
                                 buf_ref.at[slot, pl.ds(j * UNIT, UNIT)], sem_ref.at[slot])


def _fetch_units(table_ref, base, n_units, src_hbm, buf_ref, sem_ref, slot, start):
    def body(j, _):
        su = table_ref[base + j]

        @pl.when(su >= 0)
        def _():
            cp = _unit_copy(src_hbm, buf_ref, sem_ref, slot, su, j)
            if start:
                cp.start()
            else:
                cp.wait()

        return 0

    lax.fori_loop(0, n_units, body, 0)


def _zero_missing_units(table_ref, base, n_units, buf_ref, slot):
    def body(j, _):
        @pl.when(table_ref[base + j] < 0)
        def _():
            buf_ref[slot, pl.ds(pl.multiple_of(j * UNIT, UNIT), UNIT), :] = jnp.zeros((UNIT, D), BF16)

        return 0

    lax.fori_loop(0, n_units, body, 0)


def _moe_ffn_kernel(src_ref, te_ref, nu_ref, x_hbm, wg_ref, wu_ref, wd_ref, o_ref, xbuf_ref, sem_ref):
    m = pl.program_id(0)
    n_used = nu_ref[0]
    slot = m % 2

    @pl.when(m == 0)
    def _():
        _fetch_units(src_ref, 0, FFN_UNITS, x_hbm, xbuf_ref, sem_ref, 0, True)

    @pl.when(m + 1 < n_used)
    def _():
        _fetch_units(src_ref, (m + 1) * FFN_UNITS, FFN_UNITS, x_hbm, xbuf_ref, sem_ref, 1 - slot, True)

    @pl.when(m < n_used)
    def _():
        _fetch_units(src_ref, m * FFN_UNITS, FFN_UNITS, x_hbm, xbuf_ref, sem_ref, slot, False)
        _zero_missing_units(src_ref, m * FFN_UNITS, FFN_UNITS, xbuf_ref, slot)
        x = xbuf_ref[slot]
        g = jnp.dot(x, wg_ref[0], preferred_element_type=F32)
        u = jnp.dot(x, wu_ref[0], preferred_element_type=F32)
        a = (_silu(g) * u).astype(BF16)
        o_ref[...] = jnp.dot(a, wd_ref[0], preferred_element_type=F32).astype(o_ref.dtype)

    @pl.when(m >= n_used)
    def _():
        o_ref[...] = jnp.zeros(o_ref.shape, o_ref.dtype)


def moe_ffn(x_tiles, tabs, wg, wu, wd):
    n_tiles = tabs["n_ffn_tiles"]
    x_flat = x_tiles.reshape(-1, D)
    grid_spec = pltpu.PrefetchScalarGridSpec(
        num_scalar_prefetch=3,
        grid=(n_tiles,),
        in_specs=[
            pl.BlockSpec(memory_space=pl.ANY),
            pl.BlockSpec((1, D, D_EXPERT), lambda m, src, te, nu: (te[m], 0, 0)),
            pl.BlockSpec((1, D, D_EXPERT), lambda m, src, te, nu: (te[m], 0, 0)),
            pl.BlockSpec((1, D_EXPERT, D), lambda m, src, te, nu: (te[m], 0, 0)),
        ],
        out_specs=pl.BlockSpec((FFN_TM, D), lambda m, src, te, nu: (m, 0)),
        scratch_shapes=[pltpu.VMEM((2, FFN_TM, D), BF16), pltpu.SemaphoreType.DMA((2,))],
    )
    return pl.pallas_call(
        _moe_ffn_kernel,
        grid_spec=grid_spec,
        out_shape=jax.ShapeDtypeStruct((n_tiles * FFN_TM, D), BF16),
        compiler_params=_cp(("arbitrary",)),
        name="moe_ffn",
    )(tabs["src"], tabs["tile_expert"], tabs["n_used"], x_flat, wg, wu, wd)


def _shared_ffn_kernel(a_ref, wg_ref, wu_ref, wd_ref, o_ref):
    a = a_ref[...]
    g = jnp.dot(a, wg_ref[...], preferred_element_type=F32)
    u = jnp.dot(a, wu_ref[...], preferred_element_type=F32)
    o_ref[...] = jnp.dot((_silu(g) * u).astype(BF16), wd_ref[...], preferred_element_type=F32).astype(o_ref.dtype)


def shared_ffn(h, wg, wu, wd, n_rows, tm=512):
    return pl.pallas_call(
        _shared_ffn_kernel,
        grid=(n_rows // tm,),
        in_specs=[
            pl.BlockSpec((tm, D), lambda i: (i, 0)),
            pl.BlockSpec((D, D_EXPERT), lambda i: (0, 0)),
            pl.BlockSpec((D, D_EXPERT), lambda i: (0, 0)),
            pl.BlockSpec((D_EXPERT, D), lambda i: (0, 0)),
        ],
        out_specs=pl.BlockSpec((tm, D), lambda i: (i, 0)),
        out_shape=jax.ShapeDtypeStruct((n_rows, D), BF16),
        compiler_params=_cp(("parallel",)),
        name="shared_ffn",
    )(h, wg, wu, wd)


def _moe_combine_kernel(dst_ref, y_hbm, slot_ref, w_ref, sh_ref, res_ref, gm_ref, o_ref, ybuf_ref, sem_ref):
    i = pl.program_id(0)
    n = pl.num_programs(0)
    slot = i % 2

    @pl.when(i == 0)
    def _():
        _fetch_units(dst_ref, 0, UNITS_PER_TILE, y_hbm, ybuf_ref, sem_ref, 0, True)

    @pl.when(i + 1 < n)
    def _():
        _fetch_units(dst_ref, (i + 1) * UNITS_PER_TILE, UNITS_PER_TILE, y_hbm, ybuf_ref, sem_ref, 1 - slot, True)

    _fetch_units(dst_ref, i * UNITS_PER_TILE, UNITS_PER_TILE, y_hbm, ybuf_ref, sem_ref, slot, False)
    _zero_missing_units(dst_ref, i * UNITS_PER_TILE, UNITS_PER_TILE, ybuf_ref, slot)
    lane = lax.broadcasted_iota(jnp.int32, (TOK_TILE, SLOTS), 1)
    slots = slot_ref[0]
    w = w_ref[0]
    pw = jnp.zeros((TOK_TILE, SLOTS), F32)
    for k in range(TOP_K):
        pw = jnp.where(lane == slots[:, k:k + 1], w[:, k:k + 1], pw)
    routed = jnp.dot(pw.astype(BF16), ybuf_ref[slot], preferred_element_type=F32)
    o_ref[...] = res_ref[...] + gm_ref[0] * (routed + sh_ref[...].astype(F32))


def moe_combine(y_sorted, tabs, shared, res, mods, n_rows):
    nt = tabs["nt"]
    tm = TOK_TILE
    grid_spec = pltpu.PrefetchScalarGridSpec(
        num_scalar_prefetch=1,
        grid=(nt,),
        in_specs=[
            pl.BlockSpec(memory_space=pl.ANY),
            pl.BlockSpec((1, tm, 8), lambda i, dst: (i, 0, 0)),
            pl.BlockSpec((1, tm, 8), lambda i, dst: (i, 0, 0)),
            pl.BlockSpec((tm, D), lambda i, dst: (i, 0)),
            pl.BlockSpec((tm, D), lambda i, dst: (i, 0)),
            pl.BlockSpec((1, 1, D), lambda i, dst: (_seg_of_tile(i, tm) * 6 + 5, 0, 0)),
        ],
        out_specs=pl.BlockSpec((tm, D), lambda i, dst: (i, 0)),
        scratch_shapes=[pltpu.VMEM((2, SLOTS, D), BF16), pltpu.SemaphoreType.DMA((2,))],
    )
    return pl.pallas_call(
        _moe_combine_kernel,
        grid_spec=grid_spec,
        out_shape=jax.ShapeDtypeStruct((n_rows, D), F32),
        compiler_params=_cp(("arbitrary",), vmem_mb=56),
        name="moe_combine",
    )(tabs["dst"], y_sorted, tabs["slot6"], tabs["w6"], shared, res, mods)


def moe_block(xa, gain, mods, router_w, router_b, wg, wu, wd, sg, su, sd, n_rows):
    h, gates_t = norm_route(xa, gain, mods, router_w, router_b, n_rows)
    tabs = dispatch_tables(gates_t, n_rows)
    x_tiles = moe_gather(h, tabs["slot6_t"], tabs["nt"])
    y_sorted = moe_ffn(x_tiles, tabs, wg.astype(BF16), wu.astype(BF16), wd.astype(BF16))
    shared = shared_ffn(h, sg.astype(BF16), su.astype(BF16), sd.astype(BF16), n_rows)
    return moe_combine(y_sorted, tabs, shared, xa, mods, n_rows)


def rope_tables():
    t = jnp.arange(SEQ)
    row = (t // GRID_W).astype(F32)
    col = (t % GRID_W).astype(F32)
    n_freq = HEAD_DIM // 4
    inv_freq = ROPE_THETA ** (-jnp.arange(n_freq, dtype=F32) / n_freq)
    ang = jnp.concatenate([row[:, None] * inv_freq, col[:, None] * inv_freq], axis=-1)
    ang = jnp.concatenate([ang, ang], axis=-1)
    sign = jnp.where(jnp.arange(HEAD_DIM) < HEAD_DIM // 2, -1.0, 1.0)
    cos = jnp.concatenate([jnp.cos(ang)] * BATCH + [jnp.ones((BATCH * CTX, HEAD_DIM), F32)], axis=0)
    sin = jnp.concatenate([jnp.sin(ang) * sign] * BATCH + [jnp.zeros((BATCH * CTX, HEAD_DIM), F32)], axis=0)
    return cos, sin


def even_layer(xa, mods, norm1, w_in, na_qnorm, na_knorm, na_rpb, conv_w, conv_b, gate_b, hnorm, w_out):
    h = norm_mod(xa, norm1, mods, 0, NT)
    w = w_in.astype(BF16)
    c0 = 3 * NA_WIDTH
    c1 = c0 + 2 * ML_WIDTH
    c2 = c1 + 2 * ML_WIDTH
    scale = HEAD_DIM ** -0.5
    gain = jnp.concatenate([jnp.tile(na_qnorm.astype(F32) * scale, NA_HEADS), jnp.tile(na_knorm.astype(F32), NA_HEADS),
                            jnp.ones((NA_WIDTH,), F32)]).reshape(1, c0)
    qkv = proj(h, w[:, :c0], BF16, gain=gain, n_norm_cols=2 * NA_WIDTH)
    na_h = neighbourhood_attention(qkv, na_bias_table(na_rpb))

    ml_qk = proj(h, w[:, c0:c1], F32)
    ml_vo = proj(h, w[:, c1:c2], BF16)
    n_gate = w_in.shape[1] - c2
    w_gate = jnp.zeros((D, HEAD_DIM), BF16).at[:, :n_gate].set(w[:, c2:])
    g = proj(h, w_gate, F32, tn=HEAD_DIM)[:, :n_gate]
    col_scale = jnp.concatenate([jnp.ones((ML_WIDTH,), F32), jnp.full((ML_WIDTH,), HEAD_DIM ** -0.5, F32)])
    qk = conv_silu(ml_qk, conv_w.astype(F32), conv_b.astype(F32), col_scale)
    kt = qk[:, ML_WIDTH:].T
    g = g.reshape(NT, 4, ML_HEADS)
    g = jnp.concatenate([g[NX:].reshape(BATCH, CTX, 4, ML_HEADS), g[:NX].reshape(BATCH, SEQ, 4, ML_HEADS)], axis=1)
    g = g.transpose(0, 3, 1, 2).reshape(BATCH * ML_HEADS, CTX + SEQ, 4)
    g_col = jnp.zeros((BATCH * ML_HEADS, CTX + SEQ, HEAD_DIM), F32).at[:, :, :4].set(g)
    g_row = jnp.zeros((BATCH * ML_HEADS, 8, CTX + SEQ), F32).at[:, :4, :].set(g.transpose(0, 2, 1))
    ml_h = mlstm(qk, kt, ml_vo, g_col, g_row, gate_b, hnorm.astype(F32))

    wo = w_out.astype(BF16)
    return out_proj([na_h, ml_h], [wo[:NA_WIDTH], wo[NA_WIDTH:]], xa, mods, 2, NT)


def odd_layer(xa, mods, norm1, w_in, qnorm, knorm, lam, subln, w_out, lambda_init):
    h = norm_mod(xa, norm1, mods, 0, NT)
    scale = HEAD_DIM ** -0.5
    n_qk = 2 * DA_HEADS * HEAD_DIM
    gain = jnp.concatenate([jnp.tile(qnorm.astype(F32) * scale, 2 * DA_HEADS), jnp.tile(knorm.astype(F32), 2 * DA_HEADS),
                            jnp.ones((DA_HEADS * DA_VDIM,), F32)]).reshape(1, -1)
    cos, sin = rope_tables()
    qkv = proj(h, w_in.astype(BF16), BF16, gain=gain, n_norm_cols=2 * n_qk, cos=cos, sin=sin)
    o = diff_attention(qkv, lam, subln, lambda_init)
    return out_proj([o], [w_out.astype(BF16)], xa, mods, 2, NX)


def diff_lambda_init(layer):
    return 0.8 - 0.6 * math.exp(-0.3 * layer)


def kernel(x, c, ctx, c_ctx, l0_ada_w, l0_ada_b, l0_norm1, l0_norm2, l0_w_in, l0_na_qnorm, l0_na_knorm, l0_na_rpb, l0_ml_conv_w, l0_ml_conv_b, l0_ml_gate_b, l0_ml_hnorm, l0_w_out, l0_router_w, l0_router_b, l0_exp_gate, l0_exp_up, l0_exp_down, l0_sh_gate, l0_sh_up, l0_sh_down, l1_ada_w, l1_ada_b, l1_norm1, l1_norm2, l1_w_in, l1_qnorm, l1_knorm, l1_lambda, l1_subln, l1_w_out, l1_router_w, l1_router_b, l1_exp_gate, l1_exp_up, l1_exp_down, l1_sh_gate, l1_sh_up, l1_sh_down):
    assert x.shape == (BATCH, SEQ, D) and ctx.shape == (BATCH, CTX, D)
    xa = jnp.concatenate([x.reshape(NX, D), ctx.reshape(BATCH * CTX, D)], axis=0).astype(F32)
    cvec = jnp.zeros((8, D), F32).at[:BATCH].set(c).at[BATCH].set(c_ctx)

    mods0 = adaln(cvec, l0_ada_w, l0_ada_b)
    xa = even_layer(xa, mods0, l0_norm1, l0_w_in, l0_na_qnorm, l0_na_knorm, l0_na_rpb, l0_ml_conv_w, l0_ml_conv_b,
                    l0_ml_gate_b, l0_ml_hnorm, l0_w_out)
    xa = moe_block(xa, l0_norm2, mods0, l0_router_w, l0_router_b, l0_exp_gate, l0_exp_up, l0_exp_down,
                   l0_sh_gate, l0_sh_up, l0_sh_down, NT)

    mods1 = adaln(cvec, l1_ada_w, l1_ada_b)
    xl = odd_layer(xa, mods1, l1_norm1, l1_w_in, l1_qnorm, l1_knorm, l1_lambda, l1_subln, l1_w_out, diff_lambda_init(1))
    xl = moe_block(xl, l1_norm2, mods1, l1_router_w, l1_router_b, l1_exp_gate, l1_exp_up, l1_exp_down,
                   l1_sh_gate, l1_sh_up, l1_sh_down, NX)
    return xl.reshape(BATCH, SEQ, D)
```

```python
import functools
import math

import jax
import jax.numpy as jnp
import numpy as np
from jax import lax
from jax.experimental import pallas as pl
from jax.experimental.pallas import tpu as pltpu

F32 = jnp.float32
BF16 = jnp.bfloat16

D = 2048
BATCH = 2
SEQ = 4096
CTX = 256
NX = BATCH * SEQ
NT = NX + BATCH * CTX
GRID_W = 64
EPS = 1e-6
NEG_INF = -1e30

NA_HEADS = 8
HEAD_DIM = 128
NA_WIDTH = NA_HEADS * HEAD_DIM
NA_WIN_H = 8
NA_WIN_W = 16
NA_QROWS = 4
NA_KROWS = 12
ML_HEADS = 8
ML_WIDTH = ML_HEADS * HEAD_DIM
ML_CONV = 5
ML_CHUNK = 256

DA_HEADS = 8
DA_VDIM = 256
DA_KCHUNK = 1024
ROPE_THETA = 10000.0

N_EXPERTS = 64
N_GROUPS = 8
TOPK_GROUPS = 4
TOP_K = 6
D_EXPERT = 512
ROUTED_SCALE = 2.5

TOK_TILE = 256
UNIT = 16
UNITS_PER_TILE = (TOK_TILE * TOP_K + N_EXPERTS * (UNIT - 1)) // UNIT + 1
UNITS_PER_TILE = -(-UNITS_PER_TILE // 32) * 32
SLOTS = UNITS_PER_TILE * UNIT
FFN_TM = 512
FFN_UNITS = FFN_TM // UNIT

V7X_VMEM_BYTES = 64 * 1024 * 1024


def _cp(sem, vmem_mb=48):
    assert vmem_mb * 1024 * 1024 < V7X_VMEM_BYTES
    return pltpu.CompilerParams(dimension_semantics=sem, vmem_limit_bytes=vmem_mb * 1024 * 1024)


def _silu(x):
    return x * jax.nn.sigmoid(x)


def _seg_of_tile(i, tm):
    return (i * tm) // SEQ


def _ada_kernel(c_ref, w_ref, b_ref, o_ref):
    s = _silu(c_ref[...]).astype(BF16)
    o_ref[...] = jnp.dot(s, w_ref[...].astype(BF16), preferred_element_type=F32) + b_ref[...]


def adaln(cvec, w, b):
    n = w.shape[1]
    tn = 1024
    out = pl.pallas_call(
        _ada_kernel,
        grid=(n // tn,),
        in_specs=[
            pl.BlockSpec((8, D), lambda j: (0, 0)),
            pl.BlockSpec((D, tn), lambda j: (0, j)),
            pl.BlockSpec((1, tn), lambda j: (0, j)),
        ],
        out_specs=pl.BlockSpec((8, tn), lambda j: (0, j)),
        out_shape=jax.ShapeDtypeStruct((8, n), F32),
        compiler_params=_cp(("arbitrary",)),
        name="adaln",
    )(cvec, w, b.reshape(1, n))
    return out[:3].reshape(18, 1, D)


def _normed(x_ref, g_ref, sh_ref, sc_ref):
    x = x_ref[...]
    ms = jnp.mean(x * x, axis=-1, keepdims=True)
    y = x * lax.rsqrt(ms + EPS) * g_ref[...]
    return y * (1.0 + sc_ref[0]) + sh_ref[0]


def _norm_mod_kernel(x_ref, g_ref, sh_ref, sc_ref, o_ref):
    o_ref[...] = _normed(x_ref, g_ref, sh_ref, sc_ref).astype(o_ref.dtype)


def _mod_spec(which, tm):
    return pl.BlockSpec((1, 1, D), lambda i: (_seg_of_tile(i, tm) * 6 + which, 0, 0))


def norm_mod(x, gain, mods, which_shift, n_rows, tm=256):
    return pl.pallas_call(
        _norm_mod_kernel,
        grid=(n_rows // tm,),
        in_specs=[
            pl.BlockSpec((tm, D), lambda i: (i, 0)),
            pl.BlockSpec((1, D), lambda i: (0, 0)),
            _mod_spec(which_shift, tm),
            _mod_spec(which_shift + 1, tm),
        ],
        out_specs=pl.BlockSpec((tm, D), lambda i: (i, 0)),
        out_shape=jax.ShapeDtypeStruct((n_rows, D), BF16),
        compiler_params=_cp(("parallel",)),
        name="norm_mod",
    )(x, gain.reshape(1, D), mods, mods)


def _route(logits, bias_col):
    tm = logits.shape[1]
    per_group = N_EXPERTS // N_GROUPS
    scores = jax.nn.sigmoid(logits)
    sel = scores + bias_col
    row8 = lax.broadcasted_iota(jnp.int32, (per_group, tm), 0)
    grp = jnp.zeros((N_GROUPS, tm), F32)
    for g in range(N_GROUPS):
        slab = sel[g * per_group:(g + 1) * per_group, :]
        m1 = jnp.max(slab, axis=0, keepdims=True)
        first = jnp.min(jnp.where(slab == m1, row8, per_group), axis=0, keepdims=True)
        m2 = jnp.max(jnp.where(row8 == first, -jnp.inf, slab), axis=0, keepdims=True)
        grp = jnp.where(row8 == g, m1 + m2, grp)
    rank = jnp.zeros((N_GROUPS, tm), jnp.int32)
    for g in range(N_GROUPS):
        vg = grp[g:g + 1, :]
        beats = (vg > grp) | ((vg == grp) & (g < row8))
        rank = rank + jnp.where(beats, 1, 0)
    keep = jnp.where(rank < TOPK_GROUPS, 1.0, 0.0)
    cur = jnp.concatenate(
        [jnp.where(keep[g:g + 1, :] > 0.5, sel[g * per_group:(g + 1) * per_group, :], NEG_INF) for g in range(N_GROUPS)],
        axis=0)
    e_iota = lax.broadcasted_iota(jnp.int32, (N_EXPERTS, tm), 0)
    picked = jnp.zeros((N_EXPERTS, tm), F32)
    for _ in range(TOP_K):
        m = jnp.max(cur, axis=0, keepdims=True)
        idx = jnp.min(jnp.where(cur == m, e_iota, N_EXPERTS), axis=0, keepdims=True)
        hit = e_iota == idx
        picked = jnp.where(hit, 1.0, picked)
        cur = jnp.where(hit, -jnp.inf, cur)
    w = scores * picked
    return w / jnp.sum(w, axis=0, keepdims=True) * ROUTED_SCALE


def _norm_route_kernel(x_ref, g_ref, sh_ref, sc_ref, rwh_ref, rwl_ref, rb_ref, o_ref, gates_ref):
    h = _normed(x_ref, g_ref, sh_ref, sc_ref)
    h_hi = h.astype(BF16)
    o_ref[...] = h_hi
    h_lo = (h - h_hi.astype(F32)).astype(BF16)
    nt = (((1,), (1,)), ((), ()))
    logits = (lax.dot_general(rwh_ref[...], h_hi, nt, preferred_element_type=F32)
              + lax.dot_general(rwh_ref[...], h_lo, nt, preferred_element_type=F32)
              + lax.dot_general(rwl_ref[...], h_hi, nt, preferred_element_type=F32))
    gates_ref[...] = _route(logits, rb_ref[...])


def norm_route(x, gain, mods, router_w, router_b, n_rows):
    tm = TOK_TILE
    rwt = router_w.T
    rw_hi = rwt.astype(BF16)
    rw_lo = (rwt - rw_hi.astype(F32)).astype(BF16)
    return pl.pallas_call(
        _norm_route_kernel,
        grid=(n_rows // tm,),
        in_specs=[
            pl.BlockSpec((tm, D), lambda i: (i, 0)),
            pl.BlockSpec((1, D), lambda i: (0, 0)),
            _mod_spec(3, tm),
            _mod_spec(4, tm),
            pl.BlockSpec((N_EXPERTS, D), lambda i: (0, 0)),
            pl.BlockSpec((N_EXPERTS, D), lambda i: (0, 0)),
            pl.BlockSpec((N_EXPERTS, 1), lambda i: (0, 0)),
        ],
        out_specs=[pl.BlockSpec((tm, D), lambda i: (i, 0)), pl.BlockSpec((N_EXPERTS, tm), lambda i: (0, i))],
        out_shape=[jax.ShapeDtypeStruct((n_rows, D), BF16), jax.ShapeDtypeStruct((N_EXPERTS, n_rows), F32)],
        compiler_params=_cp(("parallel",)),
        name="norm_route",
    )(x, gain.reshape(1, D), mods, mods, rw_hi, rw_lo, router_b.reshape(N_EXPERTS, 1))


def _head_norm(acc, gain, g):
    a = acc[:, g * HEAD_DIM:(g + 1) * HEAD_DIM]
    ms = jnp.mean(a * a, axis=-1, keepdims=True)
    return a * lax.rsqrt(ms + EPS) * gain[:, g * HEAD_DIM:(g + 1) * HEAD_DIM]


def _proj_kernel(*refs, n_norm_tiles, rope):
    if rope:
        a_ref, w_ref, gain_ref, cos_ref, sin_ref, o_ref = refs
    else:
        a_ref, w_ref, gain_ref, o_ref = refs
    j = pl.program_id(1)
    acc = jnp.dot(a_ref[...], w_ref[...], preferred_element_type=F32)
    tn = acc.shape[1]

    @pl.when(j < n_norm_tiles)
    def _():
        gain = gain_ref[...]
        for g in range(tn // HEAD_DIM):
            y = _head_norm(acc, gain, g)
            if rope:
                y = y * cos_ref[...] + pltpu.roll(y, HEAD_DIM // 2, axis=1) * sin_ref[...]
            o_ref[:, g * HEAD_DIM:(g + 1) * HEAD_DIM] = y.astype(o_ref.dtype)

    @pl.when(j >= n_norm_tiles)
    def _():
        o_ref[...] = acc.astype(o_ref.dtype)


def proj(a, w, out_dtype, *, gain=None, n_norm_cols=0, cos=None, sin=None, tm=512, tn=512):
    m, k = a.shape
    n = w.shape[1]
    rope = cos is not None
    if gain is None:
        gain = jnp.ones((1, n), F32)
    in_specs = [
        pl.BlockSpec((tm, k), lambda i, j: (i, 0)),
        pl.BlockSpec((k, tn), lambda i, j: (0, j)),
        pl.BlockSpec((1, tn), lambda i, j: (0, j)),
    ]
    args = [a, w, gain]
    if rope:
        in_specs += [pl.BlockSpec((tm, HEAD_DIM), lambda i, j: (i, 0))] * 2
        args += [cos, sin]
    return pl.pallas_call(
        functools.partial(_proj_kernel, n_norm_tiles=n_norm_cols // tn, rope=rope),
        grid=(m // tm, n // tn),
        in_specs=in_specs,
        out_specs=pl.BlockSpec((tm, tn), lambda i, j: (i, j)),
        out_shape=jax.ShapeDtypeStruct((m, n), out_dtype),
        compiler_params=_cp(("parallel", "arbitrary")),
        name="proj",
    )(*args)


def _out_proj_kernel(*refs, n_a):
    a_refs = refs[:n_a]
    w_refs = refs[n_a:2 * n_a]
    res_ref, gm_ref, o_ref = refs[2 * n_a:]
    acc = jnp.dot(a_refs[0][...], w_refs[0][...], preferred_element_type=F32)
    for a_ref, w_ref in zip(a_refs[1:], w_refs[1:]):
        acc = acc + jnp.dot(a_ref[...], w_ref[...], preferred_element_type=F32)
    o_ref[...] = res_ref[...] + gm_ref[0] * acc


def out_proj(a_list, w_list, res, mods, which_gate, n_rows, tm=512, tn=512):
    n_a = len(a_list)
    n = w_list[0].shape[1]
    in_specs = [pl.BlockSpec((tm, a.shape[1]), lambda i, j: (i, 0)) for a in a_list]
    in_specs += [pl.BlockSpec((w.shape[0], tn), lambda i, j: (0, j)) for w in w_list]
    in_specs += [
        pl.BlockSpec((tm, tn), lambda i, j: (i, j)),
        pl.BlockSpec((1, 1, tn), lambda i, j: (_seg_of_tile(i, tm) * 6 + which_gate, 0, j)),
    ]
    return pl.pallas_call(
        functools.partial(_out_proj_kernel, n_a=n_a),
        grid=(n_rows // tm, n // tn),
        in_specs=in_specs,
        out_specs=pl.BlockSpec((tm, tn), lambda i, j: (i, j)),
        out_shape=jax.ShapeDtypeStruct((n_rows, n), F32),
        compiler_params=_cp(("parallel", "arbitrary")),
        name="out_proj",
    )(*a_list, *w_list, res, mods)


def na_bias_table(rpb):
    rows = SEQ // GRID_W
    n_dr, n_dc = 2 * NA_WIN_H - 1, 2 * NA_WIN_W - 1
    cols = np.arange(GRID_W)
    col_start = np.clip(cols - NA_WIN_W // 2, 0, GRID_W - NA_WIN_W)
    col_ok = (cols[None, :] >= col_start[:, None]) & (cols[None, :] < col_start[:, None] + NA_WIN_W)
    col_idx = np.clip(cols[None, :] - cols[:, None], 1 - NA_WIN_W, NA_WIN_W - 1) + (NA_WIN_W - 1)
    col_pick = (col_idx[None] == np.arange(n_dc)[:, None, None]).astype(np.float32)
    row_pick = np.zeros((3, NA_QROWS, NA_KROWS, n_dr), np.float32)
    ok = np.zeros((3, NA_QROWS, GRID_W, NA_KROWS, GRID_W), bool)
    for v, r0 in enumerate((0, 2 * NA_QROWS, rows - NA_QROWS)):
        kstart = int(np.clip(r0 - NA_WIN_H // 2, 0, rows - NA_KROWS))
        r = r0 + np.arange(NA_QROWS)
        kr = kstart + np.arange(NA_KROWS)
        win = np.clip(r - NA_WIN_H // 2, 0, rows - NA_WIN_H)
        row_ok = (kr[None, :] >= win[:, None]) & (kr[None, :] < win[:, None] + NA_WIN_H)
        row_idx = kr[None, :] - r[:, None] + (NA_WIN_H - 1)
        row_pick[v] = (row_idx[:, :, None] == np.arange(n_dr)) & row_ok[:, :, None]
        ok[v] = row_ok[:, None, :, None] & col_ok[None, :, None, :]
    hi = lax.Precision.HIGHEST
    by_col = jnp.einsum("hdk,kqc->hdqc", rpb.astype(F32), col_pick, precision=hi)
    tab = jnp.einsum("vijd,hdqc->hviqjc", row_pick, by_col, precision=hi)
    tab = jnp.where(ok[None], tab, NEG_INF)
    return tab.reshape(NA_HEADS, 3, NA_QROWS * GRID_W, NA_KROWS * GRID_W)


def _softmax_pv(pieces):
    m = functools.reduce(jnp.maximum, [jnp.max(s, axis=-1, keepdims=True) for s, _ in pieces])
    ps = [jnp.exp(s - m) for s, _ in pieces]
    l = functools.reduce(lambda a, b: a + b, [jnp.sum(p, axis=-1, keepdims=True) for p in ps])
    o = functools.reduce(lambda a, b: a + b,
                         [jnp.dot(p.astype(BF16), v, preferred_element_type=F32) for p, (_, v) in zip(ps, pieces)])
    return o / l


_NT_DIMS = (((1,), (1,)), ((), ()))


def _na_kernel(q_ref, k_ref, v_ref, kc_ref, vc_ref, bias_ref, o_ref):
    qb = pl.program_id(2)
    rows = SEQ // GRID_W
    kstart = pl.multiple_of(jnp.clip(qb * NA_QROWS - NA_WIN_H // 2, 0, rows - NA_KROWS) * GRID_W, GRID_W)
    q = q_ref[...]
    kw = k_ref[pl.ds(kstart, NA_KROWS * GRID_W), :]
    vw = v_ref[pl.ds(kstart, NA_KROWS * GRID_W), :]
    s_loc = lax.dot_general(q, kw, _NT_DIMS, preferred_element_type=F32) + bias_ref[0, 0]
    s_ctx = lax.dot_general(q, kc_ref[...], _NT_DIMS, preferred_element_type=F32)
    o_ref[...] = _softmax_pv([(s_loc, vw), (s_ctx, vc_ref[...])]).astype(o_ref.dtype)


def _ctx_attn_kernel(q_ref, k_ref, v_ref, o_ref):
    s = lax.dot_general(q_ref[...], k_ref[...], _NT_DIMS, preferred_element_type=F32)
    o_ref[...] = _softmax_pv([(s, v_ref[...])]).astype(o_ref.dtype)


def neighbourhood_attention(qkv, bias):
    nqb = SEQ // (NA_QROWS * GRID_W)
    tq = NA_QROWS * GRID_W
    ctx_blk = NX // CTX
    lat = pl.pallas_call(
        _na_kernel,
        grid=(BATCH, NA_HEADS, nqb),
        in_specs=[
            pl.BlockSpec((tq, HEAD_DIM), lambda b, h, i: (b * nqb + i, h)),
            pl.BlockSpec((SEQ, HEAD_DIM), lambda b, h, i: (b, NA_HEADS + h)),
            pl.BlockSpec((SEQ, HEAD_DIM), lambda b, h, i: (b, 2 * NA_HEADS + h)),
            pl.BlockSpec((CTX, HEAD_DIM), lambda b, h, i: (ctx_blk + b, NA_HEADS + h)),
            pl.BlockSpec((CTX, HEAD_DIM), lambda b, h, i: (ctx_blk + b, 2 * NA_HEADS + h)),
            pl.BlockSpec((1, 1, tq, NA_KROWS * GRID_W),
                         lambda b, h, i: (h, jnp.where(i == 0, 0, jnp.where(i == nqb - 1, 2, 1)), 0, 0)),
        ],
        out_specs=pl.BlockSpec((tq, HEAD_DIM), lambda b, h, i: (b * nqb + i, h)),
        out_shape=jax.ShapeDtypeStruct((NX, NA_WIDTH), BF16),
        compiler_params=_cp(("parallel", "parallel", "arbitrary")),
        name="na_attn",
    )(qkv, qkv, qkv, qkv, qkv, bias)
    ctx = pl.pallas_call(
        _ctx_attn_kernel,
        grid=(BATCH, NA_HEADS),
        in_specs=[
            pl.BlockSpec((CTX, HEAD_DIM), lambda b, h: (ctx_blk + b, h)),
            pl.BlockSpec((CTX, HEAD_DIM), lambda b, h: (ctx_blk + b, NA_HEADS + h)),
            pl.BlockSpec((CTX, HEAD_DIM), lambda b, h: (ctx_blk + b, 2 * NA_HEADS + h)),
        ],
        out_specs=pl.BlockSpec((CTX, HEAD_DIM), lambda b, h: (b, h)),
        out_shape=jax.ShapeDtypeStruct((BATCH * CTX, NA_WIDTH), BF16),
        compiler_params=_cp(("parallel", "parallel")),
        name="na_ctx_attn",
    )(qkv, qkv, qkv)
    return jnp.concatenate([lat, ctx], axis=0)


_CONV_HALO = 8


def _conv_kernel(prev_ref, cur_ref, next_ref, w_ref, b_ref, cs_ref, o_ref, buf_ref):
    i = pl.program_id(0)
    tm = cur_ref.shape[0]
    tiles_per_seq = SEQ // tm
    n_lat = NX // tm
    first = (i % tiles_per_seq == 0) | (i >= n_lat)
    last = (i % tiles_per_seq == tiles_per_seq - 1) | (i >= n_lat)
    buf_ref[0:_CONV_HALO, :] = prev_ref[...] * jnp.where(first, 0.0, 1.0)
    buf_ref[_CONV_HALO:_CONV_HALO + tm, :] = cur_ref[...]
    buf_ref[_CONV_HALO + tm:, :] = next_ref[...] * jnp.where(last, 0.0, 1.0)
    acc = jnp.zeros(cur_ref.shape, F32) + b_ref[...]
    for j in range(ML_CONV):
        off = _CONV_HALO + j - ML_CONV // 2
        acc = acc + buf_ref[off:off + tm, :] * w_ref[j:j + 1, :]
    o_ref[...] = (_silu(acc) * cs_ref[...]).astype(o_ref.dtype)


def conv_silu(t, w, b, col_scale):
    tm = CTX
    c = t.shape[1]
    hb = tm // _CONV_HALO
    n_halo_blocks = NT // _CONV_HALO
    wp = jnp.zeros((8, c), F32).at[:ML_CONV].set(w)
    return pl.pallas_call(
        _conv_kernel,
        grid=(NT // tm,),
        in_specs=[
            pl.BlockSpec((_CONV_HALO, c), lambda i: (jnp.maximum(i * hb - 1, 0), 0)),
            pl.BlockSpec((tm, c), lambda i: (i, 0)),
            pl.BlockSpec((_CONV_HALO, c), lambda i: (jnp.minimum((i + 1) * hb, n_halo_blocks - 1), 0)),
            pl.BlockSpec((8, c), lambda i: (0, 0)),
            pl.BlockSpec((1, c), lambda i: (0, 0)),
            pl.BlockSpec((1, c), lambda i: (0, 0)),
        ],
        out_specs=pl.BlockSpec((tm, c), lambda i: (i, 0)),
        out_shape=jax.ShapeDtypeStruct((NT, c), BF16),
        scratch_shapes=[pltpu.VMEM((tm + 2 * _CONV_HALO, c), F32)],
        compiler_params=_cp(("parallel",)),
        name="conv_silu",
    )(t, t, t, wp, b.reshape(1, c), col_scale.reshape(1, c))


def _split3(x):
    hi = x.astype(BF16)
    r = x - hi.astype(F32)
    mid = r.astype(BF16)
    lo = (r - mid.astype(F32)).astype(BF16)
    return hi, mid, lo


def _log_sigmoid(x):
    return jnp.minimum(x, 0.0) - jnp.log1p(jnp.exp(-jnp.abs(x)))


def _mlstm_chunk(reverse, q, kt, v_ext, gc, gr, gb_col, gb_row, c_ref, m):
    ln = q.shape[0]
    d = 1 if reverse else 0
    i_col = gc[:, 2 * d:2 * d + 1] + gb_col[:, 2 * d:2 * d + 1]
    f_col = _log_sigmoid(gc[:, 2 * d + 1:2 * d + 2] + gb_col[:, 2 * d + 1:2 * d + 2])
    i_row = gr[2 * d:2 * d + 1, :] + gb_row[2 * d:2 * d + 1, :]
    f_row = _log_sigmoid(gr[2 * d + 1:2 * d + 2, :] + gb_row[2 * d + 1:2 * d + 2, :])
    t_idx = lax.broadcasted_iota(jnp.int32, (ln, ln), 0)
    s_idx = lax.broadcasted_iota(jnp.int32, (ln, ln), 1)
    causal = (s_idx >= t_idx) if reverse else (s_idx <= t_idx)
    tri = jnp.where(causal, 1.0, 0.0).astype(BF16)
    f_col_w = jnp.broadcast_to(f_col, (ln, HEAD_DIM))
    b_col = functools.reduce(lambda a, b: a + b,
                             [jnp.dot(tri, p, preferred_element_type=F32) for p in _split3(f_col_w)])[:, 0:1]
    f_row_w = jnp.broadcast_to(f_row, (16, ln))
    b_row = functools.reduce(lambda a, b: a + b,
                             [lax.dot_general(p, tri, _NT_DIMS, preferred_element_type=F32) for p in _split3(f_row_w)])[0:1, :]
    total = jnp.sum(f_col, axis=0, keepdims=True)

    dmat = jnp.where(causal, b_col - b_row + i_row, -jnp.inf)
    inter = b_col + m
    m_t = jnp.maximum(inter, jnp.max(dmat, axis=-1, keepdims=True))
    w_inter = jnp.exp(inter - m_t)
    s = jnp.dot(q, kt, preferred_element_type=F32) * jnp.exp(dmat - m_t)
    numden = (w_inter * jnp.dot(q, c_ref[...].astype(BF16), preferred_element_type=F32)
              + jnp.dot(s.astype(BF16), v_ext, preferred_element_type=F32))
    den = numden[:, HEAD_DIM:HEAD_DIM + 1]
    h = numden[:, :HEAD_DIM] / jnp.maximum(jnp.abs(den), jnp.exp(-m_t))

    g = total - b_col + i_col
    m_new = jnp.maximum(total + m, jnp.max(g, axis=0, keepdims=True))
    decay = jnp.exp(total + m - m_new)
    wg = jnp.exp(g - m_new)
    upd = jnp.dot(kt, (wg * v_ext.astype(F32)).astype(BF16), preferred_element_type=F32)
    c_ref[...] = decay * c_ref[...] + upd
    return h, m_new


def _mlstm_kernel(q_ref, kt_ref, v_ref, o_ref, qc_ref, ktc_ref, vc_ref, oc_ref, gc_ref, gr_ref, gbc_ref, gbr_ref,
                  hn_ref, out_ref, outc_ref, cf_ref, cb_ref, hf_ref, hb_ref):
    ln = ML_CHUNK
    n_chunks = SEQ // ln
    ones_col = jnp.where(lax.broadcasted_iota(jnp.int32, (ln, HEAD_DIM), 1) == 0, 1.0, 0.0).astype(BF16)
    gb_col = gbc_ref[0]
    gb_row = gbr_ref[0]

    def v_ext(v):
        return jnp.concatenate([v, ones_col], axis=1)

    cf_ref[...] = jnp.zeros(cf_ref.shape, F32)
    cb_ref[...] = jnp.zeros(cb_ref.shape, F32)
    m0 = jnp.zeros((1, 1), F32)

    gc = gc_ref[0, 0:ln, :]
    gr = gr_ref[0, :, 0:ln]
    vx = v_ext(vc_ref[...])
    hf, mf = _mlstm_chunk(False, qc_ref[...], ktc_ref[...], vx, gc, gr, gb_col, gb_row, cf_ref, m0)
    hb, mb = _mlstm_chunk(True, qc_ref[...], ktc_ref[...], vx, gc, gr, gb_col, gb_row, cb_ref, m0)

    def finish(h, o_gate, hn):
        ms = jnp.mean(h * h, axis=-1, keepdims=True)
        return (h * lax.rsqrt(ms + EPS) * hn * jax.nn.sigmoid(o_gate.astype(F32))).astype(BF16)

    hn = hn_ref[...]
    outc_ref[...] = finish(hf + hb, oc_ref[...], hn)

    def body(c, carry):
        mf, mb = carry
        for reverse, c_ref, h_ref in ((False, cf_ref, hf_ref), (True, cb_ref, hb_ref)):
            cc = (n_chunks - 1 - c) if reverse else c
            r0 = pl.multiple_of(cc * ln, ln)
            g0 = pl.multiple_of(cc * ln + CTX, ln)
            h, m_new = _mlstm_chunk(reverse, q_ref[pl.ds(r0, ln), :], kt_ref[:, pl.ds(r0, ln)],
                                    v_ext(v_ref[pl.ds(r0, ln), :]), gc_ref[0, pl.ds(g0, ln), :],
                                    gr_ref[0, :, pl.ds(g0, ln)], gb_col, gb_row, c_ref, mb if reverse else mf)
            h_ref[pl.ds(r0, ln), :] = h
            if reverse:
                mb = m_new
            else:
                mf = m_new
        return mf, mb

    lax.fori_loop(0, n_chunks, body, (mf, mb))

    def fin_body(c, _):
        r0 = pl.multiple_of(c * ln, ln)
        out_ref[pl.ds(r0, ln), :] = finish(hf_ref[pl.ds(r0, ln), :] + hb_ref[pl.ds(r0, ln), :],
                                           o_ref[pl.ds(r0, ln), :], hn)
        return 0

    lax.fori_loop(0, n_chunks, fin_body, 0)


def mlstm(qk, kt, vo, gates_col, gates_row, gate_b, hnorm):
    hh = ML_HEADS
    ctx_blk = NX // CTX
    tot = CTX + SEQ
    gb = gate_b.astype(F32).transpose(2, 0, 1).reshape(hh, 4)
    gb_col = jnp.zeros((hh, 1, HEAD_DIM), F32).at[:, 0, :4].set(gb)
    gb_row = jnp.zeros((hh, 8, 1), F32).at[:, :4, 0].set(gb)
    lat, ctx = pl.pallas_call(
        _mlstm_kernel,
        grid=(BATCH, hh),
        in_specs=[
            pl.BlockSpec((SEQ, HEAD_DIM), lambda b, h: (b, h)),
            pl.BlockSpec((HEAD_DIM, SEQ), lambda b, h: (h, b)),
            pl.BlockSpec((SEQ, HEAD_DIM), lambda b, h: (b, h)),
            pl.BlockSpec((SEQ, HEAD_DIM), lambda b, h: (b, hh + h)),
            pl.BlockSpec((CTX, HEAD_DIM), lambda b, h: (ctx_blk + b, h)),
            pl.BlockSpec((HEAD_DIM, CTX), lambda b, h: (h, ctx_blk + b)),
            pl.BlockSpec((CTX, HEAD_DIM), lambda b, h: (ctx_blk + b, h)),
            pl.BlockSpec((CTX, HEAD_DIM), lambda b, h: (ctx_blk + b, hh + h)),
            pl.BlockSpec((1, tot, HEAD_DIM), lambda b, h: (b * hh + h, 0, 0)),
            pl.BlockSpec((1, 8, tot), lambda b, h: (b * hh + h, 0, 0)),
            pl.BlockSpec((1, 1, HEAD_DIM), lambda b, h: (h, 0, 0)),
            pl.BlockSpec((1, 8, 1), lambda b, h: (h, 0, 0)),
            pl.BlockSpec((1, HEAD_DIM), lambda b, h: (0, h)),
        ],
        out_specs=[pl.BlockSpec((SEQ, HEAD_DIM), lambda b, h: (b, h)), pl.BlockSpec((CTX, HEAD_DIM), lambda b, h: (b, h))],
        out_shape=[jax.ShapeDtypeStruct((NX, ML_WIDTH), BF16), jax.ShapeDtypeStruct((BATCH * CTX, ML_WIDTH), BF16)],
        scratch_shapes=[
            pltpu.VMEM((HEAD_DIM, 2 * HEAD_DIM), F32),
            pltpu.VMEM((HEAD_DIM, 2 * HEAD_DIM), F32),
            pltpu.VMEM((SEQ, HEAD_DIM), F32),
            pltpu.VMEM((SEQ, HEAD_DIM), F32),
        ],
        compiler_params=_cp(("parallel", "parallel")),
        name="mlstm",
    )(qk, kt, vo, vo, qk, kt, vo, vo, gates_col, gates_row, gb_col, gb_row, hnorm.reshape(1, ML_WIDTH))
    return jnp.concatenate([lat, ctx], axis=0)


def _diff_attn_kernel(q0_ref, q1_ref, k0_ref, k1_ref, k0c_ref, k1c_ref, v_ref, vc_ref, lam_ref, sub_ref, o_ref,
                      *, lambda_init):
    lam = lam_ref[...]
    lam_full = (jnp.exp(jnp.sum(lam[0:1] * lam[1:2], axis=-1, keepdims=True))
                - jnp.exp(jnp.sum(lam[2:3] * lam[3:4], axis=-1, keepdims=True)) + lambda_init)

    tq = q0_ref.shape[0]
    qs = (q0_ref[...], q1_ref[...])

    def step(carry, ks, v):
        out = []
        for (m, l, acc), q, k in zip(carry, qs, ks):
            s = lax.dot_general(q, k, _NT_DIMS, preferred_element_type=F32)
            m_new = jnp.maximum(m, jnp.max(s, axis=-1, keepdims=True))
            alpha = jnp.exp2(m - m_new)
            p = jnp.exp2(s - m_new)
            l = alpha * l + jnp.sum(p, axis=-1, keepdims=True)
            acc = alpha * acc + jnp.dot(p.astype(BF16), v, preferred_element_type=F32)
            out.append((m_new, l, acc))
        return tuple(out)

    init = tuple((jnp.full((tq, 1), -jnp.inf, F32), jnp.zeros((tq, 1), F32), jnp.zeros((tq, DA_VDIM), F32))
                 for _ in range(2))

    carry = init
    for c in range(SEQ // DA_KCHUNK):
        rows = slice(c * DA_KCHUNK, (c + 1) * DA_KCHUNK)
        carry = step(carry, (k0_ref[rows, :], k1_ref[rows, :]), v_ref[rows, :])
    (_, l0, acc0), (_, l1, acc1) = step(carry, (k0c_ref[...], k1c_ref[...]), vc_ref[...])
    o = acc0 / l0 - lam_full * (acc1 / l1)
    ms = jnp.mean(o * o, axis=-1, keepdims=True)
    o_ref[...] = (o * lax.rsqrt(ms + EPS) * sub_ref[...] * (1.0 - lambda_init)).astype(o_ref.dtype)


def diff_attention(qkv, lam, subln, lambda_init, tq=256):
    nq = SEQ // tq
    ctx_blk = NX // CTX
    kcol = 2 * DA_HEADS
    vcol = 2 * DA_HEADS
    return pl.pallas_call(
        functools.partial(_diff_attn_kernel, lambda_init=lambda_init),
        grid=(BATCH, DA_HEADS, nq),
        in_specs=[
            pl.BlockSpec((tq, HEAD_DIM), lambda b, h, i: (b * nq + i, 2 * h)),
            pl.BlockSpec((tq, HEAD_DIM), lambda b, h, i: (b * nq + i, 2 * h + 1)),
            pl.BlockSpec((SEQ, HEAD_DIM), lambda b, h, i: (b, kcol + 2 * h)),
            pl.BlockSpec((SEQ, HEAD_DIM), lambda b, h, i: (b, kcol + 2 * h + 1)),
            pl.BlockSpec((CTX, HEAD_DIM), lambda b, h, i: (ctx_blk + b, kcol + 2 * h)),
            pl.BlockSpec((CTX, HEAD_DIM), lambda b, h, i: (ctx_blk + b, kcol + 2 * h + 1)),
            pl.BlockSpec((SEQ, DA_VDIM), lambda b, h, i: (b, vcol + h)),
            pl.BlockSpec((CTX, DA_VDIM), lambda b, h, i: (ctx_blk + b, vcol + h)),
            pl.BlockSpec((4, HEAD_DIM), lambda b, h, i: (0, 0)),
            pl.BlockSpec((1, DA_VDIM), lambda b, h, i: (0, 0)),
        ],
        out_specs=pl.BlockSpec((tq, DA_VDIM), lambda b, h, i: (b * nq + i, h)),
        out_shape=jax.ShapeDtypeStruct((NX, DA_HEADS * DA_VDIM), BF16),
        compiler_params=_cp(("parallel", "parallel", "arbitrary"), vmem_mb=56),
        name="diff_attn",
    )(qkv, qkv, qkv, qkv, qkv, qkv, qkv, qkv, lam.astype(F32), subln.reshape(1, DA_VDIM).astype(F32))


def dispatch_tables(gates_t, n_rows):
    nt = n_rows // TOK_TILE
    max_units = (n_rows * TOP_K + nt * N_EXPERTS * (UNIT - 1)) // UNIT
    n_ffn_tiles = (max_units + N_EXPERTS * (FFN_UNITS - 1)) // FFN_UNITS + 1
    gates = gates_t.T.reshape(nt, TOK_TILE, N_EXPERTS)
    sel = gates > 0
    seli = sel.astype(jnp.int32)
    cnt = seli.sum(axis=1)
    nun = (cnt + UNIT - 1) // UNIT
    loc_off = jnp.cumsum(nun, axis=1) - nun
    rank = jnp.cumsum(seli, axis=1) - seli
    slot = loc_off[:, None, :] * UNIT + rank
    kr = jnp.cumsum(seli, axis=2) - seli
    slot6, w6 = [], []
    for k in range(TOP_K):
        hit = sel & (kr == k)
        slot6.append(jnp.where(hit.any(axis=2), jnp.sum(jnp.where(hit, slot, 0), axis=2), -1))
        w6.append(jnp.sum(jnp.where(hit, gates, 0.0), axis=2))
    pad = [jnp.full((nt, TOK_TILE), -1, jnp.int32)] * (8 - TOP_K)
    slot6 = jnp.stack(slot6 + pad, axis=-1)
    w6 = jnp.stack(w6 + [jnp.zeros((nt, TOK_TILE), F32)] * (8 - TOP_K), axis=-1)

    seg_un = nun.sum(axis=0)
    seg_pad = (seg_un + FFN_UNITS - 1) // FFN_UNITS * FFN_UNITS
    seg_end = jnp.cumsum(seg_pad)
    seg_start = seg_end - seg_pad
    gstart = seg_start[None, :] + jnp.cumsum(nun, axis=0) - nun
    u = jnp.arange(UNITS_PER_TILE, dtype=jnp.int32)
    loc_end = loc_off + nun
    ue = (loc_end[:, None, :] <= u[None, :, None]).sum(axis=-1)
    onehot = ue[:, :, None] == jnp.arange(N_EXPERTS)[None, None, :]
    dst = jnp.sum(jnp.where(onehot, (gstart - loc_off)[:, None, :], 0), axis=-1) + u[None, :]
    n_units_total = n_ffn_tiles * FFN_UNITS
    valid = ue < N_EXPERTS
    dst = jnp.where(valid, dst, n_units_total - 1).astype(jnp.int32)

    flat_dst = jnp.where(valid, dst, n_units_total).reshape(-1)
    src = jnp.full((n_units_total + 1,), UNITS_PER_TILE - 1, jnp.int32).at[flat_dst].set(
        jnp.arange(nt * UNITS_PER_TILE, dtype=jnp.int32))[:n_units_total]
    mt = jnp.arange(n_ffn_tiles, dtype=jnp.int32) * FFN_UNITS
    tile_expert = jnp.minimum((seg_end[None, :] <= mt[:, None]).sum(axis=-1), N_EXPERTS - 1).astype(jnp.int32)
    n_used = (seg_end[-1] // FFN_UNITS).astype(jnp.int32).reshape(1)
    return dict(slot6=slot6, w6=w6, slot6_t=slot6.transpose(0, 2, 1), dst=dst.reshape(-1), src=src,
                tile_expert=tile_expert, n_used=n_used, n_ffn_tiles=n_ffn_tiles, nt=nt)


def _moe_gather_kernel(h_ref, slot_ref, o_ref):
    s_iota = lax.broadcasted_iota(jnp.int32, (SLOTS, TOK_TILE), 0)
    slots = slot_ref[0]
    p = jnp.zeros((SLOTS, TOK_TILE), F32)
    for k in range(TOP_K):
        p = jnp.where(s_iota == slots[k:k + 1, :], 1.0, p)
    p = p.astype(BF16)
    o_ref[0] = jnp.dot(p, h_ref[...], preferred_element_type=F32).astype(BF16)


def moe_gather(h, slot6_t, nt):
    return pl.pallas_call(
        _moe_gather_kernel,
        grid=(nt,),
        in_specs=[pl.BlockSpec((TOK_TILE, D), lambda i: (i, 0)), pl.BlockSpec((1, 8, TOK_TILE), lambda i: (i, 0, 0))],
        out_specs=pl.BlockSpec((1, SLOTS, D), lambda i: (i, 0, 0)),
        out_shape=jax.ShapeDtypeStruct((nt, SLOTS, D), BF16),
        compiler_params=_cp(("parallel",)),
        name="moe_gather",
    )(h, slot6_t)


def _unit_copy(src_hbm, buf_ref, sem_ref, slot, src_unit, j):
    return pltpu.make_async_copy(src_hbm.at[pl.ds(pl.multiple_of(src_unit * UNIT, UNIT), UNIT)],
                                 buf_ref.at[slot, pl.ds(j * UNIT, UNIT)], sem_ref.at[slot])


def _fetch_units(table_ref, base, n_units, src_hbm, buf_ref, sem_ref, slot):
    def body(j, _):
        _unit_copy(src_hbm, buf_ref, sem_ref, slot, table_ref[base + j], j).start()
        return 0

    lax.fori_loop(0, n_units, body, 0, unroll=8)


def _wait_units(n_units, src_hbm, buf_ref, sem_ref, slot):
    pltpu.make_async_copy(src_hbm.at[pl.ds(0, n_units * UNIT)], buf_ref.at[slot], sem_ref.at[slot]).wait()


def _moe_ffn_kernel(src_ref, te_ref, nu_ref, x_hbm, wg_ref, wu_ref, wd_ref, o_ref, xbuf_ref, sem_ref,
                    wgb_ref, wub_ref, wdb_ref):
    m = pl.program_id(0)
    n_used = nu_ref[0]
    slot = m % 2

    @pl.when(m == 0)
    def _():
        _fetch_units(src_ref, 0, FFN_UNITS, x_hbm, xbuf_ref, sem_ref, 0)

    @pl.when(m + 1 < n_used)
    def _():
        _fetch_units(src_ref, (m + 1) * FFN_UNITS, FFN_UNITS, x_hbm, xbuf_ref, sem_ref, 1 - slot)

    @pl.when(m < n_used)
    def _():
        @pl.when((m == 0) | (te_ref[m] != te_ref[jnp.maximum(m - 1, 0)]))
        def _():
            wgb_ref[...] = wg_ref[0].astype(BF16)
            wub_ref[...] = wu_ref[0].astype(BF16)
            wdb_ref[...] = wd_ref[0].astype(BF16)

        _wait_units(FFN_UNITS, x_hbm, xbuf_ref, sem_ref, slot)
        x = xbuf_ref[slot]
        g = jnp.dot(x, wgb_ref[...], preferred_element_type=F32)
        u = jnp.dot(x, wub_ref[...], preferred_element_type=F32)
        a = (_silu(g) * u).astype(BF16)
        o_ref[...] = jnp.dot(a, wdb_ref[...], preferred_element_type=F32).astype(o_ref.dtype)

    @pl.when(m >= n_used)
    def _():
        o_ref[...] = jnp.zeros(o_ref.shape, o_ref.dtype)


def moe_ffn(x_tiles, tabs, wg, wu, wd):
    n_tiles = tabs["n_ffn_tiles"]
    x_flat = x_tiles.reshape(-1, D)
    grid_spec = pltpu.PrefetchScalarGridSpec(
        num_scalar_prefetch=3,
        grid=(n_tiles,),
        in_specs=[
            pl.BlockSpec(memory_space=pl.ANY),
            pl.BlockSpec((1, D, D_EXPERT), lambda m, src, te, nu: (te[m], 0, 0)),
            pl.BlockSpec((1, D, D_EXPERT), lambda m, src, te, nu: (te[m], 0, 0)),
            pl.BlockSpec((1, D_EXPERT, D), lambda m, src, te, nu: (te[m], 0, 0)),
        ],
        out_specs=pl.BlockSpec((FFN_TM, D), lambda m, src, te, nu: (m, 0)),
        scratch_shapes=[pltpu.VMEM((2, FFN_TM, D), BF16), pltpu.SemaphoreType.DMA((2,)),
                        pltpu.VMEM((D, D_EXPERT), BF16), pltpu.VMEM((D, D_EXPERT), BF16),
                        pltpu.VMEM((D_EXPERT, D), BF16)],
    )
    return pl.pallas_call(
        _moe_ffn_kernel,
        grid_spec=grid_spec,
        out_shape=jax.ShapeDtypeStruct((n_tiles * FFN_TM, D), BF16),
        compiler_params=_cp(("arbitrary",), vmem_mb=56),
        name="moe_ffn",
    )(tabs["src"], tabs["tile_expert"], tabs["n_used"], x_flat, wg, wu, wd)


def _shared_ffn_kernel(a_ref, wg_ref, wu_ref, wd_ref, o_ref):
    a = a_ref[...]
    g = jnp.dot(a, wg_ref[...], preferred_element_type=F32)
    u = jnp.dot(a, wu_ref[...], preferred_element_type=F32)
    o_ref[...] = jnp.dot((_silu(g) * u).astype(BF16), wd_ref[...], preferred_element_type=F32).astype(o_ref.dtype)


def shared_ffn(h, wg, wu, wd, n_rows, tm=512):
    return pl.pallas_call(
        _shared_ffn_kernel,
        grid=(n_rows // tm,),
        in_specs=[
            pl.BlockSpec((tm, D), lambda i: (i, 0)),
            pl.BlockSpec((D, D_EXPERT), lambda i: (0, 0)),
            pl.BlockSpec((D, D_EXPERT), lambda i: (0, 0)),
            pl.BlockSpec((D_EXPERT, D), lambda i: (0, 0)),
        ],
        out_specs=pl.BlockSpec((tm, D), lambda i: (i, 0)),
        out_shape=jax.ShapeDtypeStruct((n_rows, D), BF16),
        compiler_params=_cp(("parallel",)),
        name="shared_ffn",
    )(h, wg, wu, wd)


def _moe_combine_kernel(dst_ref, y_hbm, slot_ref, w_ref, sh_ref, res_ref, gm_ref, o_ref, ybuf_ref, sem_ref):
    i = pl.program_id(0)
    n = pl.num_programs(0)
    slot = i % 2

    @pl.when(i == 0)
    def _():
        _fetch_units(dst_ref, 0, UNITS_PER_TILE, y_hbm, ybuf_ref, sem_ref, 0)

    @pl.when(i + 1 < n)
    def _():
        _fetch_units(dst_ref, (i + 1) * UNITS_PER_TILE, UNITS_PER_TILE, y_hbm, ybuf_ref, sem_ref, 1 - slot)

    _wait_units(UNITS_PER_TILE, y_hbm, ybuf_ref, sem_ref, slot)
    lane = lax.broadcasted_iota(jnp.int32, (TOK_TILE, SLOTS), 1)
    slots = slot_ref[0]
    w = w_ref[0]
    pw = jnp.zeros((TOK_TILE, SLOTS), F32)
    for k in range(TOP_K):
        pw = jnp.where(lane == slots[:, k:k + 1], w[:, k:k + 1], pw)
    routed = jnp.dot(pw.astype(BF16), ybuf_ref[slot], preferred_element_type=F32)
    o_ref[...] = res_ref[...] + gm_ref[0] * (routed + sh_ref[...].astype(F32))


def moe_combine(y_sorted, tabs, shared, res, mods, n_rows):
    nt = tabs["nt"]
    tm = TOK_TILE
    grid_spec = pltpu.PrefetchScalarGridSpec(
        num_scalar_prefetch=1,
        grid=(nt,),
        in_specs=[
            pl.BlockSpec(memory_space=pl.ANY),
            pl.BlockSpec((1, tm, 8), lambda i, dst: (i, 0, 0)),
            pl.BlockSpec((1, tm, 8), lambda i, dst: (i, 0, 0)),
            pl.BlockSpec((tm, D), lambda i, dst: (i, 0)),
            pl.BlockSpec((tm, D), lambda i, dst: (i, 0)),
            pl.BlockSpec((1, 1, D), lambda i, dst: (_seg_of_tile(i, tm) * 6 + 5, 0, 0)),
        ],
        out_specs=pl.BlockSpec((tm, D), lambda i, dst: (i, 0)),
        scratch_shapes=[pltpu.VMEM((2, SLOTS, D), BF16), pltpu.SemaphoreType.DMA((2,))],
    )
    return pl.pallas_call(
        _moe_combine_kernel,
        grid_spec=grid_spec,
        out_shape=jax.ShapeDtypeStruct((n_rows, D), F32),
        compiler_params=_cp(("arbitrary",), vmem_mb=56),
        name="moe_combine",
    )(tabs["dst"], y_sorted, tabs["slot6"], tabs["w6"], shared, res, mods)


def moe_block(xa, gain, mods, router_w, router_b, wg, wu, wd, sg, su, sd, n_rows):
    h, gates_t = norm_route(xa, gain, mods, router_w, router_b, n_rows)
    tabs = dispatch_tables(gates_t, n_rows)
    x_tiles = moe_gather(h, tabs["slot6_t"], tabs["nt"])
    y_sorted = moe_ffn(x_tiles, tabs, wg, wu, wd)
    shared = shared_ffn(h, sg.astype(BF16), su.astype(BF16), sd.astype(BF16), n_rows)
    return moe_combine(y_sorted, tabs, shared, xa, mods, n_rows)


def rope_tables():
    t = jnp.arange(SEQ)
    row = (t // GRID_W).astype(F32)
    col = (t % GRID_W).astype(F32)
    n_freq = HEAD_DIM // 4
    inv_freq = ROPE_THETA ** (-jnp.arange(n_freq, dtype=F32) / n_freq)
    ang = jnp.concatenate([row[:, None] * inv_freq, col[:, None] * inv_freq], axis=-1)
    ang = jnp.concatenate([ang, ang], axis=-1)
    sign = jnp.where(jnp.arange(HEAD_DIM) < HEAD_DIM // 2, -1.0, 1.0)
    cos = jnp.concatenate([jnp.cos(ang)] * BATCH + [jnp.ones((BATCH * CTX, HEAD_DIM), F32)], axis=0)
    sin = jnp.concatenate([jnp.sin(ang) * sign] * BATCH + [jnp.zeros((BATCH * CTX, HEAD_DIM), F32)], axis=0)
    return cos, sin


def even_layer(xa, mods, norm1, w_in, na_qnorm, na_knorm, na_rpb, conv_w, conv_b, gate_b, hnorm, w_out):
    h = norm_mod(xa, norm1, mods, 0, NT)
    w = w_in.astype(BF16)
    c0 = 3 * NA_WIDTH
    c1 = c0 + 2 * ML_WIDTH
    c2 = c1 + 2 * ML_WIDTH
    scale = HEAD_DIM ** -0.5
    gain = jnp.concatenate([jnp.tile(na_qnorm.astype(F32) * scale, NA_HEADS), jnp.tile(na_knorm.astype(F32), NA_HEADS),
                            jnp.ones((NA_WIDTH,), F32)]).reshape(1, c0)
    qkv = proj(h, w[:, :c0], BF16, gain=gain, n_norm_cols=2 * NA_WIDTH)
    na_h = neighbourhood_attention(qkv, na_bias_table(na_rpb))

    ml_qk = proj(h, w[:, c0:c1], F32)
    ml_vo = proj(h, w[:, c1:c2], BF16)
    n_gate = w_in.shape[1] - c2
    w_gate = jnp.zeros((D, HEAD_DIM), BF16).at[:, :n_gate].set(w[:, c2:])
    g = proj(h, w_gate, F32, tn=HEAD_DIM)[:, :n_gate]
    col_scale = jnp.concatenate([jnp.ones((ML_WIDTH,), F32), jnp.full((ML_WIDTH,), HEAD_DIM ** -0.5, F32)])
    qk = conv_silu(ml_qk, conv_w.astype(F32), conv_b.astype(F32), col_scale)
    kt = qk[:, ML_WIDTH:].T
    g = g.reshape(NT, 4, ML_HEADS)
    g = jnp.concatenate([g[NX:].reshape(BATCH, CTX, 4, ML_HEADS), g[:NX].reshape(BATCH, SEQ, 4, ML_HEADS)], axis=1)
    g = g.transpose(0, 3, 1, 2).reshape(BATCH * ML_HEADS, CTX + SEQ, 4)
    g_col = jnp.zeros((BATCH * ML_HEADS, CTX + SEQ, HEAD_DIM), F32).at[:, :, :4].set(g)
    g_row = jnp.zeros((BATCH * ML_HEADS, 8, CTX + SEQ), F32).at[:, :4, :].set(g.transpose(0, 2, 1))
    ml_h = mlstm(qk, kt, ml_vo, g_col, g_row, gate_b, hnorm.astype(F32))

    wo = w_out.astype(BF16)
    return out_proj([na_h, ml_h], [wo[:NA_WIDTH], wo[NA_WIDTH:]], xa, mods, 2, NT)


def odd_layer(xa, mods, norm1, w_in, qnorm, knorm, lam, subln, w_out, lambda_init):
    h = norm_mod(xa, norm1, mods, 0, NT)
    scale = HEAD_DIM ** -0.5 * math.log2(math.e)
    n_qk = 2 * DA_HEADS * HEAD_DIM
    gain = jnp.concatenate([jnp.tile(qnorm.astype(F32) * scale, 2 * DA_HEADS), jnp.tile(knorm.astype(F32), 2 * DA_HEADS),
                            jnp.ones((DA_HEADS * DA_VDIM,), F32)]).reshape(1, -1)
    cos, sin = rope_tables()
    qkv = proj(h, w_in.astype(BF16), BF16, gain=gain, n_norm_cols=2 * n_qk, cos=cos, sin=sin)
    o = diff_attention(qkv, lam, subln, lambda_init)
    return out_proj([o], [w_out.astype(BF16)], xa, mods, 2, NX)


def diff_lambda_init(layer):
    return 0.8 - 0.6 * math.exp(-0.3 * layer)


def kernel(x, c, ctx, c_ctx, l0_ada_w, l0_ada_b, l0_norm1, l0_norm2, l0_w_in, l0_na_qnorm, l0_na_knorm, l0_na_rpb, l0_ml_conv_w, l0_ml_conv_b, l0_ml_gate_b, l0_ml_hnorm, l0_w_out, l0_router_w, l0_router_b, l0_exp_gate, l0_exp_up, l0_exp_down, l0_sh_gate, l0_sh_up, l0_sh_down, l1_ada_w, l1_ada_b, l1_norm1, l1_norm2, l1_w_in, l1_qnorm, l1_knorm, l1_lambda, l1_subln, l1_w_out, l1_router_w, l1_router_b, l1_exp_gate, l1_exp_up, l1_exp_down, l1_sh_gate, l1_sh_up, l1_sh_down):
    assert x.shape == (BATCH, SEQ, D) and ctx.shape == (BATCH, CTX, D)
    xa = jnp.concatenate([x.reshape(NX, D), ctx.reshape(BATCH * CTX, D)], axis=0).astype(F32)
    cvec = jnp.zeros((8, D), F32).at[:BATCH].set(c).at[BATCH].set(c_ctx)

    mods0 = adaln(cvec, l0_ada_w, l0_ada_b)
    xa = even_layer(xa, mods0, l0_norm1, l0_w_in, l0_na_qnorm, l0_na_knorm, l0_na_rpb, l0_ml_conv_w, l0_ml_conv_b,
                    l0_ml_gate_b, l0_ml_hnorm, l0_w_out)
    xa = moe_block(xa, l0_norm2, mods0, l0_router_w, l0_router_b, l0_exp_gate, l0_exp_up, l0_exp_down,
                   l0_sh_gate, l0_sh_up, l0_sh_down, NT)

    mods1 = adaln(cvec, l1_ada_w, l1_ada_b)
    xl = odd_layer(xa, mods1, l1_norm1, l1_w_in, l1_qnorm, l1_knorm, l1_lambda, l1_subln, l1_w_out, diff_lambda_init(1))
    xl = moe_block(xl, l1_norm2, mods1, l1_router_w, l1_router_b, l1_exp_gate, l1_exp_up, l1_exp_down,
                   l1_sh_gate, l1_sh_up, l1_sh_down, NX)
    return xl.reshape(BATCH, SEQ, D)
```

```python
import functools
import math

import jax
import jax.numpy as jnp
import numpy as np
from jax import lax
from jax.experimental import pallas as pl
from jax.experimental.pallas import tpu as pltpu

F32 = jnp.float32
BF16 = jnp.bfloat16

D = 2048
BATCH = 2
SEQ = 4096
CTX = 256
NX = BATCH * SEQ
NT = NX + BATCH * CTX
GRID_W = 64
EPS = 1e-6
NEG_INF = -1e30

NA_HEADS = 8
HEAD_DIM = 128
NA_WIDTH = NA_HEADS * HEAD_DIM
NA_WIN_H = 8
NA_WIN_W = 16
NA_QROWS = 4
NA_KROWS = 12
ML_HEADS = 8
ML_WIDTH = ML_HEADS * HEAD_DIM
ML_CONV = 5
ML_CHUNK = 256

DA_HEADS = 8
DA_VDIM = 256
DA_KCHUNK = 1024
ROPE_THETA = 10000.0

N_EXPERTS = 64
N_GROUPS = 8
TOPK_GROUPS = 4
TOP_K = 6
D_EXPERT = 512
ROUTED_SCALE = 2.5

TOK_TILE = 256
UNIT = 16
UNITS_PER_TILE = (TOK_TILE * TOP_K + N_EXPERTS * (UNIT - 1)) // UNIT + 1
UNITS_PER_TILE = -(-UNITS_PER_TILE // 32) * 32
SLOTS = UNITS_PER_TILE * UNIT
FFN_TM = 256
FFN_UNITS = FFN_TM // UNIT

V7X_VMEM_BYTES = 64 * 1024 * 1024


def _cp(sem, vmem_mb=48):
    assert vmem_mb * 1024 * 1024 < V7X_VMEM_BYTES
    return pltpu.CompilerParams(dimension_semantics=sem, vmem_limit_bytes=vmem_mb * 1024 * 1024)


def _silu(x):
    return x * jax.nn.sigmoid(x)


def _seg_of_tile(i, tm):
    return (i * tm) // SEQ


def _ada_kernel(c_ref, w_ref, b_ref, o_ref):
    s = _silu(c_ref[...]).astype(BF16)
    o_ref[...] = jnp.dot(s, w_ref[...].astype(BF16), preferred_element_type=F32) + b_ref[...]


def adaln(cvec, w, b):
    n = w.shape[1]
    tn = 1024
    out = pl.pallas_call(
        _ada_kernel,
        grid=(n // tn,),
        in_specs=[
            pl.BlockSpec((8, D), lambda j: (0, 0)),
            pl.BlockSpec((D, tn), lambda j: (0, j)),
            pl.BlockSpec((1, tn), lambda j: (0, j)),
        ],
        out_specs=pl.BlockSpec((8, tn), lambda j: (0, j)),
        out_shape=jax.ShapeDtypeStruct((8, n), F32),
        compiler_params=_cp(("arbitrary",)),
        name="adaln",
    )(cvec, w, b.reshape(1, n))
    return out[:3].reshape(18, 1, D)


def _normed(x_ref, g_ref, sh_ref, sc_ref):
    x = x_ref[...]
    ms = jnp.mean(x * x, axis=-1, keepdims=True)
    y = x * lax.rsqrt(ms + EPS) * g_ref[...]
    return y * (1.0 + sc_ref[0]) + sh_ref[0]


def _norm_mod_kernel(x_ref, g_ref, sh_ref, sc_ref, o_ref):
    o_ref[...] = _normed(x_ref, g_ref, sh_ref, sc_ref).astype(o_ref.dtype)


def _mod_spec(which, tm):
    return pl.BlockSpec((1, 1, D), lambda i: (_seg_of_tile(i, tm) * 6 + which, 0, 0))


def norm_mod(x, gain, mods, which_shift, n_rows, tm=256):
    return pl.pallas_call(
        _norm_mod_kernel,
        grid=(n_rows // tm,),
        in_specs=[
            pl.BlockSpec((tm, D), lambda i: (i, 0)),
            pl.BlockSpec((1, D), lambda i: (0, 0)),
            _mod_spec(which_shift, tm),
            _mod_spec(which_shift + 1, tm),
        ],
        out_specs=pl.BlockSpec((tm, D), lambda i: (i, 0)),
        out_shape=jax.ShapeDtypeStruct((n_rows, D), BF16),
        compiler_params=_cp(("parallel",)),
        name="norm_mod",
    )(x, gain.reshape(1, D), mods, mods)


def _route(logits, bias_col):
    tm = logits.shape[1]
    per_group = N_EXPERTS // N_GROUPS
    scores = jax.nn.sigmoid(logits)
    sel = scores + bias_col
    row8 = lax.broadcasted_iota(jnp.int32, (per_group, tm), 0)
    grp = jnp.zeros((N_GROUPS, tm), F32)
    for g in range(N_GROUPS):
        slab = sel[g * per_group:(g + 1) * per_group, :]
        m1 = jnp.max(slab, axis=0, keepdims=True)
        first = jnp.min(jnp.where(slab == m1, row8, per_group), axis=0, keepdims=True)
        m2 = jnp.max(jnp.where(row8 == first, -jnp.inf, slab), axis=0, keepdims=True)
        grp = jnp.where(row8 == g, m1 + m2, grp)
    rank = jnp.zeros((N_GROUPS, tm), jnp.int32)
    for g in range(N_GROUPS):
        vg = grp[g:g + 1, :]
        beats = (vg > grp) | ((vg == grp) & (g < row8))
        rank = rank + jnp.where(beats, 1, 0)
    keep = jnp.where(rank < TOPK_GROUPS, 1.0, 0.0)
    cur = jnp.concatenate(
        [jnp.where(keep[g:g + 1, :] > 0.5, sel[g * per_group:(g + 1) * per_group, :], NEG_INF) for g in range(N_GROUPS)],
        axis=0)
    e_iota = lax.broadcasted_iota(jnp.int32, (N_EXPERTS, tm), 0)
    picked = jnp.zeros((N_EXPERTS, tm), F32)
    for _ in range(TOP_K):
        m = jnp.max(cur, axis=0, keepdims=True)
        idx = jnp.min(jnp.where(cur == m, e_iota, N_EXPERTS), axis=0, keepdims=True)
        hit = e_iota == idx
        picked = jnp.where(hit, 1.0, picked)
        cur = jnp.where(hit, -jnp.inf, cur)
    w = scores * picked
    return w / jnp.sum(w, axis=0, keepdims=True) * ROUTED_SCALE


def _norm_route_kernel(x_ref, g_ref, sh_ref, sc_ref, rwh_ref, rwl_ref, rb_ref, o_ref, gates_ref):
    h = _normed(x_ref, g_ref, sh_ref, sc_ref)
    h_hi = h.astype(BF16)
    o_ref[...] = h_hi
    h_lo = (h - h_hi.astype(F32)).astype(BF16)
    nt = (((1,), (1,)), ((), ()))
    logits = (lax.dot_general(rwh_ref[...], h_hi, nt, preferred_element_type=F32)
              + lax.dot_general(rwh_ref[...], h_lo, nt, preferred_element_type=F32)
              + lax.dot_general(rwl_ref[...], h_hi, nt, preferred_element_type=F32))
    gates_ref[...] = _route(logits, rb_ref[...])


def norm_route(x, gain, mods, router_w, router_b, n_rows):
    tm = TOK_TILE
    rwt = router_w.T
    rw_hi = rwt.astype(BF16)
    rw_lo = (rwt - rw_hi.astype(F32)).astype(BF16)
    return pl.pallas_call(
        _norm_route_kernel,
        grid=(n_rows // tm,),
        in_specs=[
            pl.BlockSpec((tm, D), lambda i: (i, 0)),
            pl.BlockSpec((1, D), lambda i: (0, 0)),
            _mod_spec(3, tm),
            _mod_spec(4, tm),
            pl.BlockSpec((N_EXPERTS, D), lambda i: (0, 0)),
            pl.BlockSpec((N_EXPERTS, D), lambda i: (0, 0)),
            pl.BlockSpec((N_EXPERTS, 1), lambda i: (0, 0)),
        ],
        out_specs=[pl.BlockSpec((tm, D), lambda i: (i, 0)), pl.BlockSpec((N_EXPERTS, tm), lambda i: (0, i))],
        out_shape=[jax.ShapeDtypeStruct((n_rows, D), BF16), jax.ShapeDtypeStruct((N_EXPERTS, n_rows), F32)],
        compiler_params=_cp(("parallel",)),
        name="norm_route",
    )(x, gain.reshape(1, D), mods, mods, rw_hi, rw_lo, router_b.reshape(N_EXPERTS, 1))


def _head_norm(acc, gain, g):
    a = acc[:, g * HEAD_DIM:(g + 1) * HEAD_DIM]
    ms = jnp.mean(a * a, axis=-1, keepdims=True)
    return a * lax.rsqrt(ms + EPS) * gain[:, g * HEAD_DIM:(g + 1) * HEAD_DIM]


def _proj_kernel(*refs, n_norm_tiles, rope):
    if rope:
        a_ref, w_ref, gain_ref, cos_ref, sin_ref, o_ref = refs
    else:
        a_ref, w_ref, gain_ref, o_ref = refs
    j = pl.program_id(1)
    acc = jnp.dot(a_ref[...], w_ref[...], preferred_element_type=F32)
    tn = acc.shape[1]

    @pl.when(j < n_norm_tiles)
    def _():
        gain = gain_ref[...]
        for g in range(tn // HEAD_DIM):
            y = _head_norm(acc, gain, g)
            if rope:
                y = y * cos_ref[...] + pltpu.roll(y, HEAD_DIM // 2, axis=1) * sin_ref[...]
            o_ref[:, g * HEAD_DIM:(g + 1) * HEAD_DIM] = y.astype(o_ref.dtype)

    @pl.when(j >= n_norm_tiles)
    def _():
        o_ref[...] = acc.astype(o_ref.dtype)


def proj(a, w, out_dtype, *, gain=None, n_norm_cols=0, cos=None, sin=None, tm=512, tn=512):
    m, k = a.shape
    n = w.shape[1]
    rope = cos is not None
    if gain is None:
        gain = jnp.ones((1, n), F32)
    in_specs = [
        pl.BlockSpec((tm, k), lambda i, j: (i, 0)),
        pl.BlockSpec((k, tn), lambda i, j: (0, j)),
        pl.BlockSpec((1, tn), lambda i, j: (0, j)),
    ]
    args = [a, w, gain]
    if rope:
        in_specs += [pl.BlockSpec((tm, HEAD_DIM), lambda i, j: (i, 0))] * 2
        args += [cos, sin]
    return pl.pallas_call(
        functools.partial(_proj_kernel, n_norm_tiles=n_norm_cols // tn, rope=rope),
        grid=(m // tm, n // tn),
        in_specs=in_specs,
        out_specs=pl.BlockSpec((tm, tn), lambda i, j: (i, j)),
        out_shape=jax.ShapeDtypeStruct((m, n), out_dtype),
        compiler_params=_cp(("parallel", "arbitrary")),
        name="proj",
    )(*args)


def _out_proj_kernel(*refs, n_a):
    a_refs = refs[:n_a]
    w_refs = refs[n_a:2 * n_a]
    res_ref, gm_ref, o_ref = refs[2 * n_a:]
    acc = jnp.dot(a_refs[0][...], w_refs[0][...], preferred_element_type=F32)
    for a_ref, w_ref in zip(a_refs[1:], w_refs[1:]):
        acc = acc + jnp.dot(a_ref[...], w_ref[...], preferred_element_type=F32)
    o_ref[...] = res_ref[...] + gm_ref[0] * acc


def out_proj(a_list, w_list, res, mods, which_gate, n_rows, tm=512, tn=512):
    n_a = len(a_list)
    n = w_list[0].shape[1]
    in_specs = [pl.BlockSpec((tm, a.shape[1]), lambda i, j: (i, 0)) for a in a_list]
    in_specs += [pl.BlockSpec((w.shape[0], tn), lambda i, j: (0, j)) for w in w_list]
    in_specs += [
        pl.BlockSpec((tm, tn), lambda i, j: (i, j)),
        pl.BlockSpec((1, 1, tn), lambda i, j: (_seg_of_tile(i, tm) * 6 + which_gate, 0, j)),
    ]
    return pl.pallas_call(
        functools.partial(_out_proj_kernel, n_a=n_a),
        grid=(n_rows // tm, n // tn),
        in_specs=in_specs,
        out_specs=pl.BlockSpec((tm, tn), lambda i, j: (i, j)),
        out_shape=jax.ShapeDtypeStruct((n_rows, n), F32),
        compiler_params=_cp(("parallel", "arbitrary")),
        name="out_proj",
    )(*a_list, *w_list, res, mods)


def na_bias_table(rpb):
    rows = SEQ // GRID_W
    n_dr, n_dc = 2 * NA_WIN_H - 1, 2 * NA_WIN_W - 1
    cols = np.arange(GRID_W)
    col_start = np.clip(cols - NA_WIN_W // 2, 0, GRID_W - NA_WIN_W)
    col_ok = (cols[None, :] >= col_start[:, None]) & (cols[None, :] < col_start[:, None] + NA_WIN_W)
    col_idx = np.clip(cols[None, :] - cols[:, None], 1 - NA_WIN_W, NA_WIN_W - 1) + (NA_WIN_W - 1)
    col_pick = (col_idx[None] == np.arange(n_dc)[:, None, None]).astype(np.float32)
    row_pick = np.zeros((3, NA_QROWS, NA_KROWS, n_dr), np.float32)
    ok = np.zeros((3, NA_QROWS, GRID_W, NA_KROWS, GRID_W), bool)
    for v, r0 in enumerate((0, 2 * NA_QROWS, rows - NA_QROWS)):
        kstart = int(np.clip(r0 - NA_WIN_H // 2, 0, rows - NA_KROWS))
        r = r0 + np.arange(NA_QROWS)
        kr = kstart + np.arange(NA_KROWS)
        win = np.clip(r - NA_WIN_H // 2, 0, rows - NA_WIN_H)
        row_ok = (kr[None, :] >= win[:, None]) & (kr[None, :] < win[:, None] + NA_WIN_H)
        row_idx = kr[None, :] - r[:, None] + (NA_WIN_H - 1)
        row_pick[v] = (row_idx[:, :, None] == np.arange(n_dr)) & row_ok[:, :, None]
        ok[v] = row_ok[:, None, :, None] & col_ok[None, :, None, :]
    hi = lax.Precision.HIGHEST
    by_col = jnp.einsum("hdk,kqc->hdqc", rpb.astype(F32), col_pick, precision=hi)
    tab = jnp.einsum("vijd,hdqc->hviqjc", row_pick, by_col, precision=hi)
    tab = jnp.where(ok[None], tab, NEG_INF)
    return tab.reshape(NA_HEADS, 3, NA_QROWS * GRID_W, NA_KROWS * GRID_W)


def _softmax_pv(pieces):
    m = functools.reduce(jnp.maximum, [jnp.max(s, axis=-1, keepdims=True) for s, _ in pieces])
    ps = [jnp.exp(s - m) for s, _ in pieces]
    l = functools.reduce(lambda a, b: a + b, [jnp.sum(p, axis=-1, keepdims=True) for p in ps])
    o = functools.reduce(lambda a, b: a + b,
                         [jnp.dot(p.astype(BF16), v, preferred_element_type=F32) for p, (_, v) in zip(ps, pieces)])
    return o / l


_NT_DIMS = (((1,), (1,)), ((), ()))


def _na_kernel(q_ref, k_ref, v_ref, kc_ref, vc_ref, bias_ref, o_ref):
    qb = pl.program_id(2)
    rows = SEQ // GRID_W
    kstart = pl.multiple_of(jnp.clip(qb * NA_QROWS - NA_WIN_H // 2, 0, rows - NA_KROWS) * GRID_W, GRID_W)
    q = q_ref[...]
    kw = k_ref[pl.ds(kstart, NA_KROWS * GRID_W), :]
    vw = v_ref[pl.ds(kstart, NA_KROWS * GRID_W), :]
    s_loc = lax.dot_general(q, kw, _NT_DIMS, preferred_element_type=F32) + bias_ref[0, 0]
    s_ctx = lax.dot_general(q, kc_ref[...], _NT_DIMS, preferred_element_type=F32)
    o_ref[...] = _softmax_pv([(s_loc, vw), (s_ctx, vc_ref[...])]).astype(o_ref.dtype)


def _ctx_attn_kernel(q_ref, k_ref, v_ref, o_ref):
    s = lax.dot_general(q_ref[...], k_ref[...], _NT_DIMS, preferred_element_type=F32)
    o_ref[...] = _softmax_pv([(s, v_ref[...])]).astype(o_ref.dtype)


def neighbourhood_attention(qkv, bias):
    nqb = SEQ // (NA_QROWS * GRID_W)
    tq = NA_QROWS * GRID_W
    ctx_blk = NX // CTX
    lat = pl.pallas_call(
        _na_kernel,
        grid=(BATCH, NA_HEADS, nqb),
        in_specs=[
            pl.BlockSpec((tq, HEAD_DIM), lambda b, h, i: (b * nqb + i, h)),
            pl.BlockSpec((SEQ, HEAD_DIM), lambda b, h, i: (b, NA_HEADS + h)),
            pl.BlockSpec((SEQ, HEAD_DIM), lambda b, h, i: (b, 2 * NA_HEADS + h)),
            pl.BlockSpec((CTX, HEAD_DIM), lambda b, h, i: (ctx_blk + b, NA_HEADS + h)),
            pl.BlockSpec((CTX, HEAD_DIM), lambda b, h, i: (ctx_blk + b, 2 * NA_HEADS + h)),
            pl.BlockSpec((1, 1, tq, NA_KROWS * GRID_W),
                         lambda b, h, i: (h, jnp.where(i == 0, 0, jnp.where(i == nqb - 1, 2, 1)), 0, 0)),
        ],
        out_specs=pl.BlockSpec((tq, HEAD_DIM), lambda b, h, i: (b * nqb + i, h)),
        out_shape=jax.ShapeDtypeStruct((NX, NA_WIDTH), BF16),
        compiler_params=_cp(("parallel", "parallel", "arbitrary")),
        name="na_attn",
    )(qkv, qkv, qkv, qkv, qkv, bias)
    ctx = pl.pallas_call(
        _ctx_attn_kernel,
        grid=(BATCH, NA_HEADS),
        in_specs=[
            pl.BlockSpec((CTX, HEAD_DIM), lambda b, h: (ctx_blk + b, h)),
            pl.BlockSpec((CTX, HEAD_DIM), lambda b, h: (ctx_blk + b, NA_HEADS + h)),
            pl.BlockSpec((CTX, HEAD_DIM), lambda b, h: (ctx_blk + b, 2 * NA_HEADS + h)),
        ],
        out_specs=pl.BlockSpec((CTX, HEAD_DIM), lambda b, h: (b, h)),
        out_shape=jax.ShapeDtypeStruct((BATCH * CTX, NA_WIDTH), BF16),
        compiler_params=_cp(("parallel", "parallel")),
        name="na_ctx_attn",
    )(qkv, qkv, qkv)
    return jnp.concatenate([lat, ctx], axis=0)


_CONV_HALO = 8


def _conv_kernel(prev_ref, cur_ref, next_ref, w_ref, b_ref, cs_ref, o_ref, buf_ref):
    i = pl.program_id(0)
    tm = cur_ref.shape[0]
    tiles_per_seq = SEQ // tm
    n_lat = NX // tm
    first = (i % tiles_per_seq == 0) | (i >= n_lat)
    last = (i % tiles_per_seq == tiles_per_seq - 1) | (i >= n_lat)
    buf_ref[0:_CONV_HALO, :] = prev_ref[...] * jnp.where(first, 0.0, 1.0)
    buf_ref[_CONV_HALO:_CONV_HALO + tm, :] = cur_ref[...]
    buf_ref[_CONV_HALO + tm:, :] = next_ref[...] * jnp.where(last, 0.0, 1.0)
    acc = jnp.zeros(cur_ref.shape, F32) + b_ref[...]
    for j in range(ML_CONV):
        off = _CONV_HALO + j - ML_CONV // 2
        acc = acc + buf_ref[off:off + tm, :] * w_ref[j:j + 1, :]
    o_ref[...] = (_silu(acc) * cs_ref[...]).astype(o_ref.dtype)


def conv_silu(t, w, b, col_scale):
    tm = CTX
    c = t.shape[1]
    hb = tm // _CONV_HALO
    n_halo_blocks = NT // _CONV_HALO
    wp = jnp.zeros((8, c), F32).at[:ML_CONV].set(w)
    return pl.pallas_call(
        _conv_kernel,
        grid=(NT // tm,),
        in_specs=[
            pl.BlockSpec((_CONV_HALO, c), lambda i: (jnp.maximum(i * hb - 1, 0), 0)),
            pl.BlockSpec((tm, c), lambda i: (i, 0)),
            pl.BlockSpec((_CONV_HALO, c), lambda i: (jnp.minimum((i + 1) * hb, n_halo_blocks - 1), 0)),
            pl.BlockSpec((8, c), lambda i: (0, 0)),
            pl.BlockSpec((1, c), lambda i: (0, 0)),
            pl.BlockSpec((1, c), lambda i: (0, 0)),
        ],
        out_specs=pl.BlockSpec((tm, c), lambda i: (i, 0)),
        out_shape=jax.ShapeDtypeStruct((NT, c), BF16),
        scratch_shapes=[pltpu.VMEM((tm + 2 * _CONV_HALO, c), F32)],
        compiler_params=_cp(("parallel",)),
        name="conv_silu",
    )(t, t, t, wp, b.reshape(1, c), col_scale.reshape(1, c))


def _split3(x):
    hi = x.astype(BF16)
    r = x - hi.astype(F32)
    mid = r.astype(BF16)
    lo = (r - mid.astype(F32)).astype(BF16)
    return hi, mid, lo


def _log_sigmoid(x):
    return jnp.minimum(x, 0.0) - jnp.log1p(jnp.exp(-jnp.abs(x)))


def _mlstm_chunk(reverse, q, kt, v_ext, gc, gr, gb_col, gb_row, c_ref, m):
    ln = q.shape[0]
    d = 1 if reverse else 0
    i_col = gc[:, 2 * d:2 * d + 1] + gb_col[:, 2 * d:2 * d + 1]
    f_col = _log_sigmoid(gc[:, 2 * d + 1:2 * d + 2] + gb_col[:, 2 * d + 1:2 * d + 2])
    i_row = gr[2 * d:2 * d + 1, :] + gb_row[2 * d:2 * d + 1, :]
    f_row = _log_sigmoid(gr[2 * d + 1:2 * d + 2, :] + gb_row[2 * d + 1:2 * d + 2, :])
    t_idx = lax.broadcasted_iota(jnp.int32, (ln, ln), 0)
    s_idx = lax.broadcasted_iota(jnp.int32, (ln, ln), 1)
    causal = (s_idx >= t_idx) if reverse else (s_idx <= t_idx)
    tri = jnp.where(causal, 1.0, 0.0).astype(BF16)
    f_col_w = jnp.broadcast_to(f_col, (ln, HEAD_DIM))
    b_col = functools.reduce(lambda a, b: a + b,
                             [jnp.dot(tri, p, preferred_element_type=F32) for p in _split3(f_col_w)])[:, 0:1]
    f_row_w = jnp.broadcast_to(f_row, (16, ln))
    b_row = functools.reduce(lambda a, b: a + b,
                             [lax.dot_general(p, tri, _NT_DIMS, preferred_element_type=F32) for p in _split3(f_row_w)])[0:1, :]
    total = jnp.sum(f_col, axis=0, keepdims=True)

    dmat = jnp.where(causal, b_col - b_row + i_row, -jnp.inf)
    inter = b_col + m
    m_t = jnp.maximum(inter, jnp.max(dmat, axis=-1, keepdims=True))
    w_inter = jnp.exp(inter - m_t)
    s = jnp.dot(q, kt, preferred_element_type=F32) * jnp.exp(dmat - m_t)
    numden = (w_inter * jnp.dot(q, c_ref[...].astype(BF16), preferred_element_type=F32)
              + jnp.dot(s.astype(BF16), v_ext, preferred_element_type=F32))
    den = numden[:, HEAD_DIM:HEAD_DIM + 1]
    h = numden[:, :HEAD_DIM] / jnp.maximum(jnp.abs(den), jnp.exp(-m_t))

    g = total - b_col + i_col
    m_new = jnp.maximum(total + m, jnp.max(g, axis=0, keepdims=True))
    decay = jnp.exp(total + m - m_new)
    wg = jnp.exp(g - m_new)
    upd = jnp.dot(kt, (wg * v_ext.astype(F32)).astype(BF16), preferred_element_type=F32)
    c_ref[...] = decay * c_ref[...] + upd
    return h, m_new


def _mlstm_kernel(q_ref, kt_ref, v_ref, o_ref, qc_ref, ktc_ref, vc_ref, oc_ref, gc_ref, gr_ref, gbc_ref, gbr_ref,
                  hn_ref, out_ref, outc_ref, cf_ref, cb_ref, hf_ref, hb_ref):
    ln = ML_CHUNK
    n_chunks = SEQ // ln
    ones_col = jnp.where(lax.broadcasted_iota(jnp.int32, (ln, HEAD_DIM), 1) == 0, 1.0, 0.0).astype(BF16)
    gb_col = gbc_ref[0]
    gb_row = gbr_ref[0]

    def v_ext(v):
        return jnp.concatenate([v, ones_col], axis=1)

    cf_ref[...] = jnp.zeros(cf_ref.shape, F32)
    cb_ref[...] = jnp.zeros(cb_ref.shape, F32)
    m0 = jnp.zeros((1, 1), F32)

    gc = gc_ref[0, 0:ln, :]
    gr = gr_ref[0, :, 0:ln]
    vx = v_ext(vc_ref[...])
    hf, mf = _mlstm_chunk(False, qc_ref[...], ktc_ref[...], vx, gc, gr, gb_col, gb_row, cf_ref, m0)
    hb, mb = _mlstm_chunk(True, qc_ref[...], ktc_ref[...], vx, gc, gr, gb_col, gb_row, cb_ref, m0)

    def finish(h, o_gate, hn):
        ms = jnp.mean(h * h, axis=-1, keepdims=True)
        return (h * lax.rsqrt(ms + EPS) * hn * jax.nn.sigmoid(o_gate.astype(F32))).astype(BF16)

    hn = hn_ref[...]
    outc_ref[...] = finish(hf + hb, oc_ref[...], hn)

    def body(c, carry):
        mf, mb = carry
        for reverse, c_ref, h_ref in ((False, cf_ref, hf_ref), (True, cb_ref, hb_ref)):
            cc = (n_chunks - 1 - c) if reverse else c
            r0 = pl.multiple_of(cc * ln, ln)
            g0 = pl.multiple_of(cc * ln + CTX, ln)
            h, m_new = _mlstm_chunk(reverse, q_ref[pl.ds(r0, ln), :], kt_ref[:, pl.ds(r0, ln)],
                                    v_ext(v_ref[pl.ds(r0, ln), :]), gc_ref[0, pl.ds(g0, ln), :],
                                    gr_ref[0, :, pl.ds(g0, ln)], gb_col, gb_row, c_ref, mb if reverse else mf)
            h_ref[pl.ds(r0, ln), :] = h
            if reverse:
                mb = m_new
            else:
                mf = m_new
        return mf, mb

    lax.fori_loop(0, n_chunks, body, (mf, mb))

    def fin_body(c, _):
        r0 = pl.multiple_of(c * ln, ln)
        out_ref[pl.ds(r0, ln), :] = finish(hf_ref[pl.ds(r0, ln), :] + hb_ref[pl.ds(r0, ln), :],
                                           o_ref[pl.ds(r0, ln), :], hn)
        return 0

    lax.fori_loop(0, n_chunks, fin_body, 0)


def mlstm(qk, kt, vo, gates_col, gates_row, gate_b, hnorm):
    hh = ML_HEADS
    ctx_blk = NX // CTX
    tot = CTX + SEQ
    gb = gate_b.astype(F32).transpose(2, 0, 1).reshape(hh, 4)
    gb_col = jnp.zeros((hh, 1, HEAD_DIM), F32).at[:, 0, :4].set(gb)
    gb_row = jnp.zeros((hh, 8, 1), F32).at[:, :4, 0].set(gb)
    lat, ctx = pl.pallas_call(
        _mlstm_kernel,
        grid=(BATCH, hh),
        in_specs=[
            pl.BlockSpec((SEQ, HEAD_DIM), lambda b, h: (b, h)),
            pl.BlockSpec((HEAD_DIM, SEQ), lambda b, h: (h, b)),
            pl.BlockSpec((SEQ, HEAD_DIM), lambda b, h: (b, h)),
            pl.BlockSpec((SEQ, HEAD_DIM), lambda b, h: (b, hh + h)),
            pl.BlockSpec((CTX, HEAD_DIM), lambda b, h: (ctx_blk + b, h)),
            pl.BlockSpec((HEAD_DIM, CTX), lambda b, h: (h, ctx_blk + b)),
            pl.BlockSpec((CTX, HEAD_DIM), lambda b, h: (ctx_blk + b, h)),
            pl.BlockSpec((CTX, HEAD_DIM), lambda b, h: (ctx_blk + b, hh + h)),
            pl.BlockSpec((1, tot, HEAD_DIM), lambda b, h: (b * hh + h, 0, 0)),
            pl.BlockSpec((1, 8, tot), lambda b, h: (b * hh + h, 0, 0)),
            pl.BlockSpec((1, 1, HEAD_DIM), lambda b, h: (h, 0, 0)),
            pl.BlockSpec((1, 8, 1), lambda b, h: (h, 0, 0)),
            pl.BlockSpec((1, HEAD_DIM), lambda b, h: (0, h)),
        ],
        out_specs=[pl.BlockSpec((SEQ, HEAD_DIM), lambda b, h: (b, h)), pl.BlockSpec((CTX, HEAD_DIM), lambda b, h: (b, h))],
        out_shape=[jax.ShapeDtypeStruct((NX, ML_WIDTH), BF16), jax.ShapeDtypeStruct((BATCH * CTX, ML_WIDTH), BF16)],
        scratch_shapes=[
            pltpu.VMEM((HEAD_DIM, 2 * HEAD_DIM), F32),
            pltpu.VMEM((HEAD_DIM, 2 * HEAD_DIM), F32),
            pltpu.VMEM((SEQ, HEAD_DIM), F32),
            pltpu.VMEM((SEQ, HEAD_DIM), F32),
        ],
        compiler_params=_cp(("parallel", "parallel")),
        name="mlstm",
    )(qk, kt, vo, vo, qk, kt, vo, vo, gates_col, gates_row, gb_col, gb_row, hnorm.reshape(1, ML_WIDTH))
    return jnp.concatenate([lat, ctx], axis=0)


def _diff_attn_kernel(q0_ref, q1_ref, k0_ref, k1_ref, k0c_ref, k1c_ref, v_ref, vc_ref, lam_ref, sub_ref, o_ref,
                      *, lambda_init):
    lam = lam_ref[...]
    lam_full = (jnp.exp(jnp.sum(lam[0:1] * lam[1:2], axis=-1, keepdims=True))
                - jnp.exp(jnp.sum(lam[2:3] * lam[3:4], axis=-1, keepdims=True)) + lambda_init)

    tq = q0_ref.shape[0]
    qs = (q0_ref[...], q1_ref[...])

    def step(carry, ks, v):
        out = []
        for (m, l, acc), q, k in zip(carry, qs, ks):
            s = lax.dot_general(q, k, _NT_DIMS, preferred_element_type=F32)
            m_new = jnp.maximum(m, jnp.max(s, axis=-1, keepdims=True))
            alpha = jnp.exp2(m - m_new)
            p = jnp.exp2(s - m_new)
            l = alpha * l + jnp.sum(p, axis=-1, keepdims=True)
            acc = alpha * acc + jnp.dot(p.astype(BF16), v, preferred_element_type=F32)
            out.append((m_new, l, acc))
        return tuple(out)

    init = tuple((jnp.full((tq, 1), -jnp.inf, F32), jnp.zeros((tq, 1), F32), jnp.zeros((tq, DA_VDIM), F32))
                 for _ in range(2))

    carry = init
    for c in range(SEQ // DA_KCHUNK):
        rows = slice(c * DA_KCHUNK, (c + 1) * DA_KCHUNK)
        carry = step(carry, (k0_ref[rows, :], k1_ref[rows, :]), v_ref[rows, :])
    (_, l0, acc0), (_, l1, acc1) = step(carry, (k0c_ref[...], k1c_ref[...]), vc_ref[...])
    o = acc0 / l0 - lam_full * (acc1 / l1)
    ms = jnp.mean(o * o, axis=-1, keepdims=True)
    o_ref[...] = (o * lax.rsqrt(ms + EPS) * sub_ref[...] * (1.0 - lambda_init)).astype(o_ref.dtype)


def diff_attention(qkv, lam, subln, lambda_init, tq=256):
    nq = SEQ // tq
    ctx_blk = NX // CTX
    kcol = 2 * DA_HEADS
    vcol = 2 * DA_HEADS
    return pl.pallas_call(
        functools.partial(_diff_attn_kernel, lambda_init=lambda_init),
        grid=(BATCH, DA_HEADS, nq),
        in_specs=[
            pl.BlockSpec((tq, HEAD_DIM), lambda b, h, i: (b * nq + i, 2 * h)),
            pl.BlockSpec((tq, HEAD_DIM), lambda b, h, i: (b * nq + i, 2 * h + 1)),
            pl.BlockSpec((SEQ, HEAD_DIM), lambda b, h, i: (b, kcol + 2 * h)),
            pl.BlockSpec((SEQ, HEAD_DIM), lambda b, h, i: (b, kcol + 2 * h + 1)),
            pl.BlockSpec((CTX, HEAD_DIM), lambda b, h, i: (ctx_blk + b, kcol + 2 * h)),
            pl.BlockSpec((CTX, HEAD_DIM), lambda b, h, i: (ctx_blk + b, kcol + 2 * h + 1)),
            pl.BlockSpec((SEQ, DA_VDIM), lambda b, h, i: (b, vcol + h)),
            pl.BlockSpec((CTX, DA_VDIM), lambda b, h, i: (ctx_blk + b, vcol + h)),
            pl.BlockSpec((4, HEAD_DIM), lambda b, h, i: (0, 0)),
            pl.BlockSpec((1, DA_VDIM), lambda b, h, i: (0, 0)),
        ],
        out_specs=pl.BlockSpec((tq, DA_VDIM), lambda b, h, i: (b * nq + i, h)),
        out_shape=jax.ShapeDtypeStruct((NX, DA_HEADS * DA_VDIM), BF16),
        compiler_params=_cp(("parallel", "parallel", "arbitrary"), vmem_mb=56),
        name="diff_attn",
    )(qkv, qkv, qkv, qkv, qkv, qkv, qkv, qkv, lam.astype(F32), subln.reshape(1, DA_VDIM).astype(F32))


def dispatch_tables(gates_t, n_rows):
    nt = n_rows // TOK_TILE
    max_units = (n_rows * TOP_K + nt * N_EXPERTS * (UNIT - 1)) // UNIT
    n_ffn_tiles = (max_units + N_EXPERTS * (FFN_UNITS - 1)) // FFN_UNITS + 1
    gates = gates_t.T.reshape(nt, TOK_TILE, N_EXPERTS)
    sel = gates > 0
    self32 = sel.astype(F32)
    hi = lax.Precision.HIGHEST
    before_t = np.tril(np.ones((TOK_TILE, TOK_TILE), np.float32), -1)
    before_e = np.triu(np.ones((N_EXPERTS, N_EXPERTS), np.float32), 1)
    rank = jnp.einsum("ts,nse->nte", before_t, self32, precision=hi).astype(jnp.int32)
    kr = jnp.einsum("ntf,fe->nte", self32, before_e, precision=hi).astype(jnp.int32)
    cnt = self32.sum(axis=1).astype(jnp.int32)
    nun = (cnt + UNIT - 1) // UNIT
    loc_off = jnp.cumsum(nun, axis=1) - nun
    slot = loc_off[:, None, :] * UNIT + rank
    hit = sel[:, :, None, :] & (kr[:, :, None, :] == jnp.arange(TOP_K)[None, None, :, None])
    slot6 = jnp.where(hit.any(axis=-1), jnp.sum(jnp.where(hit, slot[:, :, None, :], 0), axis=-1), -1)
    w6 = jnp.sum(jnp.where(hit, gates[:, :, None, :], 0.0), axis=-1)
    slot6 = jnp.pad(slot6, ((0, 0), (0, 0), (0, 8 - TOP_K)), constant_values=-1)
    w6 = jnp.pad(w6, ((0, 0), (0, 0), (0, 8 - TOP_K)))

    seg_un = nun.sum(axis=0)
    seg_pad = (seg_un + FFN_UNITS - 1) // FFN_UNITS * FFN_UNITS
    seg_end = jnp.cumsum(seg_pad)
    seg_start = seg_end - seg_pad
    gstart = seg_start[None, :] + jnp.cumsum(nun, axis=0) - nun
    u = jnp.arange(UNITS_PER_TILE, dtype=jnp.int32)
    loc_end = loc_off + nun
    ue = (loc_end[:, None, :] <= u[None, :, None]).sum(axis=-1)
    onehot = ue[:, :, None] == jnp.arange(N_EXPERTS)[None, None, :]
    dst = jnp.sum(jnp.where(onehot, (gstart - loc_off)[:, None, :], 0), axis=-1) + u[None, :]
    n_units_total = n_ffn_tiles * FFN_UNITS
    flat_dst = jnp.where(ue < N_EXPERTS, dst, n_units_total).reshape(-1).astype(jnp.int32)
    src_write = jnp.full((n_units_total + 1,), -1, jnp.int32).at[flat_dst].set(
        jnp.arange(nt * UNITS_PER_TILE, dtype=jnp.int32))[:n_units_total]
    src_read = jnp.where(src_write >= 0, src_write, UNITS_PER_TILE - 1)
    chunk_start = jnp.concatenate([seg_start, seg_end[-1:]]).astype(jnp.int32) // FFN_UNITS
    return dict(slot6=slot6, w6=w6, slot6_t=slot6.transpose(0, 2, 1), src_read=src_read, src_write=src_write,
                chunk_start=chunk_start, nt=nt)


def _moe_gather_kernel(h_ref, slot_ref, o_ref):
    s_iota = lax.broadcasted_iota(jnp.int32, (SLOTS, TOK_TILE), 0)
    slots = slot_ref[0]
    p = jnp.zeros((SLOTS, TOK_TILE), F32)
    for k in range(TOP_K):
        p = jnp.where(s_iota == slots[k:k + 1, :], 1.0, p)
    p = p.astype(BF16)
    o_ref[0] = jnp.dot(p, h_ref[...], preferred_element_type=F32).astype(BF16)


def moe_gather(h, slot6_t, nt):
    return pl.pallas_call(
        _moe_gather_kernel,
        grid=(nt,),
        in_specs=[pl.BlockSpec((TOK_TILE, D), lambda i: (i, 0)), pl.BlockSpec((1, 8, TOK_TILE), lambda i: (i, 0, 0))],
        out_specs=pl.BlockSpec((1, SLOTS, D), lambda i: (i, 0, 0)),
        out_shape=jax.ShapeDtypeStruct((nt, SLOTS, D), BF16),
        compiler_params=_cp(("parallel",)),
        name="moe_gather",
    )(h, slot6_t)


def _unit_copy(src_hbm, buf_ref, sem_ref, slot, src_unit, j):
    return pltpu.make_async_copy(src_hbm.at[pl.ds(pl.multiple_of(src_unit * UNIT, UNIT), UNIT)],
                                 buf_ref.at[slot, pl.ds(j * UNIT, UNIT)], sem_ref.at[slot])


def _fetch_units(table_ref, base, n_units, src_hbm, buf_ref, sem_ref, slot):
    def body(j, _):
        _unit_copy(src_hbm, buf_ref, sem_ref, slot, table_ref[base + j], j).start()
        return 0

    lax.fori_loop(0, n_units, body, 0, unroll=8)


def _wait_units(n_units, src_hbm, buf_ref, sem_ref, slot):
    pltpu.make_async_copy(src_hbm.at[pl.ds(0, n_units * UNIT)], buf_ref.at[slot], sem_ref.at[slot]).wait()


def _moe_ffn_kernel(srcr_ref, srcw_ref, cs_ref, x_hbm, wg_ref, wu_ref, wd_ref, y_hbm, xbuf_ref, ybuf_ref,
                    sem_in, sem_out, wgb_ref, wub_ref, wdb_ref):
    e = pl.program_id(0)
    lo = cs_ref[e]
    hi = cs_ref[e + 1]
    total = cs_ref[N_EXPERTS]

    def out_units(c, slot, start):
        for j in range(FFN_UNITS):
            su = srcw_ref[c * FFN_UNITS + j]

            @pl.when(su >= 0)
            def _():
                cp = pltpu.make_async_copy(ybuf_ref.at[slot, pl.ds(j * UNIT, UNIT)],
                                           y_hbm.at[pl.ds(pl.multiple_of(su * UNIT, UNIT), UNIT)], sem_out.at[slot])
                if start:
                    cp.start()
                else:
                    cp.wait()

    @pl.when(hi > lo)
    def _():
        wgb_ref[...] = wg_ref[0].astype(BF16)
        wub_ref[...] = wu_ref[0].astype(BF16)
        wdb_ref[...] = wd_ref[0].astype(BF16)

    @pl.when((e == 0) & (total > 0))
    def _():
        _fetch_units(srcr_ref, 0, FFN_UNITS, x_hbm, xbuf_ref, sem_in, 0)

    def chunk(c, _):
        slot = c % 2

        @pl.when(c + 1 < total)
        def _():
            _fetch_units(srcr_ref, (c + 1) * FFN_UNITS, FFN_UNITS, x_hbm, xbuf_ref, sem_in, 1 - slot)

        _wait_units(FFN_UNITS, x_hbm, xbuf_ref, sem_in, slot)

        @pl.when(c >= 2)
        def _():
            out_units(c - 2, slot, False)

        x = xbuf_ref[slot]
        g = jnp.dot(x, wgb_ref[...], preferred_element_type=F32)
        u = jnp.dot(x, wub_ref[...], preferred_element_type=F32)
        a = (_silu(g) * u).astype(BF16)
        ybuf_ref[slot] = jnp.dot(a, wdb_ref[...], preferred_element_type=F32).astype(BF16)
        out_units(c, slot, True)
        return 0

    lax.fori_loop(lo, hi, chunk, 0)

    @pl.when(e == N_EXPERTS - 1)
    def _():
        for back in (2, 1):
            c = total - back

            @pl.when(c >= 0)
            def _():
                out_units(c, c % 2, False)


def moe_ffn(x_tiles, tabs, wg, wu, wd):
    x_flat = x_tiles.reshape(-1, D)
    idx = lambda e, *_: (e, 0, 0)
    grid_spec = pltpu.PrefetchScalarGridSpec(
        num_scalar_prefetch=3,
        grid=(N_EXPERTS,),
        in_specs=[
            pl.BlockSpec(memory_space=pl.ANY),
            pl.BlockSpec((1, D, D_EXPERT), idx),
            pl.BlockSpec((1, D, D_EXPERT), idx),
            pl.BlockSpec((1, D_EXPERT, D), idx),
        ],
        out_specs=pl.BlockSpec(memory_space=pl.ANY),
        scratch_shapes=[pltpu.VMEM((2, FFN_TM, D), BF16), pltpu.VMEM((2, FFN_TM, D), BF16),
                        pltpu.SemaphoreType.DMA((2,)), pltpu.SemaphoreType.DMA((2,)),
                        pltpu.VMEM((D, D_EXPERT), BF16), pltpu.VMEM((D, D_EXPERT), BF16),
                        pltpu.VMEM((D_EXPERT, D), BF16)],
    )
    y = pl.pallas_call(
        _moe_ffn_kernel,
        grid_spec=grid_spec,
        out_shape=jax.ShapeDtypeStruct(x_flat.shape, BF16),
        input_output_aliases={3: 0},
        compiler_params=_cp(("arbitrary",), vmem_mb=56),
        name="moe_ffn",
    )(tabs["src_read"], tabs["src_write"], tabs["chunk_start"], x_flat, wg, wu, wd)
    return y.reshape(x_tiles.shape)


def _shared_ffn_kernel(a_ref, wg_ref, wu_ref, wd_ref, o_ref):
    a = a_ref[...]
    g = jnp.dot(a, wg_ref[...], preferred_element_type=F32)
    u = jnp.dot(a, wu_ref[...], preferred_element_type=F32)
    o_ref[...] = jnp.dot((_silu(g) * u).astype(BF16), wd_ref[...], preferred_element_type=F32).astype(o_ref.dtype)


def shared_ffn(h, wg, wu, wd, n_rows, tm=512):
    return pl.pallas_call(
        _shared_ffn_kernel,
        grid=(n_rows // tm,),
        in_specs=[
            pl.BlockSpec((tm, D), lambda i: (i, 0)),
            pl.BlockSpec((D, D_EXPERT), lambda i: (0, 0)),
            pl.BlockSpec((D, D_EXPERT), lambda i: (0, 0)),
            pl.BlockSpec((D_EXPERT, D), lambda i: (0, 0)),
        ],
        out_specs=pl.BlockSpec((tm, D), lambda i: (i, 0)),
        out_shape=jax.ShapeDtypeStruct((n_rows, D), BF16),
        compiler_params=_cp(("parallel",)),
        name="shared_ffn",
    )(h, wg, wu, wd)


def _moe_combine_kernel(y_ref, slot_ref, w_ref, sh_ref, res_ref, gm_ref, o_ref):
    lane = lax.broadcasted_iota(jnp.int32, (TOK_TILE, SLOTS), 1)
    slots = slot_ref[0]
    w = w_ref[0]
    pw = jnp.zeros((TOK_TILE, SLOTS), F32)
    for k in range(TOP_K):
        pw = jnp.where(lane == slots[:, k:k + 1], w[:, k:k + 1], pw)
    routed = jnp.dot(pw.astype(BF16), y_ref[0], preferred_element_type=F32)
    o_ref[...] = res_ref[...] + gm_ref[0] * (routed + sh_ref[...].astype(F32))


def moe_combine(y_tiles, tabs, shared, res, mods, n_rows):
    nt = tabs["nt"]
    tm = TOK_TILE
    return pl.pallas_call(
        _moe_combine_kernel,
        grid=(nt,),
        in_specs=[
            pl.BlockSpec((1, SLOTS, D), lambda i: (i, 0, 0)),
            pl.BlockSpec((1, tm, 8), lambda i: (i, 0, 0)),
            pl.BlockSpec((1, tm, 8), lambda i: (i, 0, 0)),
            pl.BlockSpec((tm, D), lambda i: (i, 0)),
            pl.BlockSpec((tm, D), lambda i: (i, 0)),
            pl.BlockSpec((1, 1, D), lambda i: (_seg_of_tile(i, tm) * 6 + 5, 0, 0)),
        ],
        out_specs=pl.BlockSpec((tm, D), lambda i: (i, 0)),
        out_shape=jax.ShapeDtypeStruct((n_rows, D), F32),
        compiler_params=_cp(("parallel",), vmem_mb=56),
        name="moe_combine",
    )(y_tiles, tabs["slot6"], tabs["w6"], shared, res, mods)


def moe_block(xa, gain, mods, router_w, router_b, wg, wu, wd, sg, su, sd, n_rows):
    h, gates_t = norm_route(xa, gain, mods, router_w, router_b, n_rows)
    tabs = dispatch_tables(gates_t, n_rows)
    x_tiles = moe_gather(h, tabs["slot6_t"], tabs["nt"])
    y_tiles = moe_ffn(x_tiles, tabs, wg, wu, wd)
    shared = shared_ffn(h, sg.astype(BF16), su.astype(BF16), sd.astype(BF16), n_rows)
    return moe_combine(y_tiles, tabs, shared, xa, mods, n_rows)


def rope_tables():
    t = jnp.arange(SEQ)
    row = (t // GRID_W).astype(F32)
    col = (t % GRID_W).astype(F32)
    n_freq = HEAD_DIM // 4
    inv_freq = ROPE_THETA ** (-jnp.arange(n_freq, dtype=F32) / n_freq)
    ang = jnp.concatenate([row[:, None] * inv_freq, col[:, None] * inv_freq], axis=-1)
    ang = jnp.concatenate([ang, ang], axis=-1)
    sign = jnp.where(jnp.arange(HEAD_DIM) < HEAD_DIM // 2, -1.0, 1.0)
    cos = jnp.concatenate([jnp.cos(ang)] * BATCH + [jnp.ones((BATCH * CTX, HEAD_DIM), F32)], axis=0)
    sin = jnp.concatenate([jnp.sin(ang) * sign] * BATCH + [jnp.zeros((BATCH * CTX, HEAD_DIM), F32)], axis=0)
    return cos, sin


def even_layer(xa, mods, norm1, w_in, na_qnorm, na_knorm, na_rpb, conv_w, conv_b, gate_b, hnorm, w_out):
    h = norm_mod(xa, norm1, mods, 0, NT)
    w = w_in.astype(BF16)
    c0 = 3 * NA_WIDTH
    c1 = c0 + 2 * ML_WIDTH
    c2 = c1 + 2 * ML_WIDTH
    scale = HEAD_DIM ** -0.5
    gain = jnp.concatenate([jnp.tile(na_qnorm.astype(F32) * scale, NA_HEADS), jnp.tile(na_knorm.astype(F32), NA_HEADS),
                            jnp.ones((NA_WIDTH,), F32)]).reshape(1, c0)
    qkv = proj(h, w[:, :c0], BF16, gain=gain, n_norm_cols=2 * NA_WIDTH)
    na_h = neighbourhood_attention(qkv, na_bias_table(na_rpb))

    ml_qk = proj(h, w[:, c0:c1], F32)
    ml_vo = proj(h, w[:, c1:c2], BF16)
    n_gate = w_in.shape[1] - c2
    w_gate = jnp.zeros((D, HEAD_DIM), BF16).at[:, :n_gate].set(w[:, c2:])
    g = proj(h, w_gate, F32, tn=HEAD_DIM)[:, :n_gate]
    col_scale = jnp.concatenate([jnp.ones((ML_WIDTH,), F32), jnp.full((ML_WIDTH,), HEAD_DIM ** -0.5, F32)])
    qk = conv_silu(ml_qk, conv_w.astype(F32), conv_b.astype(F32), col_scale)
    kt = qk[:, ML_WIDTH:].T
    g = g.reshape(NT, 4, ML_HEADS)
    g = jnp.concatenate([g[NX:].reshape(BATCH, CTX, 4, ML_HEADS), g[:NX].reshape(BATCH, SEQ, 4, ML_HEADS)], axis=1)
    g = g.transpose(0, 3, 1, 2).reshape(BATCH * ML_HEADS, CTX + SEQ, 4)
    g_col = jnp.zeros((BATCH * ML_HEADS, CTX + SEQ, HEAD_DIM), F32).at[:, :, :4].set(g)
    g_row = jnp.zeros((BATCH * ML_HEADS, 8, CTX + SEQ), F32).at[:, :4, :].set(g.transpose(0, 2, 1))
    ml_h = mlstm(qk, kt, ml_vo, g_col, g_row, gate_b, hnorm.astype(F32))

    wo = w_out.astype(BF16)
    return out_proj([na_h, ml_h], [wo[:NA_WIDTH], wo[NA_WIDTH:]], xa, mods, 2, NT)


def odd_layer(xa, mods, norm1, w_in, qnorm, knorm, lam, subln, w_out, lambda_init):
    h = norm_mod(xa, norm1, mods, 0, NT)
    scale = HEAD_DIM ** -0.5 * math.log2(math.e)
    n_qk = 2 * DA_HEADS * HEAD_DIM
    gain = jnp.concatenate([jnp.tile(qnorm.astype(F32) * scale, 2 * DA_HEADS), jnp.tile(knorm.astype(F32), 2 * DA_HEADS),
                            jnp.ones((DA_HEADS * DA_VDIM,), F32)]).reshape(1, -1)
    cos, sin = rope_tables()
    qkv = proj(h, w_in.astype(BF16), BF16, gain=gain, n_norm_cols=2 * n_qk, cos=cos, sin=sin)
    o = diff_attention(qkv, lam, subln, lambda_init)
    return out_proj([o], [w_out.astype(BF16)], xa, mods, 2, NX)


def diff_lambda_init(layer):
    return 0.8 - 0.6 * math.exp(-0.3 * layer)


def kernel(x, c, ctx, c_ctx, l0_ada_w, l0_ada_b, l0_norm1, l0_norm2, l0_w_in, l0_na_qnorm, l0_na_knorm, l0_na_rpb, l0_ml_conv_w, l0_ml_conv_b, l0_ml_gate_b, l0_ml_hnorm, l0_w_out, l0_router_w, l0_router_b, l0_exp_gate, l0_exp_up, l0_exp_down, l0_sh_gate, l0_sh_up, l0_sh_down, l1_ada_w, l1_ada_b, l1_norm1, l1_norm2, l1_w_in, l1_qnorm, l1_knorm, l1_lambda, l1_subln, l1_w_out, l1_router_w, l1_router_b, l1_exp_gate, l1_exp_up, l1_exp_down, l1_sh_gate, l1_sh_up, l1_sh_down):
    assert x.shape == (BATCH, SEQ, D) and ctx.shape == (BATCH, CTX, D)
    xa = jnp.concatenate([x.reshape(NX, D), ctx.reshape(BATCH * CTX, D)], axis=0).astype(F32)
    cvec = jnp.zeros((8, D), F32).at[:BATCH].set(c).at[BATCH].set(c_ctx)

    mods0 = adaln(cvec, l0_ada_w, l0_ada_b)
    xa = even_layer(xa, mods0, l0_norm1, l0_w_in, l0_na_qnorm, l0_na_knorm, l0_na_rpb, l0_ml_conv_w, l0_ml_conv_b,
                    l0_ml_gate_b, l0_ml_hnorm, l0_w_out)
    xa = moe_block(xa, l0_norm2, mods0, l0_router_w, l0_router_b, l0_exp_gate, l0_exp_up, l0_exp_down,
                   l0_sh_gate, l0_sh_up, l0_sh_down, NT)

    mods1 = adaln(cvec, l1_ada_w, l1_ada_b)
    xl = odd_layer(xa, mods1, l1_norm1, l1_w_in, l1_qnorm, l1_knorm, l1_lambda, l1_subln, l1_w_out, diff_lambda_init(1))
    xl = moe_block(xl, l1_norm2, mods1, l1_router_w, l1_router_b, l1_exp_gate, l1_exp_up, l1_exp_down,
                   l1_sh_gate, l1_sh_up, l1_sh_down, NX)
    return xl.reshape(BATCH, SEQ, D)
```

```python
import functools
import math

import jax
import jax.numpy as jnp
import numpy as np
from jax import lax
from jax.experimental import pallas as pl
from jax.experimental.pallas import tpu as pltpu

F32 = jnp.float32
BF16 = jnp.bfloat16

D = 2048
BATCH = 2
SEQ = 4096
CTX = 256
NX = BATCH * SEQ
NT = NX + BATCH * CTX
GRID_W = 64
EPS = 1e-6
NEG_INF = -1e30

NA_HEADS = 8
HEAD_DIM = 128
NA_WIDTH = NA_HEADS * HEAD_DIM
NA_WIN_H = 8
NA_WIN_W = 16
NA_QROWS = 4
NA_KROWS = 12
ML_HEADS = 8
ML_WIDTH = ML_HEADS * HEAD_DIM
ML_CONV = 5
ML_CHUNK = 256

DA_HEADS = 8
DA_VDIM = 256
DA_KCHUNK = 1024
ROPE_THETA = 10000.0

N_EXPERTS = 64
N_GROUPS = 8
TOPK_GROUPS = 4
TOP_K = 6
D_EXPERT = 512
ROUTED_SCALE = 2.5

TOK_TILE = 256
UNIT = 16
UNITS_PER_TILE = (TOK_TILE * TOP_K + N_EXPERTS * (UNIT - 1)) // UNIT + 1
UNITS_PER_TILE = -(-UNITS_PER_TILE // 32) * 32
SLOTS = UNITS_PER_TILE * UNIT
FFN_TM = 256
FFN_UNITS = FFN_TM // UNIT
FFN_IN_DEPTH = 3

V7X_VMEM_BYTES = 64 * 1024 * 1024


def _cp(sem, vmem_mb=48):
    assert vmem_mb * 1024 * 1024 < V7X_VMEM_BYTES
    return pltpu.CompilerParams(dimension_semantics=sem, vmem_limit_bytes=vmem_mb * 1024 * 1024)


def _silu(x):
    return x * jax.nn.sigmoid(x)


def _seg_of_tile(i, tm):
    return (i * tm) // SEQ


def _ada_kernel(c_ref, w_ref, b_ref, o_ref):
    s = _silu(c_ref[...]).astype(BF16)
    o_ref[...] = jnp.dot(s, w_ref[...].astype(BF16), preferred_element_type=F32) + b_ref[...]


def adaln(cvec, w, b):
    n = w.shape[1]
    tn = 1024
    out = pl.pallas_call(
        _ada_kernel,
        grid=(n // tn,),
        in_specs=[
            pl.BlockSpec((8, D), lambda j: (0, 0)),
            pl.BlockSpec((D, tn), lambda j: (0, j)),
            pl.BlockSpec((1, tn), lambda j: (0, j)),
        ],
        out_specs=pl.BlockSpec((8, tn), lambda j: (0, j)),
        out_shape=jax.ShapeDtypeStruct((8, n), F32),
        compiler_params=_cp(("arbitrary",)),
        name="adaln",
    )(cvec, w, b.reshape(1, n))
    return out[:3].reshape(18, 1, D)


def _normed(x_ref, g_ref, sh_ref, sc_ref):
    x = x_ref[...]
    ms = jnp.mean(x * x, axis=-1, keepdims=True)
    y = x * lax.rsqrt(ms + EPS) * g_ref[...]
    return y * (1.0 + sc_ref[0]) + sh_ref[0]


def _norm_mod_kernel(x_ref, g_ref, sh_ref, sc_ref, o_ref):
    o_ref[...] = _normed(x_ref, g_ref, sh_ref, sc_ref).astype(o_ref.dtype)


def _mod_spec(which, tm):
    return pl.BlockSpec((1, 1, D), lambda i: (_seg_of_tile(i, tm) * 6 + which, 0, 0))


def norm_mod(x, gain, mods, which_shift, n_rows, tm=256):
    return pl.pallas_call(
        _norm_mod_kernel,
        grid=(n_rows // tm,),
        in_specs=[
            pl.BlockSpec((tm, D), lambda i: (i, 0)),
            pl.BlockSpec((1, D), lambda i: (0, 0)),
            _mod_spec(which_shift, tm),
            _mod_spec(which_shift + 1, tm),
        ],
        out_specs=pl.BlockSpec((tm, D), lambda i: (i, 0)),
        out_shape=jax.ShapeDtypeStruct((n_rows, D), BF16),
        compiler_params=_cp(("parallel",)),
        name="norm_mod",
    )(x, gain.reshape(1, D), mods, mods)


def _route(logits, bias_col):
    tm = logits.shape[1]
    per_group = N_EXPERTS // N_GROUPS
    scores = jax.nn.sigmoid(logits)
    sel = scores + bias_col
    row8 = lax.broadcasted_iota(jnp.int32, (per_group, tm), 0)
    grp = jnp.zeros((N_GROUPS, tm), F32)
    for g in range(N_GROUPS):
        slab = sel[g * per_group:(g + 1) * per_group, :]
        m1 = jnp.max(slab, axis=0, keepdims=True)
        first = jnp.min(jnp.where(slab == m1, row8, per_group), axis=0, keepdims=True)
        m2 = jnp.max(jnp.where(row8 == first, -jnp.inf, slab), axis=0, keepdims=True)
        grp = jnp.where(row8 == g, m1 + m2, grp)
    rank = jnp.zeros((N_GROUPS, tm), jnp.int32)
    for g in range(N_GROUPS):
        vg = grp[g:g + 1, :]
        beats = (vg > grp) | ((vg == grp) & (g < row8))
        rank = rank + jnp.where(beats, 1, 0)
    keep = jnp.where(rank < TOPK_GROUPS, 1.0, 0.0)
    cur = jnp.concatenate(
        [jnp.where(keep[g:g + 1, :] > 0.5, sel[g * per_group:(g + 1) * per_group, :], NEG_INF) for g in range(N_GROUPS)],
        axis=0)
    e_iota = lax.broadcasted_iota(jnp.int32, (N_EXPERTS, tm), 0)
    picked = jnp.zeros((N_EXPERTS, tm), F32)
    for _ in range(TOP_K):
        m = jnp.max(cur, axis=0, keepdims=True)
        idx = jnp.min(jnp.where(cur == m, e_iota, N_EXPERTS), axis=0, keepdims=True)
        hit = e_iota == idx
        picked = jnp.where(hit, 1.0, picked)
        cur = jnp.where(hit, -jnp.inf, cur)
    w = scores * picked
    return w / jnp.sum(w, axis=0, keepdims=True) * ROUTED_SCALE


def _norm_route_kernel(x_ref, g_ref, sh_ref, sc_ref, rwh_ref, rwl_ref, rb_ref, o_ref, gates_ref):
    h = _normed(x_ref, g_ref, sh_ref, sc_ref)
    h_hi = h.astype(BF16)
    o_ref[...] = h_hi
    h_lo = (h - h_hi.astype(F32)).astype(BF16)
    nt = (((1,), (1,)), ((), ()))
    logits = (lax.dot_general(rwh_ref[...], h_hi, nt, preferred_element_type=F32)
              + lax.dot_general(rwh_ref[...], h_lo, nt, preferred_element_type=F32)
              + lax.dot_general(rwl_ref[...], h_hi, nt, preferred_element_type=F32))
    gates_ref[...] = _route(logits, rb_ref[...])


def norm_route(x, gain, mods, router_w, router_b, n_rows):
    tm = TOK_TILE
    rwt = router_w.T
    rw_hi = rwt.astype(BF16)
    rw_lo = (rwt - rw_hi.astype(F32)).astype(BF16)
    return pl.pallas_call(
        _norm_route_kernel,
        grid=(n_rows // tm,),
        in_specs=[
            pl.BlockSpec((tm, D), lambda i: (i, 0)),
            pl.BlockSpec((1, D), lambda i: (0, 0)),
            _mod_spec(3, tm),
            _mod_spec(4, tm),
            pl.BlockSpec((N_EXPERTS, D), lambda i: (0, 0)),
            pl.BlockSpec((N_EXPERTS, D), lambda i: (0, 0)),
            pl.BlockSpec((N_EXPERTS, 1), lambda i: (0, 0)),
        ],
        out_specs=[pl.BlockSpec((tm, D), lambda i: (i, 0)), pl.BlockSpec((N_EXPERTS, tm), lambda i: (0, i))],
        out_shape=[jax.ShapeDtypeStruct((n_rows, D), BF16), jax.ShapeDtypeStruct((N_EXPERTS, n_rows), F32)],
        compiler_params=_cp(("parallel",)),
        name="norm_route",
    )(x, gain.reshape(1, D), mods, mods, rw_hi, rw_lo, router_b.reshape(N_EXPERTS, 1))


def _head_norm(acc, gain, g):
    a = acc[:, g * HEAD_DIM:(g + 1) * HEAD_DIM]
    ms = jnp.mean(a * a, axis=-1, keepdims=True)
    return a * lax.rsqrt(ms + EPS) * gain[:, g * HEAD_DIM:(g + 1) * HEAD_DIM]


def _proj_kernel(*refs, n_norm_tiles, rope):
    if rope:
        a_ref, w_ref, gain_ref, cos_ref, sin_ref, o_ref = refs
    else:
        a_ref, w_ref, gain_ref, o_ref = refs
    j = pl.program_id(0)
    acc = jnp.dot(a_ref[...], w_ref[...], preferred_element_type=F32)
    tn = acc.shape[1]

    @pl.when(j < n_norm_tiles)
    def _():
        gain = gain_ref[...]
        for g in range(tn // HEAD_DIM):
            y = _head_norm(acc, gain, g)
            if rope:
                y = y * cos_ref[...] + pltpu.roll(y, HEAD_DIM // 2, axis=1) * sin_ref[...]
            o_ref[:, g * HEAD_DIM:(g + 1) * HEAD_DIM] = y.astype(o_ref.dtype)

    @pl.when(j >= n_norm_tiles)
    def _():
        o_ref[...] = acc.astype(o_ref.dtype)


def proj(a, w, out_dtype, *, gain=None, n_norm_cols=0, cos=None, sin=None, tm=512, tn=1024):
    m, k = a.shape
    n = w.shape[1]
    tn = min(tn, n)
    rope = cos is not None
    if gain is None:
        gain = jnp.ones((1, n), F32)
    in_specs = [
        pl.BlockSpec((tm, k), lambda j, i: (i, 0)),
        pl.BlockSpec((k, tn), lambda j, i: (0, j)),
        pl.BlockSpec((1, tn), lambda j, i: (0, j)),
    ]
    args = [a, w, gain]
    if rope:
        in_specs += [pl.BlockSpec((tm, HEAD_DIM), lambda j, i: (i, 0))] * 2
        args += [cos, sin]
    assert n_norm_cols % tn == 0
    return pl.pallas_call(
        functools.partial(_proj_kernel, n_norm_tiles=n_norm_cols // tn, rope=rope),
        grid=(n // tn, m // tm),
        in_specs=in_specs,
        out_specs=pl.BlockSpec((tm, tn), lambda j, i: (i, j)),
        out_shape=jax.ShapeDtypeStruct((m, n), out_dtype),
        compiler_params=_cp(("arbitrary", "arbitrary")),
        name="proj",
    )(*args)


def _out_proj_kernel(*refs, n_a):
    a_refs = refs[:n_a]
    w_refs = refs[n_a:2 * n_a]
    res_ref, gm_ref, o_ref = refs[2 * n_a:]
    acc = jnp.dot(a_refs[0][...], w_refs[0][...], preferred_element_type=F32)
    for a_ref, w_ref in zip(a_refs[1:], w_refs[1:]):
        acc = acc + jnp.dot(a_ref[...], w_ref[...], preferred_element_type=F32)
    o_ref[...] = res_ref[...] + gm_ref[0] * acc


def out_proj(a_list, w_list, res, mods, which_gate, n_rows, tm=512, tn=1024):
    n_a = len(a_list)
    n = w_list[0].shape[1]
    in_specs = [pl.BlockSpec((tm, a.shape[1]), lambda j, i: (i, 0)) for a in a_list]
    in_specs += [pl.BlockSpec((w.shape[0], tn), lambda j, i: (0, j)) for w in w_list]
    in_specs += [
        pl.BlockSpec((tm, tn), lambda j, i: (i, j)),
        pl.BlockSpec((1, 1, tn), lambda j, i: (_seg_of_tile(i, tm) * 6 + which_gate, 0, j)),
    ]
    return pl.pallas_call(
        functools.partial(_out_proj_kernel, n_a=n_a),
        grid=(n // tn, n_rows // tm),
        in_specs=in_specs,
        out_specs=pl.BlockSpec((tm, tn), lambda j, i: (i, j)),
        out_shape=jax.ShapeDtypeStruct((n_rows, n), F32),
        compiler_params=_cp(("arbitrary", "arbitrary")),
        name="out_proj",
    )(*a_list, *w_list, res, mods)


def na_bias_table(rpb):
    rows = SEQ // GRID_W
    n_dr, n_dc = 2 * NA_WIN_H - 1, 2 * NA_WIN_W - 1
    cols = np.arange(GRID_W)
    col_start = np.clip(cols - NA_WIN_W // 2, 0, GRID_W - NA_WIN_W)
    col_ok = (cols[None, :] >= col_start[:, None]) & (cols[None, :] < col_start[:, None] + NA_WIN_W)
    col_idx = np.clip(cols[None, :] - cols[:, None], 1 - NA_WIN_W, NA_WIN_W - 1) + (NA_WIN_W - 1)
    col_pick = (col_idx[None] == np.arange(n_dc)[:, None, None]).astype(np.float32)
    row_pick = np.zeros((3, NA_QROWS, NA_KROWS, n_dr), np.float32)
    ok = np.zeros((3, NA_QROWS, GRID_W, NA_KROWS, GRID_W), bool)
    for v, r0 in enumerate((0, 2 * NA_QROWS, rows - NA_QROWS)):
        kstart = int(np.clip(r0 - NA_WIN_H // 2, 0, rows - NA_KROWS))
        r = r0 + np.arange(NA_QROWS)
        kr = kstart + np.arange(NA_KROWS)
        win = np.clip(r - NA_WIN_H // 2, 0, rows - NA_WIN_H)
        row_ok = (kr[None, :] >= win[:, None]) & (kr[None, :] < win[:, None] + NA_WIN_H)
        row_idx = kr[None, :] - r[:, None] + (NA_WIN_H - 1)
        row_pick[v] = (row_idx[:, :, None] == np.arange(n_dr)) & row_ok[:, :, None]
        ok[v] = row_ok[:, None, :, None] & col_ok[None, :, None, :]
    hi = lax.Precision.HIGHEST
    by_col = jnp.einsum("hdk,kqc->hdqc", rpb.astype(F32), col_pick, precision=hi)
    tab = jnp.einsum("vijd,hdqc->hviqjc", row_pick, by_col, precision=hi)
    tab = jnp.where(ok[None], tab, NEG_INF)
    return tab.reshape(NA_HEADS, 3, NA_QROWS * GRID_W, NA_KROWS * GRID_W)


def _softmax_pv(pieces):
    m = functools.reduce(jnp.maximum, [jnp.max(s, axis=-1, keepdims=True) for s, _ in pieces])
    ps = [jnp.exp(s - m) for s, _ in pieces]
    l = functools.reduce(lambda a, b: a + b, [jnp.sum(p, axis=-1, keepdims=True) for p in ps])
    o = functools.reduce(lambda a, b: a + b,
                         [jnp.dot(p.astype(BF16), v, preferred_element_type=F32) for p, (_, v) in zip(ps, pieces)])
    return o / l


_NT_DIMS = (((1,), (1,)), ((), ()))


def _na_kernel(q_ref, k_ref, v_ref, kc_ref, vc_ref, bias_ref, o_ref):
    qb = pl.program_id(2)
    rows = SEQ // GRID_W
    kstart = pl.multiple_of(jnp.clip(qb * NA_QROWS - NA_WIN_H // 2, 0, rows - NA_KROWS) * GRID_W, GRID_W)
    q = q_ref[...]
    kw = k_ref[pl.ds(kstart, NA_KROWS * GRID_W), :]
    vw = v_ref[pl.ds(kstart, NA_KROWS * GRID_W), :]
    s_loc = lax.dot_general(q, kw, _NT_DIMS, preferred_element_type=F32) + bias_ref[0, 0]
    s_ctx = lax.dot_general(q, kc_ref[...], _NT_DIMS, preferred_element_type=F32)
    o_ref[...] = _softmax_pv([(s_loc, vw), (s_ctx, vc_ref[...])]).astype(o_ref.dtype)


def _ctx_attn_kernel(q_ref, k_ref, v_ref, o_ref):
    s = lax.dot_general(q_ref[...], k_ref[...], _NT_DIMS, preferred_element_type=F32)
    o_ref[...] = _softmax_pv([(s, v_ref[...])]).astype(o_ref.dtype)


def neighbourhood_attention(qkv, bias):
    nqb = SEQ // (NA_QROWS * GRID_W)
    tq = NA_QROWS * GRID_W
    ctx_blk = NX // CTX
    lat = pl.pallas_call(
        _na_kernel,
        grid=(BATCH, NA_HEADS, nqb),
        in_specs=[
            pl.BlockSpec((tq, HEAD_DIM), lambda b, h, i: (b * nqb + i, h)),
            pl.BlockSpec((SEQ, HEAD_DIM), lambda b, h, i: (b, NA_HEADS + h)),
            pl.BlockSpec((SEQ, HEAD_DIM), lambda b, h, i: (b, 2 * NA_HEADS + h)),
            pl.BlockSpec((CTX, HEAD_DIM), lambda b, h, i: (ctx_blk + b, NA_HEADS + h)),
            pl.BlockSpec((CTX, HEAD_DIM), lambda b, h, i: (ctx_blk + b, 2 * NA_HEADS + h)),
            pl.BlockSpec((1, 1, tq, NA_KROWS * GRID_W),
                         lambda b, h, i: (h, jnp.where(i == 0, 0, jnp.where(i == nqb - 1, 2, 1)), 0, 0)),
        ],
        out_specs=pl.BlockSpec((tq, HEAD_DIM), lambda b, h, i: (b * nqb + i, h)),
        out_shape=jax.ShapeDtypeStruct((NX, NA_WIDTH), BF16),
        compiler_params=_cp(("parallel", "parallel", "arbitrary")),
        name="na_attn",
    )(qkv, qkv, qkv, qkv, qkv, bias)
    ctx = pl.pallas_call(
        _ctx_attn_kernel,
        grid=(BATCH, NA_HEADS),
        in_specs=[
            pl.BlockSpec((CTX, HEAD_DIM), lambda b, h: (ctx_blk + b, h)),
            pl.BlockSpec((CTX, HEAD_DIM), lambda b, h: (ctx_blk + b, NA_HEADS + h)),
            pl.BlockSpec((CTX, HEAD_DIM), lambda b, h: (ctx_blk + b, 2 * NA_HEADS + h)),
        ],
        out_specs=pl.BlockSpec((CTX, HEAD_DIM), lambda b, h: (b, h)),
        out_shape=jax.ShapeDtypeStruct((BATCH * CTX, NA_WIDTH), BF16),
        compiler_params=_cp(("parallel", "parallel")),
        name="na_ctx_attn",
    )(qkv, qkv, qkv)
    return jnp.concatenate([lat, ctx], axis=0)


_CONV_HALO = 8


def _conv_kernel(prev_ref, cur_ref, next_ref, w_ref, b_ref, cs_ref, o_ref, buf_ref):
    i = pl.program_id(0)
    tm = cur_ref.shape[0]
    tiles_per_seq = SEQ // tm
    n_lat = NX // tm
    first = (i % tiles_per_seq == 0) | (i >= n_lat)
    last = (i % tiles_per_seq == tiles_per_seq - 1) | (i >= n_lat)
    buf_ref[0:_CONV_HALO, :] = prev_ref[...] * jnp.where(first, 0.0, 1.0)
    buf_ref[_CONV_HALO:_CONV_HALO + tm, :] = cur_ref[...]
    buf_ref[_CONV_HALO + tm:, :] = next_ref[...] * jnp.where(last, 0.0, 1.0)
    acc = jnp.zeros(cur_ref.shape, F32) + b_ref[...]
    for j in range(ML_CONV):
        off = _CONV_HALO + j - ML_CONV // 2
        acc = acc + buf_ref[off:off + tm, :] * w_ref[j:j + 1, :]
    o_ref[...] = (_silu(acc) * cs_ref[...]).astype(o_ref.dtype)


def conv_silu(t, w, b, col_scale):
    tm = CTX
    c = t.shape[1]
    hb = tm // _CONV_HALO
    n_halo_blocks = NT // _CONV_HALO
    wp = jnp.zeros((8, c), F32).at[:ML_CONV].set(w)
    return pl.pallas_call(
        _conv_kernel,
        grid=(NT // tm,),
        in_specs=[
            pl.BlockSpec((_CONV_HALO, c), lambda i: (jnp.maximum(i * hb - 1, 0), 0)),
            pl.BlockSpec((tm, c), lambda i: (i, 0)),
            pl.BlockSpec((_CONV_HALO, c), lambda i: (jnp.minimum((i + 1) * hb, n_halo_blocks - 1), 0)),
            pl.BlockSpec((8, c), lambda i: (0, 0)),
            pl.BlockSpec((1, c), lambda i: (0, 0)),
            pl.BlockSpec((1, c), lambda i: (0, 0)),
        ],
        out_specs=pl.BlockSpec((tm, c), lambda i: (i, 0)),
        out_shape=jax.ShapeDtypeStruct((NT, c), BF16),
        scratch_shapes=[pltpu.VMEM((tm + 2 * _CONV_HALO, c), F32)],
        compiler_params=_cp(("parallel",)),
        name="conv_silu",
    )(t, t, t, wp, b.reshape(1, c), col_scale.reshape(1, c))


def _split3(x):
    hi = x.astype(BF16)
    r = x - hi.astype(F32)
    mid = r.astype(BF16)
    lo = (r - mid.astype(F32)).astype(BF16)
    return hi, mid, lo


def _log_sigmoid(x):
    return jnp.minimum(x, 0.0) - jnp.log1p(jnp.exp(-jnp.abs(x)))


def _mlstm_chunk(reverse, q, kt, v_ext, gc, gr, gb_col, gb_row, c_ref, m):
    ln = q.shape[0]
    d = 1 if reverse else 0
    i_col = gc[:, 2 * d:2 * d + 1] + gb_col[:, 2 * d:2 * d + 1]
    f_col = _log_sigmoid(gc[:, 2 * d + 1:2 * d + 2] + gb_col[:, 2 * d + 1:2 * d + 2])
    i_row = gr[2 * d:2 * d + 1, :] + gb_row[2 * d:2 * d + 1, :]
    f_row = _log_sigmoid(gr[2 * d + 1:2 * d + 2, :] + gb_row[2 * d + 1:2 * d + 2, :])
    t_idx = lax.broadcasted_iota(jnp.int32, (ln, ln), 0)
    s_idx = lax.broadcasted_iota(jnp.int32, (ln, ln), 1)
    causal = (s_idx >= t_idx) if reverse else (s_idx <= t_idx)
    tri = jnp.where(causal, 1.0, 0.0).astype(BF16)
    f_col_w = jnp.broadcast_to(f_col, (ln, HEAD_DIM))
    b_col = functools.reduce(lambda a, b: a + b,
                             [jnp.dot(tri, p, preferred_element_type=F32) for p in _split3(f_col_w)])[:, 0:1]
    f_row_w = jnp.broadcast_to(f_row, (16, ln))
    b_row = functools.reduce(lambda a, b: a + b,
                             [lax.dot_general(p, tri, _NT_DIMS, preferred_element_type=F32) for p in _split3(f_row_w)])[0:1, :]
    total = jnp.sum(f_col, axis=0, keepdims=True)

    dmat = jnp.where(causal, b_col - b_row + i_row, -jnp.inf)
    inter = b_col + m
    m_t = jnp.maximum(inter, jnp.max(dmat, axis=-1, keepdims=True))
    w_inter = jnp.exp(inter - m_t)
    s = jnp.dot(q, kt, preferred_element_type=F32) * jnp.exp(dmat - m_t)
    numden = (w_inter * jnp.dot(q, c_ref[...].astype(BF16), preferred_element_type=F32)
              + jnp.dot(s.astype(BF16), v_ext, preferred_element_type=F32))
    den = numden[:, HEAD_DIM:HEAD_DIM + 1]
    h = numden[:, :HEAD_DIM] / jnp.maximum(jnp.abs(den), jnp.exp(-m_t))

    g = total - b_col + i_col
    m_new = jnp.maximum(total + m, jnp.max(g, axis=0, keepdims=True))
    decay = jnp.exp(total + m - m_new)
    wg = jnp.exp(g - m_new)
    upd = jnp.dot(kt, (wg * v_ext.astype(F32)).astype(BF16), preferred_element_type=F32)
    c_ref[...] = decay * c_ref[...] + upd
    return h, m_new


def _mlstm_kernel(q_ref, kt_ref, v_ref, o_ref, qc_ref, ktc_ref, vc_ref, oc_ref, gc_ref, gr_ref, gbc_ref, gbr_ref,
                  hn_ref, out_ref, outc_ref, cf_ref, cb_ref, hf_ref, hb_ref):
    ln = ML_CHUNK
    n_chunks = SEQ // ln
    ones_col = jnp.where(lax.broadcasted_iota(jnp.int32, (ln, HEAD_DIM), 1) == 0, 1.0, 0.0).astype(BF16)
    gb_col = gbc_ref[0]
    gb_row = gbr_ref[0]

    def v_ext(v):
        return jnp.concatenate([v, ones_col], axis=1)

    cf_ref[...] = jnp.zeros(cf_ref.shape, F32)
    cb_ref[...] = jnp.zeros(cb_ref.shape, F32)
    m0 = jnp.zeros((1, 1), F32)

    gc = gc_ref[0, 0:ln, :]
    gr = gr_ref[0, :, 0:ln]
    vx = v_ext(vc_ref[...])
    hf, mf = _mlstm_chunk(False, qc_ref[...], ktc_ref[...], vx, gc, gr, gb_col, gb_row, cf_ref, m0)
    hb, mb = _mlstm_chunk(True, qc_ref[...], ktc_ref[...], vx, gc, gr, gb_col, gb_row, cb_ref, m0)

    def finish(h, o_gate, hn):
        ms = jnp.mean(h * h, axis=-1, keepdims=True)
        return (h * lax.rsqrt(ms + EPS) * hn * jax.nn.sigmoid(o_gate.astype(F32))).astype(BF16)

    hn = hn_ref[...]
    outc_ref[...] = finish(hf + hb, oc_ref[...], hn)

    def body(c, carry):
        mf, mb = carry
        for reverse, c_ref, h_ref in ((False, cf_ref, hf_ref), (True, cb_ref, hb_ref)):
            cc = (n_chunks - 1 - c) if reverse else c
            r0 = pl.multiple_of(cc * ln, ln)
            g0 = pl.multiple_of(cc * ln + CTX, ln)
            h, m_new = _mlstm_chunk(reverse, q_ref[pl.ds(r0, ln), :], kt_ref[:, pl.ds(r0, ln)],
                                    v_ext(v_ref[pl.ds(r0, ln), :]), gc_ref[0, pl.ds(g0, ln), :],
                                    gr_ref[0, :, pl.ds(g0, ln)], gb_col, gb_row, c_ref, mb if reverse else mf)
            h_ref[pl.ds(r0, ln), :] = h
            if reverse:
                mb = m_new
            else:
                mf = m_new
        return mf, mb

    lax.fori_loop(0, n_chunks, body, (mf, mb))

    def fin_body(c, _):
        r0 = pl.multiple_of(c * ln, ln)
        out_ref[pl.ds(r0, ln), :] = finish(hf_ref[pl.ds(r0, ln), :] + hb_ref[pl.ds(r0, ln), :],
                                           o_ref[pl.ds(r0, ln), :], hn)
        return 0

    lax.fori_loop(0, n_chunks, fin_body, 0)


def mlstm(qk, kt, vo, gates_col, gates_row, gate_b, hnorm):
    hh = ML_HEADS
    ctx_blk = NX // CTX
    tot = CTX + SEQ
    gb = gate_b.astype(F32).transpose(2, 0, 1).reshape(hh, 4)
    gb_col = jnp.zeros((hh, 1, HEAD_DIM), F32).at[:, 0, :4].set(gb)
    gb_row = jnp.zeros((hh, 8, 1), F32).at[:, :4, 0].set(gb)
    lat, ctx = pl.pallas_call(
        _mlstm_kernel,
        grid=(BATCH, hh),
        in_specs=[
            pl.BlockSpec((SEQ, HEAD_DIM), lambda b, h: (b, h)),
            pl.BlockSpec((HEAD_DIM, SEQ), lambda b, h: (h, b)),
            pl.BlockSpec((SEQ, HEAD_DIM), lambda b, h: (b, h)),
            pl.BlockSpec((SEQ, HEAD_DIM), lambda b, h: (b, hh + h)),
            pl.BlockSpec((CTX, HEAD_DIM), lambda b, h: (ctx_blk + b, h)),
            pl.BlockSpec((HEAD_DIM, CTX), lambda b, h: (h, ctx_blk + b)),
            pl.BlockSpec((CTX, HEAD_DIM), lambda b, h: (ctx_blk + b, h)),
            pl.BlockSpec((CTX, HEAD_DIM), lambda b, h: (ctx_blk + b, hh + h)),
            pl.BlockSpec((1, tot, HEAD_DIM), lambda b, h: (b * hh + h, 0, 0)),
            pl.BlockSpec((1, 8, tot), lambda b, h: (b * hh + h, 0, 0)),
            pl.BlockSpec((1, 1, HEAD_DIM), lambda b, h: (h, 0, 0)),
            pl.BlockSpec((1, 8, 1), lambda b, h: (h, 0, 0)),
            pl.BlockSpec((1, HEAD_DIM), lambda b, h: (0, h)),
        ],
        out_specs=[pl.BlockSpec((SEQ, HEAD_DIM), lambda b, h: (b, h)), pl.BlockSpec((CTX, HEAD_DIM), lambda b, h: (b, h))],
        out_shape=[jax.ShapeDtypeStruct((NX, ML_WIDTH), BF16), jax.ShapeDtypeStruct((BATCH * CTX, ML_WIDTH), BF16)],
        scratch_shapes=[
            pltpu.VMEM((HEAD_DIM, 2 * HEAD_DIM), F32),
            pltpu.VMEM((HEAD_DIM, 2 * HEAD_DIM), F32),
            pltpu.VMEM((SEQ, HEAD_DIM), F32),
            pltpu.VMEM((SEQ, HEAD_DIM), F32),
        ],
        compiler_params=_cp(("parallel", "parallel")),
        name="mlstm",
    )(qk, kt, vo, vo, qk, kt, vo, vo, gates_col, gates_row, gb_col, gb_row, hnorm.reshape(1, ML_WIDTH))
    return jnp.concatenate([lat, ctx], axis=0)


def _diff_attn_kernel(q0_ref, q1_ref, k0_ref, k1_ref, k0c_ref, k1c_ref, v_ref, vc_ref, lam_ref, sub_ref, o_ref,
                      *, lambda_init):
    lam = lam_ref[...]
    lam_full = (jnp.exp(jnp.sum(lam[0:1] * lam[1:2], axis=-1, keepdims=True))
                - jnp.exp(jnp.sum(lam[2:3] * lam[3:4], axis=-1, keepdims=True)) + lambda_init)

    tq = q0_ref.shape[0]
    qs = (q0_ref[...], q1_ref[...])

    def step(carry, ks, v):
        out = []
        for (m, l, acc), q, k in zip(carry, qs, ks):
            s = lax.dot_general(q, k, _NT_DIMS, preferred_element_type=F32)
            m_new = jnp.maximum(m, jnp.max(s, axis=-1, keepdims=True))
            alpha = jnp.exp2(m - m_new)
            p = jnp.exp2(s - m_new)
            l = alpha * l + jnp.sum(p, axis=-1, keepdims=True)
            acc = alpha * acc + jnp.dot(p.astype(BF16), v, preferred_element_type=F32)
            out.append((m_new, l, acc))
        return tuple(out)

    init = tuple((jnp.full((tq, 1), -jnp.inf, F32), jnp.zeros((tq, 1), F32), jnp.zeros((tq, DA_VDIM), F32))
                 for _ in range(2))

    carry = init
    for c in range(SEQ // DA_KCHUNK):
        rows = slice(c * DA_KCHUNK, (c + 1) * DA_KCHUNK)
        carry = step(carry, (k0_ref[rows, :], k1_ref[rows, :]), v_ref[rows, :])
    (_, l0, acc0), (_, l1, acc1) = step(carry, (k0c_ref[...], k1c_ref[...]), vc_ref[...])
    o = acc0 / l0 - lam_full * (acc1 / l1)
    ms = jnp.mean(o * o, axis=-1, keepdims=True)
    o_ref[...] = (o * lax.rsqrt(ms + EPS) * sub_ref[...] * (1.0 - lambda_init)).astype(o_ref.dtype)


def diff_attention(qkv, lam, subln, lambda_init, tq=256):
    nq = SEQ // tq
    ctx_blk = NX // CTX
    kcol = 2 * DA_HEADS
    vcol = 2 * DA_HEADS
    return pl.pallas_call(
        functools.partial(_diff_attn_kernel, lambda_init=lambda_init),
        grid=(BATCH, DA_HEADS, nq),
        in_specs=[
            pl.BlockSpec((tq, HEAD_DIM), lambda b, h, i: (b * nq + i, 2 * h)),
            pl.BlockSpec((tq, HEAD_DIM), lambda b, h, i: (b * nq + i, 2 * h + 1)),
            pl.BlockSpec((SEQ, HEAD_DIM), lambda b, h, i: (b, kcol + 2 * h)),
            pl.BlockSpec((SEQ, HEAD_DIM), lambda b, h, i: (b, kcol + 2 * h + 1)),
            pl.BlockSpec((CTX, HEAD_DIM), lambda b, h, i: (ctx_blk + b, kcol + 2 * h)),
            pl.BlockSpec((CTX, HEAD_DIM), lambda b, h, i: (ctx_blk + b, kcol + 2 * h + 1)),
            pl.BlockSpec((SEQ, DA_VDIM), lambda b, h, i: (b, vcol + h)),
            pl.BlockSpec((CTX, DA_VDIM), lambda b, h, i: (ctx_blk + b, vcol + h)),
            pl.BlockSpec((4, HEAD_DIM), lambda b, h, i: (0, 0)),
            pl.BlockSpec((1, DA_VDIM), lambda b, h, i: (0, 0)),
        ],
        out_specs=pl.BlockSpec((tq, DA_VDIM), lambda b, h, i: (b * nq + i, h)),
        out_shape=jax.ShapeDtypeStruct((NX, DA_HEADS * DA_VDIM), BF16),
        compiler_params=_cp(("parallel", "parallel", "arbitrary"), vmem_mb=56),
        name="diff_attn",
    )(qkv, qkv, qkv, qkv, qkv, qkv, qkv, qkv, lam.astype(F32), subln.reshape(1, DA_VDIM).astype(F32))


def dispatch_tables(gates_t, n_rows):
    nt = n_rows // TOK_TILE
    max_units = (n_rows * TOP_K + nt * N_EXPERTS * (UNIT - 1)) // UNIT
    n_ffn_tiles = (max_units + N_EXPERTS * (FFN_UNITS - 1)) // FFN_UNITS + 1
    gates = gates_t.T.reshape(nt, TOK_TILE, N_EXPERTS)
    sel = gates > 0
    self32 = sel.astype(F32)
    hi = lax.Precision.HIGHEST
    before_t = np.tril(np.ones((TOK_TILE, TOK_TILE), np.float32), -1)
    before_e = np.triu(np.ones((N_EXPERTS, N_EXPERTS), np.float32), 1)
    rank = jnp.einsum("ts,nse->nte", before_t, self32, precision=hi).astype(jnp.int32)
    kr = jnp.einsum("ntf,fe->nte", self32, before_e, precision=hi).astype(jnp.int32)
    cnt = self32.sum(axis=1).astype(jnp.int32)
    nun = (cnt + UNIT - 1) // UNIT
    loc_off = jnp.cumsum(nun, axis=1) - nun
    slot = loc_off[:, None, :] * UNIT + rank
    hit = sel[:, :, None, :] & (kr[:, :, None, :] == jnp.arange(TOP_K)[None, None, :, None])
    slot6 = jnp.where(hit.any(axis=-1), jnp.sum(jnp.where(hit, slot[:, :, None, :], 0), axis=-1), -1)
    w6 = jnp.sum(jnp.where(hit, gates[:, :, None, :], 0.0), axis=-1)
    slot6 = jnp.pad(slot6, ((0, 0), (0, 0), (0, 8 - TOP_K)), constant_values=-1)
    w6 = jnp.pad(w6, ((0, 0), (0, 0), (0, 8 - TOP_K)))

    seg_un = nun.sum(axis=0)
    seg_pad = (seg_un + FFN_UNITS - 1) // FFN_UNITS * FFN_UNITS
    seg_end = jnp.cumsum(seg_pad)
    seg_start = seg_end - seg_pad
    gstart = seg_start[None, :] + jnp.cumsum(nun, axis=0) - nun
    u = jnp.arange(UNITS_PER_TILE, dtype=jnp.int32)
    loc_end = loc_off + nun
    ue = (loc_end[:, None, :] <= u[None, :, None]).sum(axis=-1)
    onehot = ue[:, :, None] == jnp.arange(N_EXPERTS)[None, None, :]
    dst = jnp.sum(jnp.where(onehot, (gstart - loc_off)[:, None, :], 0), axis=-1) + u[None, :]
    n_units_total = n_ffn_tiles * FFN_UNITS
    flat_dst = jnp.where(ue < N_EXPERTS, dst, n_units_total).reshape(-1).astype(jnp.int32)
    src_write = jnp.full((n_units_total + 1,), -1, jnp.int32).at[flat_dst].set(
        jnp.arange(nt * UNITS_PER_TILE, dtype=jnp.int32))[:n_units_total]
    src_read = jnp.where(src_write >= 0, src_write, UNITS_PER_TILE - 1)
    chunk_start = jnp.concatenate([seg_start, seg_end[-1:]]).astype(jnp.int32) // FFN_UNITS
    return dict(slot6=slot6, w6=w6, slot6_t=slot6.transpose(0, 2, 1), src_read=src_read, src_write=src_write,
                chunk_start=chunk_start, nt=nt)


def _moe_gather_kernel(h_ref, slot_ref, o_ref):
    s_iota = lax.broadcasted_iota(jnp.int32, (SLOTS, TOK_TILE), 0)
    slots = slot_ref[0]
    p = jnp.zeros((SLOTS, TOK_TILE), F32)
    for k in range(TOP_K):
        p = jnp.where(s_iota == slots[k:k + 1, :], 1.0, p)
    p = p.astype(BF16)
    o_ref[0] = jnp.dot(p, h_ref[...], preferred_element_type=F32).astype(BF16)


def moe_gather(h, slot6_t, nt):
    return pl.pallas_call(
        _moe_gather_kernel,
        grid=(nt,),
        in_specs=[pl.BlockSpec((TOK_TILE, D), lambda i: (i, 0)), pl.BlockSpec((1, 8, TOK_TILE), lambda i: (i, 0, 0))],
        out_specs=pl.BlockSpec((1, SLOTS, D), lambda i: (i, 0, 0)),
        out_shape=jax.ShapeDtypeStruct((nt, SLOTS, D), BF16),
        compiler_params=_cp(("parallel",)),
        name="moe_gather",
    )(h, slot6_t)


def _unit_copy(src_hbm, buf_ref, sem_ref, slot, src_unit, j):
    return pltpu.make_async_copy(src_hbm.at[pl.ds(pl.multiple_of(src_unit * UNIT, UNIT), UNIT)],
                                 buf_ref.at[slot, pl.ds(j * UNIT, UNIT)], sem_ref.at[slot])


def _fetch_units(table_ref, base, n_units, src_hbm, buf_ref, sem_ref, slot):
    def body(j, _):
        _unit_copy(src_hbm, buf_ref, sem_ref, slot, table_ref[base + j], j).start()
        return 0

    lax.fori_loop(0, n_units, body, 0, unroll=8)


def _wait_units(n_units, src_hbm, buf_ref, sem_ref, slot):
    pltpu.make_async_copy(src_hbm.at[pl.ds(0, n_units * UNIT)], buf_ref.at[slot], sem_ref.at[slot]).wait()


def _moe_ffn_kernel(srcr_ref, srcw_ref, cs_ref, x_hbm, wg_ref, wu_ref, wd_ref, y_hbm, xbuf_ref, ybuf_ref,
                    sem_in, sem_out, wgb_ref, wub_ref, wdb_ref):
    e = pl.program_id(0)
    lo = cs_ref[e]
    hi = cs_ref[e + 1]
    total = cs_ref[N_EXPERTS]

    def out_units(c, slot, start):
        for j in range(FFN_UNITS):
            su = srcw_ref[c * FFN_UNITS + j]

            @pl.when(su >= 0)
            def _():
                cp = pltpu.make_async_copy(ybuf_ref.at[slot, pl.ds(j * UNIT, UNIT)],
                                           y_hbm.at[pl.ds(pl.multiple_of(su * UNIT, UNIT), UNIT)], sem_out.at[slot])
                if start:
                    cp.start()
                else:
                    cp.wait()

    @pl.when(hi > lo)
    def _():
        wgb_ref[...] = wg_ref[0].astype(BF16)
        wub_ref[...] = wu_ref[0].astype(BF16)
        wdb_ref[...] = wd_ref[0].astype(BF16)

    def fetch(c):
        @pl.when(c < total)
        def _():
            _fetch_units(srcr_ref, c * FFN_UNITS, FFN_UNITS, x_hbm, xbuf_ref, sem_in, c % FFN_IN_DEPTH)

    @pl.when(e == 0)
    def _():
        for c in range(FFN_IN_DEPTH - 1):
            fetch(c)

    def chunk(c, _):
        slot = c % 2
        in_slot = c % FFN_IN_DEPTH
        fetch(c + FFN_IN_DEPTH - 1)
        _wait_units(FFN_UNITS, x_hbm, xbuf_ref, sem_in, in_slot)

        @pl.when(c >= 2)
        def _():
            out_units(c - 2, slot, False)

        x = xbuf_ref[in_slot]
        g = jnp.dot(x, wgb_ref[...], preferred_element_type=F32)
        u = jnp.dot(x, wub_ref[...], preferred_element_type=F32)
        a = (_silu(g) * u).astype(BF16)
        ybuf_ref[slot] = jnp.dot(a, wdb_ref[...], preferred_element_type=F32).astype(BF16)
        out_units(c, slot, True)
        return 0

    lax.fori_loop(lo, hi, chunk, 0)

    @pl.when(e == N_EXPERTS - 1)
    def _():
        for back in (2, 1):
            c = total - back

            @pl.when(c >= 0)
            def _():
                out_units(c, c % 2, False)


def moe_ffn(x_tiles, tabs, wg, wu, wd):
    x_flat = x_tiles.reshape(-1, D)
    idx = lambda e, *_: (e, 0, 0)
    grid_spec = pltpu.PrefetchScalarGridSpec(
        num_scalar_prefetch=3,
        grid=(N_EXPERTS,),
        in_specs=[
            pl.BlockSpec(memory_space=pl.ANY),
            pl.BlockSpec((1, D, D_EXPERT), idx),
            pl.BlockSpec((1, D, D_EXPERT), idx),
            pl.BlockSpec((1, D_EXPERT, D), idx),
        ],
        out_specs=pl.BlockSpec(memory_space=pl.ANY),
        scratch_shapes=[pltpu.VMEM((FFN_IN_DEPTH, FFN_TM, D), BF16), pltpu.VMEM((2, FFN_TM, D), BF16),
                        pltpu.SemaphoreType.DMA((FFN_IN_DEPTH,)), pltpu.SemaphoreType.DMA((2,)),
                        pltpu.VMEM((D, D_EXPERT), BF16), pltpu.VMEM((D, D_EXPERT), BF16),
                        pltpu.VMEM((D_EXPERT, D), BF16)],
    )
    y = pl.pallas_call(
        _moe_ffn_kernel,
        grid_spec=grid_spec,
        out_shape=jax.ShapeDtypeStruct(x_flat.shape, BF16),
        input_output_aliases={3: 0},
        compiler_params=_cp(("arbitrary",), vmem_mb=56),
        name="moe_ffn",
    )(tabs["src_read"], tabs["src_write"], tabs["chunk_start"], x_flat, wg, wu, wd)
    return y.reshape(x_tiles.shape)


def _shared_ffn_kernel(a_ref, wg_ref, wu_ref, wd_ref, o_ref):
    a = a_ref[...]
    g = jnp.dot(a, wg_ref[...], preferred_element_type=F32)
    u = jnp.dot(a, wu_ref[...], preferred_element_type=F32)
    o_ref[...] = jnp.dot((_silu(g) * u).astype(BF16), wd_ref[...], preferred_element_type=F32).astype(o_ref.dtype)


def shared_ffn(h, wg, wu, wd, n_rows, tm=512):
    return pl.pallas_call(
        _shared_ffn_kernel,
        grid=(n_rows // tm,),
        in_specs=[
            pl.BlockSpec((tm, D), lambda i: (i, 0)),
            pl.BlockSpec((D, D_EXPERT), lambda i: (0, 0)),
            pl.BlockSpec((D, D_EXPERT), lambda i: (0, 0)),
            pl.BlockSpec((D_EXPERT, D), lambda i: (0, 0)),
        ],
        out_specs=pl.BlockSpec((tm, D), lambda i: (i, 0)),
        out_shape=jax.ShapeDtypeStruct((n_rows, D), BF16),
        compiler_params=_cp(("parallel",)),
        name="shared_ffn",
    )(h, wg, wu, wd)


def _moe_combine_kernel(y_ref, slot_ref, w_ref, sh_ref, res_ref, gm_ref, o_ref):
    lane = lax.broadcasted_iota(jnp.int32, (TOK_TILE, SLOTS), 1)
    slots = slot_ref[0]
    w = w_ref[0]
    pw = jnp.zeros((TOK_TILE, SLOTS), F32)
    for k in range(TOP_K):
        pw = jnp.where(lane == slots[:, k:k + 1], w[:, k:k + 1], pw)
    routed = jnp.dot(pw.astype(BF16), y_ref[0], preferred_element_type=F32)
    o_ref[...] = res_ref[...] + gm_ref[0] * (routed + sh_ref[...].astype(F32))


def moe_combine(y_tiles, tabs, shared, res, mods, n_rows):
    nt = tabs["nt"]
    tm = TOK_TILE
    return pl.pallas_call(
        _moe_combine_kernel,
        grid=(nt,),
        in_specs=[
            pl.BlockSpec((1, SLOTS, D), lambda i: (i, 0, 0)),
            pl.BlockSpec((1, tm, 8), lambda i: (i, 0, 0)),
            pl.BlockSpec((1, tm, 8), lambda i: (i, 0, 0)),
            pl.BlockSpec((tm, D), lambda i: (i, 0)),
            pl.BlockSpec((tm, D), lambda i: (i, 0)),
            pl.BlockSpec((1, 1, D), lambda i: (_seg_of_tile(i, tm) * 6 + 5, 0, 0)),
        ],
        out_specs=pl.BlockSpec((tm, D), lambda i: (i, 0)),
        out_shape=jax.ShapeDtypeStruct((n_rows, D), F32),
        compiler_params=_cp(("parallel",), vmem_mb=56),
        name="moe_combine",
    )(y_tiles, tabs["slot6"], tabs["w6"], shared, res, mods)


def moe_block(xa, gain, mods, router_w, router_b, wg, wu, wd, sg, su, sd, n_rows):
    h, gates_t = norm_route(xa, gain, mods, router_w, router_b, n_rows)
    tabs = dispatch_tables(gates_t, n_rows)
    x_tiles = moe_gather(h, tabs["slot6_t"], tabs["nt"])
    y_tiles = moe_ffn(x_tiles, tabs, wg, wu, wd)
    shared = shared_ffn(h, sg.astype(BF16), su.astype(BF16), sd.astype(BF16), n_rows)
    return moe_combine(y_tiles, tabs, shared, xa, mods, n_rows)


def rope_tables():
    t = jnp.arange(SEQ)
    row = (t // GRID_W).astype(F32)
    col = (t % GRID_W).astype(F32)
    n_freq = HEAD_DIM // 4
    inv_freq = ROPE_THETA ** (-jnp.arange(n_freq, dtype=F32) / n_freq)
    ang = jnp.concatenate([row[:, None] * inv_freq, col[:, None] * inv_freq], axis=-1)
    ang = jnp.concatenate([ang, ang], axis=-1)
    sign = jnp.where(jnp.arange(HEAD_DIM) < HEAD_DIM // 2, -1.0, 1.0)
    cos = jnp.concatenate([jnp.cos(ang)] * BATCH + [jnp.ones((BATCH * CTX, HEAD_DIM), F32)], axis=0)
    sin = jnp.concatenate([jnp.sin(ang) * sign] * BATCH + [jnp.zeros((BATCH * CTX, HEAD_DIM), F32)], axis=0)
    return cos, sin


def even_layer(xa, mods, norm1, w_in, na_qnorm, na_knorm, na_rpb, conv_w, conv_b, gate_b, hnorm, w_out):
    h = norm_mod(xa, norm1, mods, 0, NT)
    w = w_in.astype(BF16)
    c0 = 3 * NA_WIDTH
    c1 = c0 + 2 * ML_WIDTH
    c2 = c1 + 2 * ML_WIDTH
    scale = HEAD_DIM ** -0.5
    gain = jnp.concatenate([jnp.tile(na_qnorm.astype(F32) * scale, NA_HEADS), jnp.tile(na_knorm.astype(F32), NA_HEADS),
                            jnp.ones((NA_WIDTH,), F32)]).reshape(1, c0)
    qkv = proj(h, w[:, :c0], BF16, gain=gain, n_norm_cols=2 * NA_WIDTH)
    na_h = neighbourhood_attention(qkv, na_bias_table(na_rpb))

    ml_qk = proj(h, w[:, c0:c1], F32)
    ml_vo = proj(h, w[:, c1:c2], BF16)
    n_gate = w_in.shape[1] - c2
    w_gate = jnp.zeros((D, HEAD_DIM), BF16).at[:, :n_gate].set(w[:, c2:])
    g = proj(h, w_gate, F32, tn=HEAD_DIM)[:, :n_gate]
    col_scale = jnp.concatenate([jnp.ones((ML_WIDTH,), F32), jnp.full((ML_WIDTH,), HEAD_DIM ** -0.5, F32)])
    qk = conv_silu(ml_qk, conv_w.astype(F32), conv_b.astype(F32), col_scale)
    kt = qk[:, ML_WIDTH:].T
    g = g.reshape(NT, 4, ML_HEADS)
    g = jnp.concatenate([g[NX:].reshape(BATCH, CTX, 4, ML_HEADS), g[:NX].reshape(BATCH, SEQ, 4, ML_HEADS)], axis=1)
    g = g.transpose(0, 3, 1, 2).reshape(BATCH * ML_HEADS, CTX + SEQ, 4)
    g_col = jnp.zeros((BATCH * ML_HEADS, CTX + SEQ, HEAD_DIM), F32).at[:, :, :4].set(g)
    g_row = jnp.zeros((BATCH * ML_HEADS, 8, CTX + SEQ), F32).at[:, :4, :].set(g.transpose(0, 2, 1))
    ml_h = mlstm(qk, kt, ml_vo, g_col, g_row, gate_b, hnorm.astype(F32))

    wo = w_out.astype(BF16)
    return out_proj([na_h, ml_h], [wo[:NA_WIDTH], wo[NA_WIDTH:]], xa, mods, 2, NT)


def odd_layer(xa, mods, norm1, w_in, qnorm, knorm, lam, subln, w_out, lambda_init):
    h = norm_mod(xa, norm1, mods, 0, NT)
    scale = HEAD_DIM ** -0.5 * math.log2(math.e)
    n_qk = 2 * DA_HEADS * HEAD_DIM
    gain = jnp.concatenate([jnp.tile(qnorm.astype(F32) * scale, 2 * DA_HEADS), jnp.tile(knorm.astype(F32), 2 * DA_HEADS),
                            jnp.ones((DA_HEADS * DA_VDIM,), F32)]).reshape(1, -1)
    cos, sin = rope_tables()
    qkv = proj(h, w_in.astype(BF16), BF16, gain=gain, n_norm_cols=2 * n_qk, cos=cos, sin=sin)
    o = diff_attention(qkv, lam, subln, lambda_init)
    return out_proj([o], [w_out.astype(BF16)], xa, mods, 2, NX)


def diff_lambda_init(layer):
    return 0.8 - 0.6 * math.exp(-0.3 * layer)


def kernel(x, c, ctx, c_ctx, l0_ada_w, l0_ada_b, l0_norm1, l0_norm2, l0_w_in, l0_na_qnorm, l0_na_knorm, l0_na_rpb, l0_ml_conv_w, l0_ml_conv_b, l0_ml_gate_b, l0_ml_hnorm, l0_w_out, l0_router_w, l0_router_b, l0_exp_gate, l0_exp_up, l0_exp_down, l0_sh_gate, l0_sh_up, l0_sh_down, l1_ada_w, l1_ada_b, l1_norm1, l1_norm2, l1_w_in, l1_qnorm, l1_knorm, l1_lambda, l1_subln, l1_w_out, l1_router_w, l1_router_b, l1_exp_gate, l1_exp_up, l1_exp_down, l1_sh_gate, l1_sh_up, l1_sh_down):
    assert x.shape == (BATCH, SEQ, D) and ctx.shape == (BATCH, CTX, D)
    xa = jnp.concatenate([x.reshape(NX, D), ctx.reshape(BATCH * CTX, D)], axis=0).astype(F32)
    cvec = jnp.zeros((8, D), F32).at[:BATCH].set(c).at[BATCH].set(c_ctx)

    mods0 = adaln(cvec, l0_ada_w, l0_ada_b)
    xa = even_layer(xa, mods0, l0_norm1, l0_w_in, l0_na_qnorm, l0_na_knorm, l0_na_rpb, l0_ml_conv_w, l0_ml_conv_b,
                    l0_ml_gate_b, l0_ml_hnorm, l0_w_out)
    xa = moe_block(xa, l0_norm2, mods0, l0_router_w, l0_router_b, l0_exp_gate, l0_exp_up, l0_exp_down,
                   l0_sh_gate, l0_sh_up, l0_sh_down, NT)

    mods1 = adaln(cvec, l1_ada_w, l1_ada_b)
    xl = odd_layer(xa, mods1, l1_norm1, l1_w_in, l1_qnorm, l1_knorm, l1_lambda, l1_subln, l1_w_out, diff_lambda_init(1))
    xl = moe_block(xl, l1_norm2, mods1, l1_router_w, l1_router_b, l1_exp_gate, l1_exp_up, l1_exp_down,
                   l1_sh_gate, l1_sh_up, l1_sh_down, NX)
    return xl.reshape(BATCH, SEQ, D)
```

```python
import functools
import math

import jax
import jax.numpy as jnp
import numpy as np
from jax import lax
from jax.experimental import pallas as pl
from jax.experimental.pallas import tpu as pltpu

F32 = jnp.float32
BF16 = jnp.bfloat16

D = 2048
BATCH = 2
SEQ = 4096
CTX = 256
NX = BATCH * SEQ
NT = NX + BATCH * CTX
GRID_W = 64
EPS = 1e-6
NEG_INF = -1e30

NA_HEADS = 8
HEAD_DIM = 128
NA_WIDTH = NA_HEADS * HEAD_DIM
NA_WIN_H = 8
NA_WIN_W = 16
NA_QROWS = 4
NA_KROWS = 12
ML_HEADS = 8
ML_WIDTH = ML_HEADS * HEAD_DIM
ML_CONV = 5
ML_CHUNK = 256
ML_HEADS_PER_STEP = 2

DA_HEADS = 8
DA_VDIM = 256
DA_KCHUNK = 1024
ROPE_THETA = 10000.0

N_EXPERTS = 64
N_GROUPS = 8
TOPK_GROUPS = 4
TOP_K = 6
D_EXPERT = 512
ROUTED_SCALE = 2.5

TOK_TILE = 256
UNIT = 16
UNITS_PER_TILE = (TOK_TILE * TOP_K + N_EXPERTS * (UNIT - 1)) // UNIT + 1
UNITS_PER_TILE = -(-UNITS_PER_TILE // 32) * 32
SLOTS = UNITS_PER_TILE * UNIT
FFN_TM = 256
FFN_UNITS = FFN_TM // UNIT
FFN_IN_DEPTH = 4

V7X_VMEM_BYTES = 64 * 1024 * 1024


def _cp(sem, vmem_mb=48):
    assert vmem_mb * 1024 * 1024 < V7X_VMEM_BYTES
    return pltpu.CompilerParams(dimension_semantics=sem, vmem_limit_bytes=vmem_mb * 1024 * 1024)


def _silu(x):
    return x * jax.nn.sigmoid(x)


def _seg_of_tile(i, tm):
    return (i * tm) // SEQ


def _ada_kernel(c_ref, w_ref, b_ref, o_ref):
    s = _silu(c_ref[...]).astype(BF16)
    o_ref[...] = jnp.dot(s, w_ref[...].astype(BF16), preferred_element_type=F32) + b_ref[...]


def adaln(cvec, w, b):
    n = w.shape[1]
    tn = 1024
    out = pl.pallas_call(
        _ada_kernel,
        grid=(n // tn,),
        in_specs=[
            pl.BlockSpec((8, D), lambda j: (0, 0)),
            pl.BlockSpec((D, tn), lambda j: (0, j)),
            pl.BlockSpec((1, tn), lambda j: (0, j)),
        ],
        out_specs=pl.BlockSpec((8, tn), lambda j: (0, j)),
        out_shape=jax.ShapeDtypeStruct((8, n), F32),
        compiler_params=_cp(("arbitrary",)),
        name="adaln",
    )(cvec, w, b.reshape(1, n))
    return out[:3].reshape(18, 1, D)


def _normed(x_ref, g_ref, sh_ref, sc_ref):
    x = x_ref[...]
    ms = jnp.mean(x * x, axis=-1, keepdims=True)
    y = x * lax.rsqrt(ms + EPS) * g_ref[...]
    return y * (1.0 + sc_ref[0]) + sh_ref[0]


def _norm_mod_kernel(x_ref, g_ref, sh_ref, sc_ref, o_ref):
    o_ref[...] = _normed(x_ref, g_ref, sh_ref, sc_ref).astype(o_ref.dtype)


def _mod_spec(which, tm):
    return pl.BlockSpec((1, 1, D), lambda i: (_seg_of_tile(i, tm) * 6 + which, 0, 0))


def norm_mod(x, gain, mods, which_shift, n_rows, tm=256):
    return pl.pallas_call(
        _norm_mod_kernel,
        grid=(n_rows // tm,),
        in_specs=[
            pl.BlockSpec((tm, D), lambda i: (i, 0)),
            pl.BlockSpec((1, D), lambda i: (0, 0)),
            _mod_spec(which_shift, tm),
            _mod_spec(which_shift + 1, tm),
        ],
        out_specs=pl.BlockSpec((tm, D), lambda i: (i, 0)),
        out_shape=jax.ShapeDtypeStruct((n_rows, D), BF16),
        compiler_params=_cp(("parallel",)),
        name="norm_mod",
    )(x, gain.reshape(1, D), mods, mods)


def _route(logits, bias_col):
    tm = logits.shape[1]
    per_group = N_EXPERTS // N_GROUPS
    scores = jax.nn.sigmoid(logits)
    sel = scores + bias_col
    row8 = lax.broadcasted_iota(jnp.int32, (per_group, tm), 0)
    grp = jnp.zeros((N_GROUPS, tm), F32)
    for g in range(N_GROUPS):
        slab = sel[g * per_group:(g + 1) * per_group, :]
        m1 = jnp.max(slab, axis=0, keepdims=True)
        first = jnp.min(jnp.where(slab == m1, row8, per_group), axis=0, keepdims=True)
        m2 = jnp.max(jnp.where(row8 == first, -jnp.inf, slab), axis=0, keepdims=True)
        grp = jnp.where(row8 == g, m1 + m2, grp)
    rank = jnp.zeros((N_GROUPS, tm), jnp.int32)
    for g in range(N_GROUPS):
        vg = grp[g:g + 1, :]
        beats = (vg > grp) | ((vg == grp) & (g < row8))
        rank = rank + jnp.where(beats, 1, 0)
    keep = jnp.where(rank < TOPK_GROUPS, 1.0, 0.0)
    cur = jnp.concatenate(
        [jnp.where(keep[g:g + 1, :] > 0.5, sel[g * per_group:(g + 1) * per_group, :], NEG_INF) for g in range(N_GROUPS)],
        axis=0)
    e_iota = lax.broadcasted_iota(jnp.int32, (N_EXPERTS, tm), 0)
    picked = jnp.zeros((N_EXPERTS, tm), F32)
    for _ in range(TOP_K):
        m = jnp.max(cur, axis=0, keepdims=True)
        idx = jnp.min(jnp.where(cur == m, e_iota, N_EXPERTS), axis=0, keepdims=True)
        hit = e_iota == idx
        picked = jnp.where(hit, 1.0, picked)
        cur = jnp.where(hit, -jnp.inf, cur)
    w = scores * picked
    return w / jnp.sum(w, axis=0, keepdims=True) * ROUTED_SCALE


def _norm_route_kernel(x_ref, g_ref, sh_ref, sc_ref, rwh_ref, rwl_ref, rb_ref, o_ref, gates_ref):
    h = _normed(x_ref, g_ref, sh_ref, sc_ref)
    h_hi = h.astype(BF16)
    o_ref[...] = h_hi
    h_lo = (h - h_hi.astype(F32)).astype(BF16)
    nt = (((1,), (1,)), ((), ()))
    logits = (lax.dot_general(rwh_ref[...], h_hi, nt, preferred_element_type=F32)
              + lax.dot_general(rwh_ref[...], h_lo, nt, preferred_element_type=F32)
              + lax.dot_general(rwl_ref[...], h_hi, nt, preferred_element_type=F32))
    gates_ref[...] = _route(logits, rb_ref[...])


def norm_route(x, gain, mods, router_w, router_b, n_rows):
    tm = TOK_TILE
    rwt = router_w.T
    rw_hi = rwt.astype(BF16)
    rw_lo = (rwt - rw_hi.astype(F32)).astype(BF16)
    return pl.pallas_call(
        _norm_route_kernel,
        grid=(n_rows // tm,),
        in_specs=[
            pl.BlockSpec((tm, D), lambda i: (i, 0)),
            pl.BlockSpec((1, D), lambda i: (0, 0)),
            _mod_spec(3, tm),
            _mod_spec(4, tm),
            pl.BlockSpec((N_EXPERTS, D), lambda i: (0, 0)),
            pl.BlockSpec((N_EXPERTS, D), lambda i: (0, 0)),
            pl.BlockSpec((N_EXPERTS, 1), lambda i: (0, 0)),
        ],
        out_specs=[pl.BlockSpec((tm, D), lambda i: (i, 0)), pl.BlockSpec((N_EXPERTS, tm), lambda i: (0, i))],
        out_shape=[jax.ShapeDtypeStruct((n_rows, D), BF16), jax.ShapeDtypeStruct((N_EXPERTS, n_rows), F32)],
        compiler_params=_cp(("parallel",)),
        name="norm_route",
    )(x, gain.reshape(1, D), mods, mods, rw_hi, rw_lo, router_b.reshape(N_EXPERTS, 1))


def _head_norm(acc, gain, g):
    a = acc[:, g * HEAD_DIM:(g + 1) * HEAD_DIM]
    ms = jnp.mean(a * a, axis=-1, keepdims=True)
    return a * lax.rsqrt(ms + EPS) * gain[:, g * HEAD_DIM:(g + 1) * HEAD_DIM]


def _proj_kernel(*refs, n_norm_tiles, rope):
    if rope:
        a_ref, w_ref, gain_ref, cos_ref, sin_ref, o_ref = refs
    else:
        a_ref, w_ref, gain_ref, o_ref = refs
    j = pl.program_id(0)
    acc = jnp.dot(a_ref[...], w_ref[...], preferred_element_type=F32)
    tn = acc.shape[1]

    @pl.when(j < n_norm_tiles)
    def _():
        gain = gain_ref[...]
        for g in range(tn // HEAD_DIM):
            y = _head_norm(acc, gain, g)
            if rope:
                y = y * cos_ref[...] + pltpu.roll(y, HEAD_DIM // 2, axis=1) * sin_ref[...]
            o_ref[:, g * HEAD_DIM:(g + 1) * HEAD_DIM] = y.astype(o_ref.dtype)

    @pl.when(j >= n_norm_tiles)
    def _():
        o_ref[...] = acc.astype(o_ref.dtype)


def proj(a, w, out_dtype, *, gain=None, n_norm_cols=0, cos=None, sin=None, tm=512, tn=1024):
    m, k = a.shape
    n = w.shape[1]
    tn = min(tn, n)
    rope = cos is not None
    if gain is None:
        gain = jnp.ones((1, n), F32)
    in_specs = [
        pl.BlockSpec((tm, k), lambda j, i: (i, 0)),
        pl.BlockSpec((k, tn), lambda j, i: (0, j)),
        pl.BlockSpec((1, tn), lambda j, i: (0, j)),
    ]
    args = [a, w, gain]
    if rope:
        in_specs += [pl.BlockSpec((tm, HEAD_DIM), lambda j, i: (i, 0))] * 2
        args += [cos, sin]
    assert n_norm_cols % tn == 0
    return pl.pallas_call(
        functools.partial(_proj_kernel, n_norm_tiles=n_norm_cols // tn, rope=rope),
        grid=(n // tn, m // tm),
        in_specs=in_specs,
        out_specs=pl.BlockSpec((tm, tn), lambda j, i: (i, j)),
        out_shape=jax.ShapeDtypeStruct((m, n), out_dtype),
        compiler_params=_cp(("arbitrary", "arbitrary")),
        name="proj",
    )(*args)


def _out_proj_kernel(*refs, n_a):
    a_refs = refs[:n_a]
    w_refs = refs[n_a:2 * n_a]
    res_ref, gm_ref, o_ref = refs[2 * n_a:]
    acc = jnp.dot(a_refs[0][...], w_refs[0][...], preferred_element_type=F32)
    for a_ref, w_ref in zip(a_refs[1:], w_refs[1:]):
        acc = acc + jnp.dot(a_ref[...], w_ref[...], preferred_element_type=F32)
    o_ref[...] = res_ref[...] + gm_ref[0] * acc


def out_proj(a_list, w_list, res, mods, which_gate, n_rows, tm=512, tn=1024):
    n_a = len(a_list)
    n = w_list[0].shape[1]
    in_specs = [pl.BlockSpec((tm, a.shape[1]), lambda j, i: (i, 0)) for a in a_list]
    in_specs += [pl.BlockSpec((w.shape[0], tn), lambda j, i: (0, j)) for w in w_list]
    in_specs += [
        pl.BlockSpec((tm, tn), lambda j, i: (i, j)),
        pl.BlockSpec((1, 1, tn), lambda j, i: (_seg_of_tile(i, tm) * 6 + which_gate, 0, j)),
    ]
    return pl.pallas_call(
        functools.partial(_out_proj_kernel, n_a=n_a),
        grid=(n // tn, n_rows // tm),
        in_specs=in_specs,
        out_specs=pl.BlockSpec((tm, tn), lambda j, i: (i, j)),
        out_shape=jax.ShapeDtypeStruct((n_rows, n), F32),
        compiler_params=_cp(("arbitrary", "arbitrary")),
        name="out_proj",
    )(*a_list, *w_list, res, mods)


def na_bias_table(rpb):
    rows = SEQ // GRID_W
    n_dc = 2 * NA_WIN_W - 1
    cols = np.arange(GRID_W)
    col_start = np.clip(cols - NA_WIN_W // 2, 0, GRID_W - NA_WIN_W)
    col_ok = (cols[None, :] >= col_start[:, None]) & (cols[None, :] < col_start[:, None] + NA_WIN_W)
    col_idx = np.clip(cols[None, :] - cols[:, None], 1 - NA_WIN_W, NA_WIN_W - 1) + (NA_WIN_W - 1)
    col_pick = (col_idx[None] == np.arange(n_dc)[:, None, None]).astype(np.float32)
    by_col = jnp.einsum("hdk,kqc->hdqc", rpb.astype(F32), col_pick, precision=lax.Precision.HIGHEST)
    by_col = jnp.where(col_ok[None, None], by_col, NEG_INF)
    masked = jnp.full((NA_HEADS, GRID_W, GRID_W), NEG_INF, F32)
    tables = []
    for r0 in (0, 2 * NA_QROWS, rows - NA_QROWS):
        kstart = int(np.clip(r0 - NA_WIN_H // 2, 0, rows - NA_KROWS))
        q_rows = []
        for i in range(NA_QROWS):
            r = r0 + i
            win = int(np.clip(r - NA_WIN_H // 2, 0, rows - NA_WIN_H))
            blocks = []
            for j in range(NA_KROWS):
                kr = kstart + j
                in_window = win <= kr < win + NA_WIN_H
                blocks.append(by_col[:, kr - r + NA_WIN_H - 1] if in_window else masked)
            q_rows.append(jnp.concatenate(blocks, axis=-1))
        tables.append(jnp.concatenate(q_rows, axis=1))
    return jnp.stack(tables, axis=1)


def _softmax_pv(pieces):
    m = functools.reduce(jnp.maximum, [jnp.max(s, axis=-1, keepdims=True) for s, _ in pieces])
    ps = [jnp.exp(s - m) for s, _ in pieces]
    l = functools.reduce(lambda a, b: a + b, [jnp.sum(p, axis=-1, keepdims=True) for p in ps])
    o = functools.reduce(lambda a, b: a + b,
                         [jnp.dot(p.astype(BF16), v, preferred_element_type=F32) for p, (_, v) in zip(ps, pieces)])
    return o / l


_NT_DIMS = (((1,), (1,)), ((), ()))


def _na_kernel(q_ref, k_ref, v_ref, kc_ref, vc_ref, bias_ref, o_ref):
    qb = pl.program_id(2)
    rows = SEQ // GRID_W
    kstart = pl.multiple_of(jnp.clip(qb * NA_QROWS - NA_WIN_H // 2, 0, rows - NA_KROWS) * GRID_W, GRID_W)
    q = q_ref[...]
    kw = k_ref[pl.ds(kstart, NA_KROWS * GRID_W), :]
    vw = v_ref[pl.ds(kstart, NA_KROWS * GRID_W), :]
    s_loc = lax.dot_general(q, kw, _NT_DIMS, preferred_element_type=F32) + bias_ref[0, 0]
    s_ctx = lax.dot_general(q, kc_ref[...], _NT_DIMS, preferred_element_type=F32)
    o_ref[...] = _softmax_pv([(s_loc, vw), (s_ctx, vc_ref[...])]).astype(o_ref.dtype)


def _ctx_attn_kernel(q_ref, k_ref, v_ref, o_ref):
    s = lax.dot_general(q_ref[...], k_ref[...], _NT_DIMS, preferred_element_type=F32)
    o_ref[...] = _softmax_pv([(s, v_ref[...])]).astype(o_ref.dtype)


def neighbourhood_attention(qkv, bias):
    nqb = SEQ // (NA_QROWS * GRID_W)
    tq = NA_QROWS * GRID_W
    ctx_blk = NX // CTX
    lat = pl.pallas_call(
        _na_kernel,
        grid=(BATCH, NA_HEADS, nqb),
        in_specs=[
            pl.BlockSpec((tq, HEAD_DIM), lambda b, h, i: (b * nqb + i, h)),
            pl.BlockSpec((SEQ, HEAD_DIM), lambda b, h, i: (b, NA_HEADS + h)),
            pl.BlockSpec((SEQ, HEAD_DIM), lambda b, h, i: (b, 2 * NA_HEADS + h)),
            pl.BlockSpec((CTX, HEAD_DIM), lambda b, h, i: (ctx_blk + b, NA_HEADS + h)),
            pl.BlockSpec((CTX, HEAD_DIM), lambda b, h, i: (ctx_blk + b, 2 * NA_HEADS + h)),
            pl.BlockSpec((1, 1, tq, NA_KROWS * GRID_W),
                         lambda b, h, i: (h, jnp.where(i == 0, 0, jnp.where(i == nqb - 1, 2, 1)), 0, 0)),
        ],
        out_specs=pl.BlockSpec((tq, HEAD_DIM), lambda b, h, i: (b * nqb + i, h)),
        out_shape=jax.ShapeDtypeStruct((NX, NA_WIDTH), BF16),
        compiler_params=_cp(("parallel", "parallel", "arbitrary")),
        name="na_attn",
    )(qkv, qkv, qkv, qkv, qkv, bias)
    ctx = pl.pallas_call(
        _ctx_attn_kernel,
        grid=(BATCH, NA_HEADS),
        in_specs=[
            pl.BlockSpec((CTX, HEAD_DIM), lambda b, h: (ctx_blk + b, h)),
            pl.BlockSpec((CTX, HEAD_DIM), lambda b, h: (ctx_blk + b, NA_HEADS + h)),
            pl.BlockSpec((CTX, HEAD_DIM), lambda b, h: (ctx_blk + b, 2 * NA_HEADS + h)),
        ],
        out_specs=pl.BlockSpec((CTX, HEAD_DIM), lambda b, h: (b, h)),
        out_shape=jax.ShapeDtypeStruct((BATCH * CTX, NA_WIDTH), BF16),
        compiler_params=_cp(("parallel", "parallel")),
        name="na_ctx_attn",
    )(qkv, qkv, qkv)
    return jnp.concatenate([lat, ctx], axis=0)


_CONV_HALO = 8


def _conv_kernel(prev_ref, cur_ref, next_ref, w_ref, b_ref, cs_ref, o_ref, buf_ref):
    i = pl.program_id(0)
    tm = cur_ref.shape[0]
    tiles_per_seq = SEQ // tm
    n_lat = NX // tm
    first = (i % tiles_per_seq == 0) | (i >= n_lat)
    last = (i % tiles_per_seq == tiles_per_seq - 1) | (i >= n_lat)
    buf_ref[0:_CONV_HALO, :] = prev_ref[...] * jnp.where(first, 0.0, 1.0)
    buf_ref[_CONV_HALO:_CONV_HALO + tm, :] = cur_ref[...]
    buf_ref[_CONV_HALO + tm:, :] = next_ref[...] * jnp.where(last, 0.0, 1.0)
    acc = jnp.zeros(cur_ref.shape, F32) + b_ref[...]
    for j in range(ML_CONV):
        off = _CONV_HALO + j - ML_CONV // 2
        acc = acc + buf_ref[off:off + tm, :] * w_ref[j:j + 1, :]
    o_ref[...] = (_silu(acc) * cs_ref[...]).astype(o_ref.dtype)


def conv_silu(t, w, b, col_scale):
    tm = CTX
    c = t.shape[1]
    hb = tm // _CONV_HALO
    n_halo_blocks = NT // _CONV_HALO
    wp = jnp.zeros((8, c), F32).at[:ML_CONV].set(w)
    return pl.pallas_call(
        _conv_kernel,
        grid=(NT // tm,),
        in_specs=[
            pl.BlockSpec((_CONV_HALO, c), lambda i: (jnp.maximum(i * hb - 1, 0), 0)),
            pl.BlockSpec((tm, c), lambda i: (i, 0)),
            pl.BlockSpec((_CONV_HALO, c), lambda i: (jnp.minimum((i + 1) * hb, n_halo_blocks - 1), 0)),
            pl.BlockSpec((8, c), lambda i: (0, 0)),
            pl.BlockSpec((1, c), lambda i: (0, 0)),
            pl.BlockSpec((1, c), lambda i: (0, 0)),
        ],
        out_specs=pl.BlockSpec((tm, c), lambda i: (i, 0)),
        out_shape=jax.ShapeDtypeStruct((NT, c), BF16),
        scratch_shapes=[pltpu.VMEM((tm + 2 * _CONV_HALO, c), F32)],
        compiler_params=_cp(("parallel",)),
        name="conv_silu",
    )(t, t, t, wp, b.reshape(1, c), col_scale.reshape(1, c))


def _split3(x):
    hi = x.astype(BF16)
    r = x - hi.astype(F32)
    mid = r.astype(BF16)
    lo = (r - mid.astype(F32)).astype(BF16)
    return hi, mid, lo


def _log_sigmoid(x):
    return jnp.minimum(x, 0.0) - jnp.log1p(jnp.exp(-jnp.abs(x)))


def _mlstm_chunk(reverse, q, kt, v_ext, gc, gr, gb_col, gb_row, c_ref, m):
    ln = q.shape[0]
    d = 1 if reverse else 0
    i_col = gc[:, 2 * d:2 * d + 1] + gb_col[:, 2 * d:2 * d + 1]
    f_col = _log_sigmoid(gc[:, 2 * d + 1:2 * d + 2] + gb_col[:, 2 * d + 1:2 * d + 2])
    i_row = gr[2 * d:2 * d + 1, :] + gb_row[2 * d:2 * d + 1, :]
    f_row = _log_sigmoid(gr[2 * d + 1:2 * d + 2, :] + gb_row[2 * d + 1:2 * d + 2, :])
    t_idx = lax.broadcasted_iota(jnp.int32, (ln, ln), 0)
    s_idx = lax.broadcasted_iota(jnp.int32, (ln, ln), 1)
    causal = (s_idx >= t_idx) if reverse else (s_idx <= t_idx)
    tri = jnp.where(causal, 1.0, 0.0).astype(BF16)
    f_col_w = jnp.broadcast_to(f_col, (ln, HEAD_DIM))
    b_col = functools.reduce(lambda a, b: a + b,
                             [jnp.dot(tri, p, preferred_element_type=F32) for p in _split3(f_col_w)])[:, 0:1]
    f_row_w = jnp.broadcast_to(f_row, (16, ln))
    b_row = functools.reduce(lambda a, b: a + b,
                             [lax.dot_general(p, tri, _NT_DIMS, preferred_element_type=F32) for p in _split3(f_row_w)])[0:1, :]
    total = jnp.sum(f_col, axis=0, keepdims=True)

    dmat = jnp.where(causal, b_col - b_row + i_row, -jnp.inf)
    inter = b_col + m
    m_t = jnp.maximum(inter, jnp.max(dmat, axis=-1, keepdims=True))
    w_inter = jnp.exp(inter - m_t)
    s = jnp.dot(q, kt, preferred_element_type=F32) * jnp.exp(dmat - m_t)
    numden = (w_inter * jnp.dot(q, c_ref[...].astype(BF16), preferred_element_type=F32)
              + jnp.dot(s.astype(BF16), v_ext, preferred_element_type=F32))
    den = numden[:, HEAD_DIM:HEAD_DIM + 1]
    h = numden[:, :HEAD_DIM] / jnp.maximum(jnp.abs(den), jnp.exp(-m_t))

    g = total - b_col + i_col
    m_new = jnp.maximum(total + m, jnp.max(g, axis=0, keepdims=True))
    decay = jnp.exp(total + m - m_new)
    wg = jnp.exp(g - m_new)
    upd = jnp.dot(kt, (wg * v_ext.astype(F32)).astype(BF16), preferred_element_type=F32)
    c_ref[...] = decay * c_ref[...] + upd
    return h, m_new


def _mlstm_kernel(q_ref, kt_ref, v_ref, o_ref, qc_ref, ktc_ref, vc_ref, oc_ref, gc_ref, gr_ref, gbc_ref, gbr_ref,
                  hn_ref, out_ref, outc_ref, cf_ref, cb_ref, hf_ref, hb_ref):
    ln = ML_CHUNK
    n_chunks = SEQ // ln
    nh = ML_HEADS_PER_STEP
    ones_col = jnp.where(lax.broadcasted_iota(jnp.int32, (ln, HEAD_DIM), 1) == 0, 1.0, 0.0).astype(BF16)

    def v_ext(v):
        return jnp.concatenate([v, ones_col], axis=1)

    def cols(hd):
        return slice(hd * HEAD_DIM, (hd + 1) * HEAD_DIM)

    def finish(h, o_gate, hn):
        ms = jnp.mean(h * h, axis=-1, keepdims=True)
        return (h * lax.rsqrt(ms + EPS) * hn * jax.nn.sigmoid(o_gate.astype(F32))).astype(BF16)

    def finish_heads(h, o_gate):
        return jnp.concatenate([finish(h[:, cols(hd)], o_gate[:, cols(hd)], hn_ref[:, cols(hd)]) for hd in range(nh)],
                               axis=1)

    cf_ref[...] = jnp.zeros(cf_ref.shape, F32)
    cb_ref[...] = jnp.zeros(cb_ref.shape, F32)
    m0 = jnp.zeros((1, 1), F32)

    ms = []
    for hd in range(nh):
        gc = gc_ref[hd, 0:ln, :]
        gr = gr_ref[hd, :, 0:ln]
        vx = v_ext(vc_ref[:, cols(hd)])
        hf, mf = _mlstm_chunk(False, qc_ref[:, cols(hd)], ktc_ref[cols(hd), :], vx, gc, gr, gbc_ref[hd], gbr_ref[hd],
                              cf_ref.at[hd], m0)
        hb, mb = _mlstm_chunk(True, qc_ref[:, cols(hd)], ktc_ref[cols(hd), :], vx, gc, gr, gbc_ref[hd], gbr_ref[hd],
                              cb_ref.at[hd], m0)
        outc_ref[:, cols(hd)] = finish(hf + hb, oc_ref[:, cols(hd)], hn_ref[:, cols(hd)])
        ms += [mf, mb]

    def body(c, carry):
        carry = list(carry)
        for hd in range(nh):
            for reverse, c_ref, h_ref in ((False, cf_ref, hf_ref), (True, cb_ref, hb_ref)):
                cc = (n_chunks - 1 - c) if reverse else c
                r0 = pl.multiple_of(cc * ln, ln)
                g0 = pl.multiple_of(cc * ln + CTX, ln)
                k = 2 * hd + int(reverse)
                h, carry[k] = _mlstm_chunk(reverse, q_ref[pl.ds(r0, ln), cols(hd)], kt_ref[cols(hd), pl.ds(r0, ln)],
                                           v_ext(v_ref[pl.ds(r0, ln), cols(hd)]), gc_ref[hd, pl.ds(g0, ln), :],
                                           gr_ref[hd, :, pl.ds(g0, ln)], gbc_ref[hd], gbr_ref[hd], c_ref.at[hd],
                                           carry[k])
                h_ref[pl.ds(r0, ln), cols(hd)] = h
        return tuple(carry)

    lax.fori_loop(0, n_chunks, body, tuple(ms))

    def fin_body(c, _):
        r0 = pl.multiple_of(c * ln, ln)
        out_ref[pl.ds(r0, ln), :] = finish_heads(hf_ref[pl.ds(r0, ln), :] + hb_ref[pl.ds(r0, ln), :],
                                                 o_ref[pl.ds(r0, ln), :])
        return 0

    lax.fori_loop(0, n_chunks, fin_body, 0)


def mlstm(qk, kt, vo, gates_col, gates_row, gate_b, hnorm):
    hh = ML_HEADS
    nh = ML_HEADS_PER_STEP
    hs = hh // nh
    w = nh * HEAD_DIM
    ctx_blk = NX // CTX
    tot = CTX + SEQ
    gb = gate_b.astype(F32).transpose(2, 0, 1).reshape(hh, 4)
    gb_col = jnp.zeros((hh, 1, HEAD_DIM), F32).at[:, 0, :4].set(gb)
    gb_row = jnp.zeros((hh, 8, 1), F32).at[:, :4, 0].set(gb)
    lat, ctx = pl.pallas_call(
        _mlstm_kernel,
        grid=(BATCH, hs),
        in_specs=[
            pl.BlockSpec((SEQ, w), lambda b, h: (b, h)),
            pl.BlockSpec((w, SEQ), lambda b, h: (h, b)),
            pl.BlockSpec((SEQ, w), lambda b, h: (b, h)),
            pl.BlockSpec((SEQ, w), lambda b, h: (b, hs + h)),
            pl.BlockSpec((CTX, w), lambda b, h: (ctx_blk + b, h)),
            pl.BlockSpec((w, CTX), lambda b, h: (h, ctx_blk + b)),
            pl.BlockSpec((CTX, w), lambda b, h: (ctx_blk + b, h)),
            pl.BlockSpec((CTX, w), lambda b, h: (ctx_blk + b, hs + h)),
            pl.BlockSpec((nh, tot, HEAD_DIM), lambda b, h: (b * hs + h, 0, 0)),
            pl.BlockSpec((nh, 8, tot), lambda b, h: (b * hs + h, 0, 0)),
            pl.BlockSpec((nh, 1, HEAD_DIM), lambda b, h: (h, 0, 0)),
            pl.BlockSpec((nh, 8, 1), lambda b, h: (h, 0, 0)),
            pl.BlockSpec((1, w), lambda b, h: (0, h)),
        ],
        out_specs=[pl.BlockSpec((SEQ, w), lambda b, h: (b, h)), pl.BlockSpec((CTX, w), lambda b, h: (b, h))],
        out_shape=[jax.ShapeDtypeStruct((NX, ML_WIDTH), BF16), jax.ShapeDtypeStruct((BATCH * CTX, ML_WIDTH), BF16)],
        scratch_shapes=[
            pltpu.VMEM((nh, HEAD_DIM, 2 * HEAD_DIM), F32),
            pltpu.VMEM((nh, HEAD_DIM, 2 * HEAD_DIM), F32),
            pltpu.VMEM((SEQ, w), F32),
            pltpu.VMEM((SEQ, w), F32),
        ],
        compiler_params=_cp(("parallel", "parallel")),
        name="mlstm",
    )(qk, kt, vo, vo, qk, kt, vo, vo, gates_col, gates_row, gb_col, gb_row, hnorm.reshape(1, ML_WIDTH))
    return jnp.concatenate([lat, ctx], axis=0)


def _diff_attn_kernel(q0_ref, q1_ref, k0_ref, k1_ref, k0c_ref, k1c_ref, v_ref, vc_ref, lam_ref, sub_ref, o_ref,
                      *, lambda_init):
    lam = lam_ref[...]
    lam_full = (jnp.exp(jnp.sum(lam[0:1] * lam[1:2], axis=-1, keepdims=True))
                - jnp.exp(jnp.sum(lam[2:3] * lam[3:4], axis=-1, keepdims=True)) + lambda_init)

    tq = q0_ref.shape[0]
    qs = (q0_ref[...], q1_ref[...])

    def step(carry, ks, v):
        out = []
        for (m, l, acc), q, k in zip(carry, qs, ks):
            s = lax.dot_general(q, k, _NT_DIMS, preferred_element_type=F32)
            m_new = jnp.maximum(m, jnp.max(s, axis=-1, keepdims=True))
            alpha = jnp.exp2(m - m_new)
            p = jnp.exp2(s - m_new)
            l = alpha * l + jnp.sum(p, axis=-1, keepdims=True)
            acc = alpha * acc + jnp.dot(p.astype(BF16), v, preferred_element_type=F32)
            out.append((m_new, l, acc))
        return tuple(out)

    init = tuple((jnp.full((tq, 1), -jnp.inf, F32), jnp.zeros((tq, 1), F32), jnp.zeros((tq, DA_VDIM), F32))
                 for _ in range(2))

    carry = init
    for c in range(SEQ // DA_KCHUNK):
        rows = slice(c * DA_KCHUNK, (c + 1) * DA_KCHUNK)
        carry = step(carry, (k0_ref[rows, :], k1_ref[rows, :]), v_ref[rows, :])
    (_, l0, acc0), (_, l1, acc1) = step(carry, (k0c_ref[...], k1c_ref[...]), vc_ref[...])
    o = acc0 / l0 - lam_full * (acc1 / l1)
    ms = jnp.mean(o * o, axis=-1, keepdims=True)
    o_ref[...] = (o * lax.rsqrt(ms + EPS) * sub_ref[...] * (1.0 - lambda_init)).astype(o_ref.dtype)


def diff_attention(qkv, lam, subln, lambda_init, tq=512):
    nq = SEQ // tq
    ctx_blk = NX // CTX
    kcol = 2 * DA_HEADS
    vcol = 2 * DA_HEADS
    return pl.pallas_call(
        functools.partial(_diff_attn_kernel, lambda_init=lambda_init),
        grid=(BATCH, DA_HEADS, nq),
        in_specs=[
            pl.BlockSpec((tq, HEAD_DIM), lambda b, h, i: (b * nq + i, 2 * h)),
            pl.BlockSpec((tq, HEAD_DIM), lambda b, h, i: (b * nq + i, 2 * h + 1)),
            pl.BlockSpec((SEQ, HEAD_DIM), lambda b, h, i: (b, kcol + 2 * h)),
            pl.BlockSpec((SEQ, HEAD_DIM), lambda b, h, i: (b, kcol + 2 * h + 1)),
            pl.BlockSpec((CTX, HEAD_DIM), lambda b, h, i: (ctx_blk + b, kcol + 2 * h)),
            pl.BlockSpec((CTX, HEAD_DIM), lambda b, h, i: (ctx_blk + b, kcol + 2 * h + 1)),
            pl.BlockSpec((SEQ, DA_VDIM), lambda b, h, i: (b, vcol + h)),
            pl.BlockSpec((CTX, DA_VDIM), lambda b, h, i: (ctx_blk + b, vcol + h)),
            pl.BlockSpec((4, HEAD_DIM), lambda b, h, i: (0, 0)),
            pl.BlockSpec((1, DA_VDIM), lambda b, h, i: (0, 0)),
        ],
        out_specs=pl.BlockSpec((tq, DA_VDIM), lambda b, h, i: (b * nq + i, h)),
        out_shape=jax.ShapeDtypeStruct((NX, DA_HEADS * DA_VDIM), BF16),
        compiler_params=_cp(("parallel", "parallel", "arbitrary"), vmem_mb=56),
        name="diff_attn",
    )(qkv, qkv, qkv, qkv, qkv, qkv, qkv, qkv, lam.astype(F32), subln.reshape(1, DA_VDIM).astype(F32))


def dispatch_tables(gates_t, n_rows):
    nt = n_rows // TOK_TILE
    max_units = (n_rows * TOP_K + nt * N_EXPERTS * (UNIT - 1)) // UNIT
    n_ffn_tiles = (max_units + N_EXPERTS * (FFN_UNITS - 1)) // FFN_UNITS + 1
    gates = gates_t.T.reshape(nt, TOK_TILE, N_EXPERTS)
    sel = gates > 0
    self32 = sel.astype(F32)
    hi = lax.Precision.HIGHEST
    before_t = np.tril(np.ones((TOK_TILE, TOK_TILE), np.float32), -1)
    before_e = np.triu(np.ones((N_EXPERTS, N_EXPERTS), np.float32), 1)
    rank = jnp.einsum("ts,nse->nte", before_t, self32, precision=hi).astype(jnp.int32)
    kr = jnp.einsum("ntf,fe->nte", self32, before_e, precision=hi).astype(jnp.int32)
    cnt = self32.sum(axis=1).astype(jnp.int32)
    nun = (cnt + UNIT - 1) // UNIT
    loc_off = jnp.cumsum(nun, axis=1) - nun
    slot = loc_off[:, None, :] * UNIT + rank
    hit = sel[:, :, None, :] & (kr[:, :, None, :] == jnp.arange(TOP_K)[None, None, :, None])
    slot6 = jnp.where(hit.any(axis=-1), jnp.sum(jnp.where(hit, slot[:, :, None, :], 0), axis=-1), -1)
    w6 = jnp.sum(jnp.where(hit, gates[:, :, None, :], 0.0), axis=-1)
    slot6 = jnp.pad(slot6, ((0, 0), (0, 0), (0, 8 - TOP_K)), constant_values=-1)
    w6 = jnp.pad(w6, ((0, 0), (0, 0), (0, 8 - TOP_K)))

    seg_un = nun.sum(axis=0)
    seg_pad = (seg_un + FFN_UNITS - 1) // FFN_UNITS * FFN_UNITS
    seg_end = jnp.cumsum(seg_pad)
    seg_start = seg_end - seg_pad
    gstart = seg_start[None, :] + jnp.cumsum(nun, axis=0) - nun
    u = jnp.arange(UNITS_PER_TILE, dtype=jnp.int32)
    loc_end = loc_off + nun
    ue = (loc_end[:, None, :] <= u[None, :, None]).sum(axis=-1)
    onehot = ue[:, :, None] == jnp.arange(N_EXPERTS)[None, None, :]
    dst = jnp.sum(jnp.where(onehot, (gstart - loc_off)[:, None, :], 0), axis=-1) + u[None, :]
    n_units_total = n_ffn_tiles * FFN_UNITS
    flat_dst = jnp.where(ue < N_EXPERTS, dst, n_units_total).reshape(-1).astype(jnp.int32)
    src_write = jnp.full((n_units_total + 1,), -1, jnp.int32).at[flat_dst].set(
        jnp.arange(nt * UNITS_PER_TILE, dtype=jnp.int32))[:n_units_total]
    src_read = jnp.where(src_write >= 0, src_write, UNITS_PER_TILE - 1)
    chunk_start = jnp.concatenate([seg_start, seg_end[-1:]]).astype(jnp.int32) // FFN_UNITS
    return dict(slot6=slot6, w6=w6, slot6_t=slot6.transpose(0, 2, 1), src_read=src_read, src_write=src_write,
                chunk_start=chunk_start, nt=nt)


def _moe_gather_kernel(h_ref, slot_ref, o_ref):
    s_iota = lax.broadcasted_iota(jnp.int32, (SLOTS, TOK_TILE), 0)
    slots = slot_ref[0]
    p = jnp.zeros((SLOTS, TOK_TILE), F32)
    for k in range(TOP_K):
        p = jnp.where(s_iota == slots[k:k + 1, :], 1.0, p)
    p = p.astype(BF16)
    o_ref[0] = jnp.dot(p, h_ref[...], preferred_element_type=F32).astype(BF16)


def moe_gather(h, slot6_t, nt):
    return pl.pallas_call(
        _moe_gather_kernel,
        grid=(nt,),
        in_specs=[pl.BlockSpec((TOK_TILE, D), lambda i: (i, 0)), pl.BlockSpec((1, 8, TOK_TILE), lambda i: (i, 0, 0))],
        out_specs=pl.BlockSpec((1, SLOTS, D), lambda i: (i, 0, 0)),
        out_shape=jax.ShapeDtypeStruct((nt, SLOTS, D), BF16),
        compiler_params=_cp(("parallel",)),
        name="moe_gather",
    )(h, slot6_t)


def _unit_copy(src_hbm, buf_ref, sem_ref, slot, src_unit, j):
    return pltpu.make_async_copy(src_hbm.at[pl.ds(pl.multiple_of(src_unit * UNIT, UNIT), UNIT)],
                                 buf_ref.at[slot, pl.ds(j * UNIT, UNIT)], sem_ref.at[slot])


def _fetch_units(table_ref, base, n_units, src_hbm, buf_ref, sem_ref, slot):
    def body(j, _):
        _unit_copy(src_hbm, buf_ref, sem_ref, slot, table_ref[base + j], j).start()
        return 0

    lax.fori_loop(0, n_units, body, 0, unroll=8)


def _wait_units(n_units, src_hbm, buf_ref, sem_ref, slot):
    pltpu.make_async_copy(src_hbm.at[pl.ds(0, n_units * UNIT)], buf_ref.at[slot], sem_ref.at[slot]).wait()


def _moe_ffn_kernel(srcr_ref, srcw_ref, cs_ref, x_hbm, wg_ref, wu_ref, wd_ref, y_hbm, xbuf_ref, ybuf_ref,
                    sem_in, sem_out, wgb_ref, wub_ref, wdb_ref):
    e = pl.program_id(0)
    lo = cs_ref[e]
    hi = cs_ref[e + 1]
    total = cs_ref[N_EXPERTS]

    def out_units(c, slot, start):
        for j in range(FFN_UNITS):
            su = srcw_ref[c * FFN_UNITS + j]

            @pl.when(su >= 0)
            def _():
                cp = pltpu.make_async_copy(ybuf_ref.at[slot, pl.ds(j * UNIT, UNIT)],
                                           y_hbm.at[pl.ds(pl.multiple_of(su * UNIT, UNIT), UNIT)], sem_out.at[slot])
                if start:
                    cp.start()
                else:
                    cp.wait()

    @pl.when(hi > lo)
    def _():
        wgb_ref[...] = wg_ref[0].astype(BF16)
        wub_ref[...] = wu_ref[0].astype(BF16)
        wdb_ref[...] = wd_ref[0].astype(BF16)

    def fetch(c):
        @pl.when(c < total)
        def _():
            _fetch_units(srcr_ref, c * FFN_UNITS, FFN_UNITS, x_hbm, xbuf_ref, sem_in, c % FFN_IN_DEPTH)

    @pl.when(e == 0)
    def _():
        for c in range(FFN_IN_DEPTH - 1):
            fetch(c)

    def chunk(c, _):
        slot = c % 2
        in_slot = c % FFN_IN_DEPTH
        fetch(c + FFN_IN_DEPTH - 1)
        _wait_units(FFN_UNITS, x_hbm, xbuf_ref, sem_in, in_slot)

        @pl.when(c >= 2)
        def _():
            out_units(c - 2, slot, False)

        x = xbuf_ref[in_slot]
        g = jnp.dot(x, wgb_ref[...], preferred_element_type=F32)
        u = jnp.dot(x, wub_ref[...], preferred_element_type=F32)
        a = (_silu(g) * u).astype(BF16)
        ybuf_ref[slot] = jnp.dot(a, wdb_ref[...], preferred_element_type=F32).astype(BF16)
        out_units(c, slot, True)
        return 0

    lax.fori_loop(lo, hi, chunk, 0)

    @pl.when(e == N_EXPERTS - 1)
    def _():
        for back in (2, 1):
            c = total - back

            @pl.when(c >= 0)
            def _():
                out_units(c, c % 2, False)


def moe_ffn(x_tiles, tabs, wg, wu, wd):
    x_flat = x_tiles.reshape(-1, D)
    idx = lambda e, *_: (e, 0, 0)
    grid_spec = pltpu.PrefetchScalarGridSpec(
        num_scalar_prefetch=3,
        grid=(N_EXPERTS,),
        in_specs=[
            pl.BlockSpec(memory_space=pl.ANY),
            pl.BlockSpec((1, D, D_EXPERT), idx),
            pl.BlockSpec((1, D, D_EXPERT), idx),
            pl.BlockSpec((1, D_EXPERT, D), idx),
        ],
        out_specs=pl.BlockSpec(memory_space=pl.ANY),
        scratch_shapes=[pltpu.VMEM((FFN_IN_DEPTH, FFN_TM, D), BF16), pltpu.VMEM((2, FFN_TM, D), BF16),
                        pltpu.SemaphoreType.DMA((FFN_IN_DEPTH,)), pltpu.SemaphoreType.DMA((2,)),
                        pltpu.VMEM((D, D_EXPERT), BF16), pltpu.VMEM((D, D_EXPERT), BF16),
                        pltpu.VMEM((D_EXPERT, D), BF16)],
    )
    y = pl.pallas_call(
        _moe_ffn_kernel,
        grid_spec=grid_spec,
        out_shape=jax.ShapeDtypeStruct(x_flat.shape, BF16),
        input_output_aliases={3: 0},
        compiler_params=_cp(("arbitrary",), vmem_mb=56),
        name="moe_ffn",
    )(tabs["src_read"], tabs["src_write"], tabs["chunk_start"], x_flat, wg, wu, wd)
    return y.reshape(x_tiles.shape)


def _shared_ffn_kernel(a_ref, wg_ref, wu_ref, wd_ref, o_ref):
    a = a_ref[...]
    g = jnp.dot(a, wg_ref[...], preferred_element_type=F32)
    u = jnp.dot(a, wu_ref[...], preferred_element_type=F32)
    o_ref[...] = jnp.dot((_silu(g) * u).astype(BF16), wd_ref[...], preferred_element_type=F32).astype(o_ref.dtype)


def shared_ffn(h, wg, wu, wd, n_rows, tm=512):
    return pl.pallas_call(
        _shared_ffn_kernel,
        grid=(n_rows // tm,),
        in_specs=[
            pl.BlockSpec((tm, D), lambda i: (i, 0)),
            pl.BlockSpec((D, D_EXPERT), lambda i: (0, 0)),
            pl.BlockSpec((D, D_EXPERT), lambda i: (0, 0)),
            pl.BlockSpec((D_EXPERT, D), lambda i: (0, 0)),
        ],
        out_specs=pl.BlockSpec((tm, D), lambda i: (i, 0)),
        out_shape=jax.ShapeDtypeStruct((n_rows, D), BF16),
        compiler_params=_cp(("parallel",)),
        name="shared_ffn",
    )(h, wg, wu, wd)


def _moe_combine_kernel(y_ref, slot_ref, w_ref, sh_ref, res_ref, gm_ref, o_ref):
    lane = lax.broadcasted_iota(jnp.int32, (TOK_TILE, SLOTS), 1)
    slots = slot_ref[0]
    w = w_ref[0]
    pw = jnp.zeros((TOK_TILE, SLOTS), F32)
    for k in range(TOP_K):
        pw = jnp.where(lane == slots[:, k:k + 1], w[:, k:k + 1], pw)
    routed = jnp.dot(pw.astype(BF16), y_ref[0], preferred_element_type=F32)
    o_ref[...] = res_ref[...] + gm_ref[0] * (routed + sh_ref[...].astype(F32))


def moe_combine(y_tiles, tabs, shared, res, mods, n_rows):
    nt = tabs["nt"]
    tm = TOK_TILE
    return pl.pallas_call(
        _moe_combine_kernel,
        grid=(nt,),
        in_specs=[
            pl.BlockSpec((1, SLOTS, D), lambda i: (i, 0, 0)),
            pl.BlockSpec((1, tm, 8), lambda i: (i, 0, 0)),
            pl.BlockSpec((1, tm, 8), lambda i: (i, 0, 0)),
            pl.BlockSpec((tm, D), lambda i: (i, 0)),
            pl.BlockSpec((tm, D), lambda i: (i, 0)),
            pl.BlockSpec((1, 1, D), lambda i: (_seg_of_tile(i, tm) * 6 + 5, 0, 0)),
        ],
        out_specs=pl.BlockSpec((tm, D), lambda i: (i, 0)),
        out_shape=jax.ShapeDtypeStruct((n_rows, D), F32),
        compiler_params=_cp(("parallel",), vmem_mb=56),
        name="moe_combine",
    )(y_tiles, tabs["slot6"], tabs["w6"], shared, res, mods)


def moe_block(xa, gain, mods, router_w, router_b, wg, wu, wd, sg, su, sd, n_rows):
    h, gates_t = norm_route(xa, gain, mods, router_w, router_b, n_rows)
    tabs = dispatch_tables(gates_t, n_rows)
    x_tiles = moe_gather(h, tabs["slot6_t"], tabs["nt"])
    y_tiles = moe_ffn(x_tiles, tabs, wg, wu, wd)
    shared = shared_ffn(h, sg.astype(BF16), su.astype(BF16), sd.astype(BF16), n_rows)
    return moe_combine(y_tiles, tabs, shared, xa, mods, n_rows)


def rope_tables():
    t = jnp.arange(SEQ)
    row = (t // GRID_W).astype(F32)
    col = (t % GRID_W).astype(F32)
    n_freq = HEAD_DIM // 4
    inv_freq = ROPE_THETA ** (-jnp.arange(n_freq, dtype=F32) / n_freq)
    ang = jnp.concatenate([row[:, None] * inv_freq, col[:, None] * inv_freq], axis=-1)
    ang = jnp.concatenate([ang, ang], axis=-1)
    sign = jnp.where(jnp.arange(HEAD_DIM) < HEAD_DIM // 2, -1.0, 1.0)
    cos = jnp.concatenate([jnp.cos(ang)] * BATCH + [jnp.ones((BATCH * CTX, HEAD_DIM), F32)], axis=0)
    sin = jnp.concatenate([jnp.sin(ang) * sign] * BATCH + [jnp.zeros((BATCH * CTX, HEAD_DIM), F32)], axis=0)
    return cos, sin


def even_layer(xa, mods, norm1, w_in, na_qnorm, na_knorm, na_rpb, conv_w, conv_b, gate_b, hnorm, w_out):
    h = norm_mod(xa, norm1, mods, 0, NT)
    w = w_in.astype(BF16)
    c0 = 3 * NA_WIDTH
    c1 = c0 + 2 * ML_WIDTH
    c2 = c1 + 2 * ML_WIDTH
    scale = HEAD_DIM ** -0.5
    gain = jnp.concatenate([jnp.tile(na_qnorm.astype(F32) * scale, NA_HEADS), jnp.tile(na_knorm.astype(F32), NA_HEADS),
                            jnp.ones((NA_WIDTH,), F32)]).reshape(1, c0)
    qkv = proj(h, w[:, :c0], BF16, gain=gain, n_norm_cols=2 * NA_WIDTH)
    na_h = neighbourhood_attention(qkv, na_bias_table(na_rpb))

    ml_qk = proj(h, w[:, c0:c1], F32)
    ml_vo = proj(h, w[:, c1:c2], BF16)
    n_gate = w_in.shape[1] - c2
    w_gate = jnp.zeros((D, HEAD_DIM), BF16).at[:, :n_gate].set(w[:, c2:])
    g = proj(h, w_gate, F32, tn=HEAD_DIM)[:, :n_gate]
    col_scale = jnp.concatenate([jnp.ones((ML_WIDTH,), F32), jnp.full((ML_WIDTH,), HEAD_DIM ** -0.5, F32)])
    qk = conv_silu(ml_qk, conv_w.astype(F32), conv_b.astype(F32), col_scale)
    kt = qk[:, ML_WIDTH:].T
    g = g.reshape(NT, 4, ML_HEADS)
    g = jnp.concatenate([g[NX:].reshape(BATCH, CTX, 4, ML_HEADS), g[:NX].reshape(BATCH, SEQ, 4, ML_HEADS)], axis=1)
    g = g.transpose(0, 3, 1, 2).reshape(BATCH * ML_HEADS, CTX + SEQ, 4)
    g_col = jnp.zeros((BATCH * ML_HEADS, CTX + SEQ, HEAD_DIM), F32).at[:, :, :4].set(g)
    g_row = jnp.zeros((BATCH * ML_HEADS, 8, CTX + SEQ), F32).at[:, :4, :].set(g.transpose(0, 2, 1))
    ml_h = mlstm(qk, kt, ml_vo, g_col, g_row, gate_b, hnorm.astype(F32))

    wo = w_out.astype(BF16)
    return out_proj([na_h, ml_h], [wo[:NA_WIDTH], wo[NA_WIDTH:]], xa, mods, 2, NT)


def odd_layer(xa, mods, norm1, w_in, qnorm, knorm, lam, subln, w_out, lambda_init):
    h = norm_mod(xa, norm1, mods, 0, NT)
    scale = HEAD_DIM ** -0.5 * math.log2(math.e)
    n_qk = 2 * DA_HEADS * HEAD_DIM
    gain = jnp.concatenate([jnp.tile(qnorm.astype(F32) * scale, 2 * DA_HEADS), jnp.tile(knorm.astype(F32), 2 * DA_HEADS),
                            jnp.ones((DA_HEADS * DA_VDIM,), F32)]).reshape(1, -1)
    cos, sin = rope_tables()
    qkv = proj(h, w_in.astype(BF16), BF16, gain=gain, n_norm_cols=2 * n_qk, cos=cos, sin=sin)
    o = diff_attention(qkv, lam, subln, lambda_init)
    return out_proj([o], [w_out.astype(BF16)], xa, mods, 2, NX)


def diff_lambda_init(layer):
    return 0.8 - 0.6 * math.exp(-0.3 * layer)


def kernel(x, c, ctx, c_ctx, l0_ada_w, l0_ada_b, l0_norm1, l0_norm2, l0_w_in, l0_na_qnorm, l0_na_knorm, l0_na_rpb, l0_ml_conv_w, l0_ml_conv_b, l0_ml_gate_b, l0_ml_hnorm, l0_w_out, l0_router_w, l0_router_b, l0_exp_gate, l0_exp_up, l0_exp_down, l0_sh_gate, l0_sh_up, l0_sh_down, l1_ada_w, l1_ada_b, l1_norm1, l1_norm2, l1_w_in, l1_qnorm, l1_knorm, l1_lambda, l1_subln, l1_w_out, l1_router_w, l1_router_b, l1_exp_gate, l1_exp_up, l1_exp_down, l1_sh_gate, l1_sh_up, l1_sh_down):
    assert x.shape == (BATCH, SEQ, D) and ctx.shape == (BATCH, CTX, D)
    xa = jnp.concatenate([x.reshape(NX, D), ctx.reshape(BATCH * CTX, D)], axis=0).astype(F32)
    cvec = jnp.zeros((8, D), F32).at[:BATCH].set(c).at[BATCH].set(c_ctx)

    mods0 = adaln(cvec, l0_ada_w, l0_ada_b)
    xa = even_layer(xa, mods0, l0_norm1, l0_w_in, l0_na_qnorm, l0_na_knorm, l0_na_rpb, l0_ml_conv_w, l0_ml_conv_b,
                    l0_ml_gate_b, l0_ml_hnorm, l0_w_out)
    xa = moe_block(xa, l0_norm2, mods0, l0_router_w, l0_router_b, l0_exp_gate, l0_exp_up, l0_exp_down,
                   l0_sh_gate, l0_sh_up, l0_sh_down, NT)

    mods1 = adaln(cvec, l1_ada_w, l1_ada_b)
    xl = odd_layer(xa, mods1, l1_norm1, l1_w_in, l1_qnorm, l1_knorm, l1_lambda, l1_subln, l1_w_out, diff_lambda_init(1))
    xl = moe_block(xl, l1_norm2, mods1, l1_router_w, l1_router_b, l1_exp_gate, l1_exp_up, l1_exp_down,
                   l1_sh_gate, l1_sh_up, l1_sh_down, NX)
    return xl.reshape(BATCH, SEQ, D)
```

```python
import functools
import math

import jax
import jax.numpy as jnp
import numpy as np
from jax import lax
from jax.experimental import pallas as pl
from jax.experimental.pallas import tpu as pltpu

F32 = jnp.float32
BF16 = jnp.bfloat16

D = 2048
BATCH = 2
SEQ = 4096
CTX = 256
NX = BATCH * SEQ
NT = NX + BATCH * CTX
GRID_W = 64
EPS = 1e-6
NEG_INF = -1e30

NA_HEADS = 8
HEAD_DIM = 128
NA_WIDTH = NA_HEADS * HEAD_DIM
NA_WIN_H = 8
NA_WIN_W = 16
NA_QROWS = 4
NA_KROWS = 12
ML_HEADS = 8
ML_WIDTH = ML_HEADS * HEAD_DIM
ML_CONV = 5
ML_CHUNK = 256
ML_HEADS_PER_STEP = 2

DA_HEADS = 8
DA_VDIM = 256
DA_KCHUNK = 1024
ROPE_THETA = 10000.0

N_EXPERTS = 64
N_GROUPS = 8
TOPK_GROUPS = 4
TOP_K = 6
D_EXPERT = 512
ROUTED_SCALE = 2.5

TOK_TILE = 256
UNIT = 16
UNITS_PER_TILE = (TOK_TILE * TOP_K + N_EXPERTS * (UNIT - 1)) // UNIT + 1
UNITS_PER_TILE = -(-UNITS_PER_TILE // 32) * 32
SLOTS = UNITS_PER_TILE * UNIT
FFN_TM = 256
FFN_UNITS = FFN_TM // UNIT
FFN_IN_DEPTH = 6

V7X_VMEM_BYTES = 64 * 1024 * 1024


def _cp(sem, vmem_mb=48):
    assert vmem_mb * 1024 * 1024 < V7X_VMEM_BYTES
    return pltpu.CompilerParams(dimension_semantics=sem, vmem_limit_bytes=vmem_mb * 1024 * 1024)


def _silu(x):
    return x * jax.nn.sigmoid(x)


def _seg_of_tile(i, tm):
    return (i * tm) // SEQ


def _ada_kernel(c_ref, w_ref, b_ref, o_ref):
    s = _silu(c_ref[...]).astype(BF16)
    o_ref[...] = jnp.dot(s, w_ref[...].astype(BF16), preferred_element_type=F32) + b_ref[...]


def adaln(cvec, w, b):
    n = w.shape[1]
    tn = 1024
    out = pl.pallas_call(
        _ada_kernel,
        grid=(n // tn,),
        in_specs=[
            pl.BlockSpec((8, D), lambda j: (0, 0)),
            pl.BlockSpec((D, tn), lambda j: (0, j)),
            pl.BlockSpec((1, tn), lambda j: (0, j)),
        ],
        out_specs=pl.BlockSpec((8, tn), lambda j: (0, j)),
        out_shape=jax.ShapeDtypeStruct((8, n), F32),
        compiler_params=_cp(("arbitrary",)),
        name="adaln",
    )(cvec, w, b.reshape(1, n))
    return out[:3].reshape(18, 1, D)


def _normed(x_ref, g_ref, sh_ref, sc_ref):
    x = x_ref[...]
    ms = jnp.mean(x * x, axis=-1, keepdims=True)
    y = x * lax.rsqrt(ms + EPS) * g_ref[...]
    return y * (1.0 + sc_ref[0]) + sh_ref[0]


def _norm_mod_kernel(x_ref, g_ref, sh_ref, sc_ref, o_ref):
    o_ref[...] = _normed(x_ref, g_ref, sh_ref, sc_ref).astype(o_ref.dtype)


def _mod_spec(which, tm):
    return pl.BlockSpec((1, 1, D), lambda i: (_seg_of_tile(i, tm) * 6 + which, 0, 0))


def norm_mod(x, gain, mods, which_shift, n_rows, tm=512):
    return pl.pallas_call(
        _norm_mod_kernel,
        grid=(n_rows // tm,),
        in_specs=[
            pl.BlockSpec((tm, D), lambda i: (i, 0)),
            pl.BlockSpec((1, D), lambda i: (0, 0)),
            _mod_spec(which_shift, tm),
            _mod_spec(which_shift + 1, tm),
        ],
        out_specs=pl.BlockSpec((tm, D), lambda i: (i, 0)),
        out_shape=jax.ShapeDtypeStruct((n_rows, D), BF16),
        compiler_params=_cp(("parallel",)),
        name="norm_mod",
    )(x, gain.reshape(1, D), mods, mods)


def _route(logits, bias_col):
    tm = logits.shape[1]
    per_group = N_EXPERTS // N_GROUPS
    scores = jax.nn.sigmoid(logits)
    sel = scores + bias_col
    row8 = lax.broadcasted_iota(jnp.int32, (per_group, tm), 0)
    grp = jnp.zeros((N_GROUPS, tm), F32)
    for g in range(N_GROUPS):
        slab = sel[g * per_group:(g + 1) * per_group, :]
        m1 = jnp.max(slab, axis=0, keepdims=True)
        first = jnp.min(jnp.where(slab == m1, row8, per_group), axis=0, keepdims=True)
        m2 = jnp.max(jnp.where(row8 == first, -jnp.inf, slab), axis=0, keepdims=True)
        grp = jnp.where(row8 == g, m1 + m2, grp)
    rank = jnp.zeros((N_GROUPS, tm), jnp.int32)
    for g in range(N_GROUPS):
        vg = grp[g:g + 1, :]
        beats = (vg > grp) | ((vg == grp) & (g < row8))
        rank = rank + jnp.where(beats, 1, 0)
    keep = jnp.where(rank < TOPK_GROUPS, 1.0, 0.0)
    cur = jnp.concatenate(
        [jnp.where(keep[g:g + 1, :] > 0.5, sel[g * per_group:(g + 1) * per_group, :], NEG_INF) for g in range(N_GROUPS)],
        axis=0)
    e_iota = lax.broadcasted_iota(jnp.int32, (N_EXPERTS, tm), 0)
    picked = jnp.zeros((N_EXPERTS, tm), F32)
    for _ in range(TOP_K):
        m = jnp.max(cur, axis=0, keepdims=True)
        idx = jnp.min(jnp.where(cur == m, e_iota, N_EXPERTS), axis=0, keepdims=True)
        hit = e_iota == idx
        picked = jnp.where(hit, 1.0, picked)
        cur = jnp.where(hit, -jnp.inf, cur)
    w = scores * picked
    return w / jnp.sum(w, axis=0, keepdims=True) * ROUTED_SCALE, picked


def _tile_slots(gates, picked):
    tm = gates.shape[1]
    pick = picked.astype(BF16)
    e_r = lax.broadcasted_iota(jnp.int32, (N_EXPERTS, N_EXPERTS), 0)
    e_c = lax.broadcasted_iota(jnp.int32, (N_EXPERTS, N_EXPERTS), 1)
    lower_e = jnp.where(e_c < e_r, 1.0, 0.0).astype(BF16)
    t_r = lax.broadcasted_iota(jnp.int32, (tm, tm), 0)
    t_c = lax.broadcasted_iota(jnp.int32, (tm, tm), 1)
    before_t = jnp.where(t_r < t_c, 1.0, 0.0).astype(BF16)
    kr = jnp.dot(lower_e, pick, preferred_element_type=F32)
    rank = jnp.dot(pick, before_t, preferred_element_type=F32)
    cnt = jnp.sum(picked, axis=1, keepdims=True)
    units = jnp.floor((cnt + (UNIT - 1)) * (1.0 / UNIT))
    units_w = jnp.broadcast_to(units, (N_EXPERTS, HEAD_DIM))
    first_unit = jnp.dot(lower_e, units_w.astype(BF16), preferred_element_type=F32)[:, 0:1]
    slot = first_unit * UNIT + rank
    row8 = lax.broadcasted_iota(jnp.int32, (8, tm), 0)
    slots = jnp.full((8, tm), -1.0, F32)
    weights = jnp.zeros((8, tm), F32)
    for k in range(TOP_K):
        hit = (picked > 0.5) & (kr == k)
        found = jnp.sum(jnp.where(hit, 1.0, 0.0), axis=0, keepdims=True)
        slot_k = jnp.where(found > 0.5, jnp.sum(jnp.where(hit, slot, 0.0), axis=0, keepdims=True), -1.0)
        w_k = jnp.sum(jnp.where(hit, gates, 0.0), axis=0, keepdims=True)
        slots = jnp.where(row8 == k, slot_k, slots)
        weights = jnp.where(row8 == k, w_k, weights)
    return slots.astype(jnp.int32), weights, units_w


def _norm_route_kernel(x_ref, g_ref, sh_ref, sc_ref, rwh_ref, rwl_ref, rb_ref, o_ref, slot_ref, w_ref, units_ref):
    h = _normed(x_ref, g_ref, sh_ref, sc_ref)
    h_hi = h.astype(BF16)
    o_ref[...] = h_hi
    h_lo = (h - h_hi.astype(F32)).astype(BF16)
    nt = (((1,), (1,)), ((), ()))
    logits = (lax.dot_general(rwh_ref[...], h_hi, nt, preferred_element_type=F32)
              + lax.dot_general(rwh_ref[...], h_lo, nt, preferred_element_type=F32)
              + lax.dot_general(rwl_ref[...], h_hi, nt, preferred_element_type=F32))
    gates, picked = _route(logits, rb_ref[...])
    slot_ref[0], w_ref[0], units_ref[0] = _tile_slots(gates, picked)


def norm_route(x, gain, mods, router_w, router_b, n_rows):
    tm = TOK_TILE
    nt = n_rows // tm
    rwt = router_w.T
    rw_hi = rwt.astype(BF16)
    rw_lo = (rwt - rw_hi.astype(F32)).astype(BF16)
    return pl.pallas_call(
        _norm_route_kernel,
        grid=(nt,),
        in_specs=[
            pl.BlockSpec((tm, D), lambda i: (i, 0)),
            pl.BlockSpec((1, D), lambda i: (0, 0)),
            _mod_spec(3, tm),
            _mod_spec(4, tm),
            pl.BlockSpec((N_EXPERTS, D), lambda i: (0, 0)),
            pl.BlockSpec((N_EXPERTS, D), lambda i: (0, 0)),
            pl.BlockSpec((N_EXPERTS, 1), lambda i: (0, 0)),
        ],
        out_specs=[pl.BlockSpec((tm, D), lambda i: (i, 0)), pl.BlockSpec((1, 8, tm), lambda i: (i, 0, 0)),
                   pl.BlockSpec((1, 8, tm), lambda i: (i, 0, 0)),
                   pl.BlockSpec((1, N_EXPERTS, HEAD_DIM), lambda i: (i, 0, 0))],
        out_shape=[jax.ShapeDtypeStruct((n_rows, D), BF16), jax.ShapeDtypeStruct((nt, 8, tm), jnp.int32),
                   jax.ShapeDtypeStruct((nt, 8, tm), F32), jax.ShapeDtypeStruct((nt, N_EXPERTS, HEAD_DIM), F32)],
        compiler_params=_cp(("parallel",)),
        name="norm_route",
    )(x, gain.reshape(1, D), mods, mods, rw_hi, rw_lo, router_b.reshape(N_EXPERTS, 1))


def _head_norm(acc, gain, g):
    a = acc[:, g * HEAD_DIM:(g + 1) * HEAD_DIM]
    ms = jnp.mean(a * a, axis=-1, keepdims=True)
    return a * lax.rsqrt(ms + EPS) * gain[:, g * HEAD_DIM:(g + 1) * HEAD_DIM]


def _proj_kernel(*refs, n_norm_tiles, rope):
    if rope:
        a_ref, w_ref, gain_ref, cos_ref, sin_ref, o_ref = refs
    else:
        a_ref, w_ref, gain_ref, o_ref = refs
    j = pl.program_id(0)
    acc = jnp.dot(a_ref[...], w_ref[...], preferred_element_type=F32)
    tn = acc.shape[1]

    @pl.when(j < n_norm_tiles)
    def _():
        gain = gain_ref[...]
        for g in range(tn // HEAD_DIM):
            y = _head_norm(acc, gain, g)
            if rope:
                y = y * cos_ref[...] + pltpu.roll(y, HEAD_DIM // 2, axis=1) * sin_ref[...]
            o_ref[:, g * HEAD_DIM:(g + 1) * HEAD_DIM] = y.astype(o_ref.dtype)

    @pl.when(j >= n_norm_tiles)
    def _():
        o_ref[...] = acc.astype(o_ref.dtype)


def proj(a, w, out_dtype, *, gain=None, n_norm_cols=0, cos=None, sin=None, tm=512, tn=1024):
    m, k = a.shape
    n = w.shape[1]
    tn = min(tn, n)
    rope = cos is not None
    if gain is None:
        gain = jnp.ones((1, n), F32)
    in_specs = [
        pl.BlockSpec((tm, k), lambda j, i: (i, 0)),
        pl.BlockSpec((k, tn), lambda j, i: (0, j)),
        pl.BlockSpec((1, tn), lambda j, i: (0, j)),
    ]
    args = [a, w, gain]
    if rope:
        in_specs += [pl.BlockSpec((tm, HEAD_DIM), lambda j, i: (i, 0))] * 2
        args += [cos, sin]
    assert n_norm_cols % tn == 0
    return pl.pallas_call(
        functools.partial(_proj_kernel, n_norm_tiles=n_norm_cols // tn, rope=rope),
        grid=(n // tn, m // tm),
        in_specs=in_specs,
        out_specs=pl.BlockSpec((tm, tn), lambda j, i: (i, j)),
        out_shape=jax.ShapeDtypeStruct((m, n), out_dtype),
        compiler_params=_cp(("arbitrary", "arbitrary")),
        name="proj",
    )(*args)


def _out_proj_kernel(*refs, n_a):
    a_refs = refs[:n_a]
    w_refs = refs[n_a:2 * n_a]
    res_ref, gm_ref, o_ref = refs[2 * n_a:]
    acc = jnp.dot(a_refs[0][...], w_refs[0][...], preferred_element_type=F32)
    for a_ref, w_ref in zip(a_refs[1:], w_refs[1:]):
        acc = acc + jnp.dot(a_ref[...], w_ref[...], preferred_element_type=F32)
    o_ref[...] = res_ref[...] + gm_ref[0] * acc


def out_proj(a_list, w_list, res, mods, which_gate, n_rows, tm=512, tn=1024):
    n_a = len(a_list)
    n = w_list[0].shape[1]
    in_specs = [pl.BlockSpec((tm, a.shape[1]), lambda j, i: (i, 0)) for a in a_list]
    in_specs += [pl.BlockSpec((w.shape[0], tn), lambda j, i: (0, j)) for w in w_list]
    in_specs += [
        pl.BlockSpec((tm, tn), lambda j, i: (i, j)),
        pl.BlockSpec((1, 1, tn), lambda j, i: (_seg_of_tile(i, tm) * 6 + which_gate, 0, j)),
    ]
    return pl.pallas_call(
        functools.partial(_out_proj_kernel, n_a=n_a),
        grid=(n // tn, n_rows // tm),
        in_specs=in_specs,
        out_specs=pl.BlockSpec((tm, tn), lambda j, i: (i, j)),
        out_shape=jax.ShapeDtypeStruct((n_rows, n), F32),
        compiler_params=_cp(("arbitrary", "arbitrary")),
        name="out_proj",
    )(*a_list, *w_list, res, mods)


def na_bias_table(rpb):
    rows = SEQ // GRID_W
    n_dc = 2 * NA_WIN_W - 1
    cols = np.arange(GRID_W)
    col_start = np.clip(cols - NA_WIN_W // 2, 0, GRID_W - NA_WIN_W)
    col_ok = (cols[None, :] >= col_start[:, None]) & (cols[None, :] < col_start[:, None] + NA_WIN_W)
    col_idx = np.clip(cols[None, :] - cols[:, None], 1 - NA_WIN_W, NA_WIN_W - 1) + (NA_WIN_W - 1)
    col_pick = (col_idx[None] == np.arange(n_dc)[:, None, None]).astype(np.float32)
    by_col = jnp.einsum("hdk,kqc->hdqc", rpb.astype(F32), col_pick, precision=lax.Precision.HIGHEST)
    by_col = jnp.where(col_ok[None, None], by_col, NEG_INF)
    masked = jnp.full((NA_HEADS, GRID_W, GRID_W), NEG_INF, F32)
    tables = []
    for r0 in (0, 2 * NA_QROWS, rows - NA_QROWS):
        kstart = int(np.clip(r0 - NA_WIN_H // 2, 0, rows - NA_KROWS))
        q_rows = []
        for i in range(NA_QROWS):
            r = r0 + i
            win = int(np.clip(r - NA_WIN_H // 2, 0, rows - NA_WIN_H))
            blocks = []
            for j in range(NA_KROWS):
                kr = kstart + j
                in_window = win <= kr < win + NA_WIN_H
                blocks.append(by_col[:, kr - r + NA_WIN_H - 1] if in_window else masked)
            q_rows.append(jnp.concatenate(blocks, axis=-1))
        tables.append(jnp.concatenate(q_rows, axis=1))
    return jnp.stack(tables, axis=1)


def _softmax_pv(pieces):
    m = functools.reduce(jnp.maximum, [jnp.max(s, axis=-1, keepdims=True) for s, _ in pieces])
    ps = [jnp.exp(s - m) for s, _ in pieces]
    l = functools.reduce(lambda a, b: a + b, [jnp.sum(p, axis=-1, keepdims=True) for p in ps])
    o = functools.reduce(lambda a, b: a + b,
                         [jnp.dot(p.astype(BF16), v, preferred_element_type=F32) for p, (_, v) in zip(ps, pieces)])
    return o / l


_NT_DIMS = (((1,), (1,)), ((), ()))


def _na_kernel(q_ref, k_ref, v_ref, kc_ref, vc_ref, bias_ref, o_ref):
    qb = pl.program_id(2)
    rows = SEQ // GRID_W
    kstart = pl.multiple_of(jnp.clip(qb * NA_QROWS - NA_WIN_H // 2, 0, rows - NA_KROWS) * GRID_W, GRID_W)
    q = q_ref[...]
    kw = k_ref[pl.ds(kstart, NA_KROWS * GRID_W), :]
    vw = v_ref[pl.ds(kstart, NA_KROWS * GRID_W), :]
    s_loc = lax.dot_general(q, kw, _NT_DIMS, preferred_element_type=F32) + bias_ref[0, 0]
    s_ctx = lax.dot_general(q, kc_ref[...], _NT_DIMS, preferred_element_type=F32)
    o_ref[...] = _softmax_pv([(s_loc, vw), (s_ctx, vc_ref[...])]).astype(o_ref.dtype)


def _ctx_attn_kernel(q_ref, k_ref, v_ref, o_ref):
    s = lax.dot_general(q_ref[...], k_ref[...], _NT_DIMS, preferred_element_type=F32)
    o_ref[...] = _softmax_pv([(s, v_ref[...])]).astype(o_ref.dtype)


def neighbourhood_attention(qkv, bias):
    nqb = SEQ // (NA_QROWS * GRID_W)
    tq = NA_QROWS * GRID_W
    ctx_blk = NX // CTX
    lat = pl.pallas_call(
        _na_kernel,
        grid=(BATCH, NA_HEADS, nqb),
        in_specs=[
            pl.BlockSpec((tq, HEAD_DIM), lambda b, h, i: (b * nqb + i, h)),
            pl.BlockSpec((SEQ, HEAD_DIM), lambda b, h, i: (b, NA_HEADS + h)),
            pl.BlockSpec((SEQ, HEAD_DIM), lambda b, h, i: (b, 2 * NA_HEADS + h)),
            pl.BlockSpec((CTX, HEAD_DIM), lambda b, h, i: (ctx_blk + b, NA_HEADS + h)),
            pl.BlockSpec((CTX, HEAD_DIM), lambda b, h, i: (ctx_blk + b, 2 * NA_HEADS + h)),
            pl.BlockSpec((1, 1, tq, NA_KROWS * GRID_W),
                         lambda b, h, i: (h, jnp.where(i == 0, 0, jnp.where(i == nqb - 1, 2, 1)), 0, 0)),
        ],
        out_specs=pl.BlockSpec((tq, HEAD_DIM), lambda b, h, i: (b * nqb + i, h)),
        out_shape=jax.ShapeDtypeStruct((NX, NA_WIDTH), BF16),
        compiler_params=_cp(("parallel", "parallel", "arbitrary")),
        name="na_attn",
    )(qkv, qkv, qkv, qkv, qkv, bias)
    ctx = pl.pallas_call(
        _ctx_attn_kernel,
        grid=(BATCH, NA_HEADS),
        in_specs=[
            pl.BlockSpec((CTX, HEAD_DIM), lambda b, h: (ctx_blk + b, h)),
            pl.BlockSpec((CTX, HEAD_DIM), lambda b, h: (ctx_blk + b, NA_HEADS + h)),
            pl.BlockSpec((CTX, HEAD_DIM), lambda b, h: (ctx_blk + b, 2 * NA_HEADS + h)),
        ],
        out_specs=pl.BlockSpec((CTX, HEAD_DIM), lambda b, h: (b, h)),
        out_shape=jax.ShapeDtypeStruct((BATCH * CTX, NA_WIDTH), BF16),
        compiler_params=_cp(("parallel", "parallel")),
        name="na_ctx_attn",
    )(qkv, qkv, qkv)
    return jnp.concatenate([lat, ctx], axis=0)


_CONV_HALO = 8


def _conv_kernel(prev_ref, cur_ref, next_ref, w_ref, b_ref, cs_ref, o_ref, buf_ref):
    i = pl.program_id(0)
    tm = cur_ref.shape[0]
    tiles_per_seq = SEQ // tm
    n_lat = NX // tm
    first = (i % tiles_per_seq == 0) | (i >= n_lat)
    last = (i % tiles_per_seq == tiles_per_seq - 1) | (i >= n_lat)
    buf_ref[0:_CONV_HALO, :] = prev_ref[...] * jnp.where(first, 0.0, 1.0)
    buf_ref[_CONV_HALO:_CONV_HALO + tm, :] = cur_ref[...]
    buf_ref[_CONV_HALO + tm:, :] = next_ref[...] * jnp.where(last, 0.0, 1.0)
    acc = jnp.zeros(cur_ref.shape, F32) + b_ref[...]
    for j in range(ML_CONV):
        off = _CONV_HALO + j - ML_CONV // 2
        acc = acc + buf_ref[off:off + tm, :] * w_ref[j:j + 1, :]
    o_ref[...] = (_silu(acc) * cs_ref[...]).astype(o_ref.dtype)


def conv_silu(t, w, b, col_scale):
    tm = CTX
    c = t.shape[1]
    hb = tm // _CONV_HALO
    n_halo_blocks = NT // _CONV_HALO
    wp = jnp.zeros((8, c), F32).at[:ML_CONV].set(w)
    return pl.pallas_call(
        _conv_kernel,
        grid=(NT // tm,),
        in_specs=[
            pl.BlockSpec((_CONV_HALO, c), lambda i: (jnp.maximum(i * hb - 1, 0), 0)),
            pl.BlockSpec((tm, c), lambda i: (i, 0)),
            pl.BlockSpec((_CONV_HALO, c), lambda i: (jnp.minimum((i + 1) * hb, n_halo_blocks - 1), 0)),
            pl.BlockSpec((8, c), lambda i: (0, 0)),
            pl.BlockSpec((1, c), lambda i: (0, 0)),
            pl.BlockSpec((1, c), lambda i: (0, 0)),
        ],
        out_specs=pl.BlockSpec((tm, c), lambda i: (i, 0)),
        out_shape=jax.ShapeDtypeStruct((NT, c), BF16),
        scratch_shapes=[pltpu.VMEM((tm + 2 * _CONV_HALO, c), F32)],
        compiler_params=_cp(("parallel",)),
        name="conv_silu",
    )(t, t, t, wp, b.reshape(1, c), col_scale.reshape(1, c))


def _split3(x):
    hi = x.astype(BF16)
    r = x - hi.astype(F32)
    mid = r.astype(BF16)
    lo = (r - mid.astype(F32)).astype(BF16)
    return hi, mid, lo


def _log_sigmoid(x):
    return jnp.minimum(x, 0.0) - jnp.log1p(jnp.exp(-jnp.abs(x)))


def _mlstm_chunk(reverse, q, kt, v_ext, gc, gr, gb_col, gb_row, c_ref, m):
    ln = q.shape[0]
    d = 1 if reverse else 0
    i_col = gc[:, 2 * d:2 * d + 1] + gb_col[:, 2 * d:2 * d + 1]
    f_col = _log_sigmoid(gc[:, 2 * d + 1:2 * d + 2] + gb_col[:, 2 * d + 1:2 * d + 2])
    i_row = gr[2 * d:2 * d + 1, :] + gb_row[2 * d:2 * d + 1, :]
    f_row = _log_sigmoid(gr[2 * d + 1:2 * d + 2, :] + gb_row[2 * d + 1:2 * d + 2, :])
    t_idx = lax.broadcasted_iota(jnp.int32, (ln, ln), 0)
    s_idx = lax.broadcasted_iota(jnp.int32, (ln, ln), 1)
    causal = (s_idx >= t_idx) if reverse else (s_idx <= t_idx)
    tri = jnp.where(causal, 1.0, 0.0).astype(BF16)
    f_col_w = jnp.broadcast_to(f_col, (ln, HEAD_DIM))
    b_col = functools.reduce(lambda a, b: a + b,
                             [jnp.dot(tri, p, preferred_element_type=F32) for p in _split3(f_col_w)])[:, 0:1]
    f_row_w = jnp.broadcast_to(f_row, (16, ln))
    b_row = functools.reduce(lambda a, b: a + b,
                             [lax.dot_general(p, tri, _NT_DIMS, preferred_element_type=F32) for p in _split3(f_row_w)])[0:1, :]
    total = jnp.sum(f_col, axis=0, keepdims=True)

    dmat = jnp.where(causal, b_col - b_row + i_row, -jnp.inf)
    inter = b_col + m
    m_t = jnp.maximum(inter, jnp.max(dmat, axis=-1, keepdims=True))
    w_inter = jnp.exp(inter - m_t)
    s = jnp.dot(q, kt, preferred_element_type=F32) * jnp.exp(dmat - m_t)
    numden = (w_inter * jnp.dot(q, c_ref[...].astype(BF16), preferred_element_type=F32)
              + jnp.dot(s.astype(BF16), v_ext, preferred_element_type=F32))
    den = numden[:, HEAD_DIM:HEAD_DIM + 1]
    h = numden[:, :HEAD_DIM] / jnp.maximum(jnp.abs(den), jnp.exp(-m_t))

    g = total - b_col + i_col
    m_new = jnp.maximum(total + m, jnp.max(g, axis=0, keepdims=True))
    decay = jnp.exp(total + m - m_new)
    wg = jnp.exp(g - m_new)
    upd = jnp.dot(kt, (wg * v_ext.astype(F32)).astype(BF16), preferred_element_type=F32)
    c_ref[...] = decay * c_ref[...] + upd
    return h, m_new


def _mlstm_kernel(q_ref, kt_ref, v_ref, o_ref, qc_ref, ktc_ref, vc_ref, oc_ref, gc_ref, gr_ref, gbc_ref, gbr_ref,
                  hn_ref, out_ref, outc_ref, cf_ref, cb_ref, hf_ref, hb_ref):
    ln = ML_CHUNK
    n_chunks = SEQ // ln
    nh = ML_HEADS_PER_STEP
    ones_col = jnp.where(lax.broadcasted_iota(jnp.int32, (ln, HEAD_DIM), 1) == 0, 1.0, 0.0).astype(BF16)

    def v_ext(v):
        return jnp.concatenate([v, ones_col], axis=1)

    def cols(hd):
        return slice(hd * HEAD_DIM, (hd + 1) * HEAD_DIM)

    def finish(h, o_gate, hn):
        ms = jnp.mean(h * h, axis=-1, keepdims=True)
        return (h * lax.rsqrt(ms + EPS) * hn * jax.nn.sigmoid(o_gate.astype(F32))).astype(BF16)

    def finish_heads(h, o_gate):
        return jnp.concatenate([finish(h[:, cols(hd)], o_gate[:, cols(hd)], hn_ref[:, cols(hd)]) for hd in range(nh)],
                               axis=1)

    cf_ref[...] = jnp.zeros(cf_ref.shape, F32)
    cb_ref[...] = jnp.zeros(cb_ref.shape, F32)
    m0 = jnp.zeros((1, 1), F32)

    ms = []
    for hd in range(nh):
        gc = gc_ref[hd, 0:ln, :]
        gr = gr_ref[hd, :, 0:ln]
        vx = v_ext(vc_ref[:, cols(hd)])
        hf, mf = _mlstm_chunk(False, qc_ref[:, cols(hd)], ktc_ref[cols(hd), :], vx, gc, gr, gbc_ref[hd], gbr_ref[hd],
                              cf_ref.at[hd], m0)
        hb, mb = _mlstm_chunk(True, qc_ref[:, cols(hd)], ktc_ref[cols(hd), :], vx, gc, gr, gbc_ref[hd], gbr_ref[hd],
                              cb_ref.at[hd], m0)
        outc_ref[:, cols(hd)] = finish(hf + hb, oc_ref[:, cols(hd)], hn_ref[:, cols(hd)])
        ms += [mf, mb]

    def body(c, carry):
        carry = list(carry)
        for hd in range(nh):
            for reverse, c_ref, h_ref in ((False, cf_ref, hf_ref), (True, cb_ref, hb_ref)):
                cc = (n_chunks - 1 - c) if reverse else c
                r0 = pl.multiple_of(cc * ln, ln)
                g0 = pl.multiple_of(cc * ln + CTX, ln)
                k = 2 * hd + int(reverse)
                h, carry[k] = _mlstm_chunk(reverse, q_ref[pl.ds(r0, ln), cols(hd)], kt_ref[cols(hd), pl.ds(r0, ln)],
                                           v_ext(v_ref[pl.ds(r0, ln), cols(hd)]), gc_ref[hd, pl.ds(g0, ln), :],
                                           gr_ref[hd, :, pl.ds(g0, ln)], gbc_ref[hd], gbr_ref[hd], c_ref.at[hd],
                                           carry[k])
                h_ref[pl.ds(r0, ln), cols(hd)] = h
        return tuple(carry)

    lax.fori_loop(0, n_chunks, body, tuple(ms))

    def fin_body(c, _):
        r0 = pl.multiple_of(c * ln, ln)
        out_ref[pl.ds(r0, ln), :] = finish_heads(hf_ref[pl.ds(r0, ln), :] + hb_ref[pl.ds(r0, ln), :],
                                                 o_ref[pl.ds(r0, ln), :])
        return 0

    lax.fori_loop(0, n_chunks, fin_body, 0)


def mlstm(qk, kt, vo, gates_col, gates_row, gate_b, hnorm):
    hh = ML_HEADS
    nh = ML_HEADS_PER_STEP
    hs = hh // nh
    w = nh * HEAD_DIM
    ctx_blk = NX // CTX
    tot = CTX + SEQ
    gb = gate_b.astype(F32).transpose(2, 0, 1).reshape(hh, 4)
    gb_col = jnp.zeros((hh, 1, HEAD_DIM), F32).at[:, 0, :4].set(gb)
    gb_row = jnp.zeros((hh, 8, 1), F32).at[:, :4, 0].set(gb)
    lat, ctx = pl.pallas_call(
        _mlstm_kernel,
        grid=(BATCH, hs),
        in_specs=[
            pl.BlockSpec((SEQ, w), lambda b, h: (b, h)),
            pl.BlockSpec((w, SEQ), lambda b, h: (h, b)),
            pl.BlockSpec((SEQ, w), lambda b, h: (b, h)),
            pl.BlockSpec((SEQ, w), lambda b, h: (b, hs + h)),
            pl.BlockSpec((CTX, w), lambda b, h: (ctx_blk + b, h)),
            pl.BlockSpec((w, CTX), lambda b, h: (h, ctx_blk + b)),
            pl.BlockSpec((CTX, w), lambda b, h: (ctx_blk + b, h)),
            pl.BlockSpec((CTX, w), lambda b, h: (ctx_blk + b, hs + h)),
            pl.BlockSpec((nh, tot, HEAD_DIM), lambda b, h: (b * hs + h, 0, 0)),
            pl.BlockSpec((nh, 8, tot), lambda b, h: (b * hs + h, 0, 0)),
            pl.BlockSpec((nh, 1, HEAD_DIM), lambda b, h: (h, 0, 0)),
            pl.BlockSpec((nh, 8, 1), lambda b, h: (h, 0, 0)),
            pl.BlockSpec((1, w), lambda b, h: (0, h)),
        ],
        out_specs=[pl.BlockSpec((SEQ, w), lambda b, h: (b, h)), pl.BlockSpec((CTX, w), lambda b, h: (b, h))],
        out_shape=[jax.ShapeDtypeStruct((NX, ML_WIDTH), BF16), jax.ShapeDtypeStruct((BATCH * CTX, ML_WIDTH), BF16)],
        scratch_shapes=[
            pltpu.VMEM((nh, HEAD_DIM, 2 * HEAD_DIM), F32),
            pltpu.VMEM((nh, HEAD_DIM, 2 * HEAD_DIM), F32),
            pltpu.VMEM((SEQ, w), F32),
            pltpu.VMEM((SEQ, w), F32),
        ],
        compiler_params=_cp(("parallel", "parallel")),
        name="mlstm",
    )(qk, kt, vo, vo, qk, kt, vo, vo, gates_col, gates_row, gb_col, gb_row, hnorm.reshape(1, ML_WIDTH))
    return jnp.concatenate([lat, ctx], axis=0)


def _diff_attn_kernel(q0_ref, q1_ref, k0_ref, k1_ref, k0c_ref, k1c_ref, v_ref, vc_ref, lam_ref, sub_ref, o_ref,
                      *, lambda_init):
    lam = lam_ref[...]
    lam_full = (jnp.exp(jnp.sum(lam[0:1] * lam[1:2], axis=-1, keepdims=True))
                - jnp.exp(jnp.sum(lam[2:3] * lam[3:4], axis=-1, keepdims=True)) + lambda_init)

    tq = q0_ref.shape[0]
    qs = (q0_ref[...], q1_ref[...])

    def step(carry, ks, v):
        out = []
        for (m, l, acc), q, k in zip(carry, qs, ks):
            s = lax.dot_general(q, k, _NT_DIMS, preferred_element_type=F32)
            m_new = jnp.maximum(m, jnp.max(s, axis=-1, keepdims=True))
            alpha = jnp.exp2(m - m_new)
            p = jnp.exp2(s - m_new)
            l = alpha * l + jnp.sum(p, axis=-1, keepdims=True)
            acc = alpha * acc + jnp.dot(p.astype(BF16), v, preferred_element_type=F32)
            out.append((m_new, l, acc))
        return tuple(out)

    init = tuple((jnp.full((tq, 1), -jnp.inf, F32), jnp.zeros((tq, 1), F32), jnp.zeros((tq, DA_VDIM), F32))
                 for _ in range(2))

    carry = init
    for c in range(SEQ // DA_KCHUNK):
        rows = slice(c * DA_KCHUNK, (c + 1) * DA_KCHUNK)
        carry = step(carry, (k0_ref[rows, :], k1_ref[rows, :]), v_ref[rows, :])
    (_, l0, acc0), (_, l1, acc1) = step(carry, (k0c_ref[...], k1c_ref[...]), vc_ref[...])
    o = acc0 / l0 - lam_full * (acc1 / l1)
    ms = jnp.mean(o * o, axis=-1, keepdims=True)
    o_ref[...] = (o * lax.rsqrt(ms + EPS) * sub_ref[...] * (1.0 - lambda_init)).astype(o_ref.dtype)


def diff_attention(qkv, lam, subln, lambda_init, tq=512):
    nq = SEQ // tq
    ctx_blk = NX // CTX
    kcol = 2 * DA_HEADS
    vcol = 2 * DA_HEADS
    return pl.pallas_call(
        functools.partial(_diff_attn_kernel, lambda_init=lambda_init),
        grid=(BATCH, DA_HEADS, nq),
        in_specs=[
            pl.BlockSpec((tq, HEAD_DIM), lambda b, h, i: (b * nq + i, 2 * h)),
            pl.BlockSpec((tq, HEAD_DIM), lambda b, h, i: (b * nq + i, 2 * h + 1)),
            pl.BlockSpec((SEQ, HEAD_DIM), lambda b, h, i: (b, kcol + 2 * h)),
            pl.BlockSpec((SEQ, HEAD_DIM), lambda b, h, i: (b, kcol + 2 * h + 1)),
            pl.BlockSpec((CTX, HEAD_DIM), lambda b, h, i: (ctx_blk + b, kcol + 2 * h)),
            pl.BlockSpec((CTX, HEAD_DIM), lambda b, h, i: (ctx_blk + b, kcol + 2 * h + 1)),
            pl.BlockSpec((SEQ, DA_VDIM), lambda b, h, i: (b, vcol + h)),
            pl.BlockSpec((CTX, DA_VDIM), lambda b, h, i: (ctx_blk + b, vcol + h)),
            pl.BlockSpec((4, HEAD_DIM), lambda b, h, i: (0, 0)),
            pl.BlockSpec((1, DA_VDIM), lambda b, h, i: (0, 0)),
        ],
        out_specs=pl.BlockSpec((tq, DA_VDIM), lambda b, h, i: (b * nq + i, h)),
        out_shape=jax.ShapeDtypeStruct((NX, DA_HEADS * DA_VDIM), BF16),
        compiler_params=_cp(("parallel", "parallel", "arbitrary"), vmem_mb=56),
        name="diff_attn",
    )(qkv, qkv, qkv, qkv, qkv, qkv, qkv, qkv, lam.astype(F32), subln.reshape(1, DA_VDIM).astype(F32))


def dispatch_tables(slot6_t, w6_t, units, n_rows):
    nt = n_rows // TOK_TILE
    max_units = (n_rows * TOP_K + nt * N_EXPERTS * (UNIT - 1)) // UNIT
    n_ffn_tiles = (max_units + N_EXPERTS * (FFN_UNITS - 1)) // FFN_UNITS + 1
    nun = units[:, :, 0].astype(jnp.int32)
    loc_off = jnp.cumsum(nun, axis=1) - nun
    slot6 = slot6_t.transpose(0, 2, 1)
    w6 = w6_t.transpose(0, 2, 1)

    seg_un = nun.sum(axis=0)
    seg_pad = (seg_un + FFN_UNITS - 1) // FFN_UNITS * FFN_UNITS
    seg_end = jnp.cumsum(seg_pad)
    seg_start = seg_end - seg_pad
    gstart = seg_start[None, :] + jnp.cumsum(nun, axis=0) - nun
    u = jnp.arange(UNITS_PER_TILE, dtype=jnp.int32)
    loc_end = loc_off + nun
    ue = (loc_end[:, None, :] <= u[None, :, None]).sum(axis=-1)
    onehot = ue[:, :, None] == jnp.arange(N_EXPERTS)[None, None, :]
    dst = jnp.sum(jnp.where(onehot, (gstart - loc_off)[:, None, :], 0), axis=-1) + u[None, :]
    n_units_total = n_ffn_tiles * FFN_UNITS
    flat_dst = jnp.where(ue < N_EXPERTS, dst, n_units_total).reshape(-1).astype(jnp.int32)
    src_write = jnp.full((n_units_total + 1,), -1, jnp.int32).at[flat_dst].set(
        jnp.arange(nt * UNITS_PER_TILE, dtype=jnp.int32))[:n_units_total]
    src_read = jnp.where(src_write >= 0, src_write, UNITS_PER_TILE - 1)
    chunk_start = jnp.concatenate([seg_start, seg_end[-1:]]).astype(jnp.int32) // FFN_UNITS
    return dict(slot6=slot6, w6=w6, slot6_t=slot6_t, src_read=src_read, src_write=src_write,
                chunk_start=chunk_start, nt=nt)


def _moe_gather_kernel(h_ref, slot_ref, o_ref):
    s_iota = lax.broadcasted_iota(jnp.int32, (SLOTS, TOK_TILE), 0)
    slots = slot_ref[0]
    p = jnp.zeros((SLOTS, TOK_TILE), F32)
    for k in range(TOP_K):
        p = jnp.where(s_iota == slots[k:k + 1, :], 1.0, p)
    p = p.astype(BF16)
    o_ref[0] = jnp.dot(p, h_ref[...], preferred_element_type=F32).astype(BF16)


def moe_gather(h, slot6_t, nt):
    return pl.pallas_call(
        _moe_gather_kernel,
        grid=(nt,),
        in_specs=[pl.BlockSpec((TOK_TILE, D), lambda i: (i, 0)), pl.BlockSpec((1, 8, TOK_TILE), lambda i: (i, 0, 0))],
        out_specs=pl.BlockSpec((1, SLOTS, D), lambda i: (i, 0, 0)),
        out_shape=jax.ShapeDtypeStruct((nt, SLOTS, D), BF16),
        compiler_params=_cp(("parallel",)),
        name="moe_gather",
    )(h, slot6_t)


def _unit_copy(src_hbm, buf_ref, sem_ref, slot, src_unit, j):
    return pltpu.make_async_copy(src_hbm.at[pl.ds(pl.multiple_of(src_unit * UNIT, UNIT), UNIT)],
                                 buf_ref.at[slot, pl.ds(j * UNIT, UNIT)], sem_ref.at[slot])


def _fetch_units(table_ref, base, n_units, src_hbm, buf_ref, sem_ref, slot):
    def body(j, _):
        _unit_copy(src_hbm, buf_ref, sem_ref, slot, table_ref[base + j], j).start()
        return 0

    lax.fori_loop(0, n_units, body, 0, unroll=8)


def _wait_units(n_units, src_hbm, buf_ref, sem_ref, slot):
    pltpu.make_async_copy(src_hbm.at[pl.ds(0, n_units * UNIT)], buf_ref.at[slot], sem_ref.at[slot]).wait()


def _moe_ffn_kernel(srcr_ref, srcw_ref, cs_ref, x_hbm, wg_ref, wu_ref, wd_ref, y_hbm, xbuf_ref, ybuf_ref,
                    sem_in, sem_out, wgb_ref, wub_ref, wdb_ref):
    e = pl.program_id(0)
    lo = cs_ref[e]
    hi = cs_ref[e + 1]
    total = cs_ref[N_EXPERTS]

    def out_units(c, slot, start):
        for j in range(FFN_UNITS):
            su = srcw_ref[c * FFN_UNITS + j]

            @pl.when(su >= 0)
            def _():
                cp = pltpu.make_async_copy(ybuf_ref.at[slot, pl.ds(j * UNIT, UNIT)],
                                           y_hbm.at[pl.ds(pl.multiple_of(su * UNIT, UNIT), UNIT)], sem_out.at[slot])
                if start:
                    cp.start()
                else:
                    cp.wait()

    @pl.when(hi > lo)
    def _():
        wgb_ref[...] = wg_ref[0].astype(BF16)
        wub_ref[...] = wu_ref[0].astype(BF16)
        wdb_ref[...] = wd_ref[0].astype(BF16)

    def fetch(c):
        @pl.when(c < total)
        def _():
            _fetch_units(srcr_ref, c * FFN_UNITS, FFN_UNITS, x_hbm, xbuf_ref, sem_in, c % FFN_IN_DEPTH)

    @pl.when(e == 0)
    def _():
        for c in range(FFN_IN_DEPTH - 1):
            fetch(c)

    def chunk(c, _):
        slot = c % 2
        in_slot = c % FFN_IN_DEPTH
        fetch(c + FFN_IN_DEPTH - 1)
        _wait_units(FFN_UNITS, x_hbm, xbuf_ref, sem_in, in_slot)

        @pl.when(c >= 2)
        def _():
            out_units(c - 2, slot, False)

        x = xbuf_ref[in_slot]
        g = jnp.dot(x, wgb_ref[...], preferred_element_type=F32)
        u = jnp.dot(x, wub_ref[...], preferred_element_type=F32)
        a = (_silu(g) * u).astype(BF16)
        ybuf_ref[slot] = jnp.dot(a, wdb_ref[...], preferred_element_type=F32).astype(BF16)
        out_units(c, slot, True)
        return 0

    lax.fori_loop(lo, hi, chunk, 0)

    @pl.when(e == N_EXPERTS - 1)
    def _():
        for back in (2, 1):
            c = total - back

            @pl.when(c >= 0)
            def _():
                out_units(c, c % 2, False)


def moe_ffn(x_tiles, tabs, wg, wu, wd):
    x_flat = x_tiles.reshape(-1, D)
    idx = lambda e, *_: (e, 0, 0)
    grid_spec = pltpu.PrefetchScalarGridSpec(
        num_scalar_prefetch=3,
        grid=(N_EXPERTS,),
        in_specs=[
            pl.BlockSpec(memory_space=pl.ANY),
            pl.BlockSpec((1, D, D_EXPERT), idx),
            pl.BlockSpec((1, D, D_EXPERT), idx),
            pl.BlockSpec((1, D_EXPERT, D), idx),
        ],
        out_specs=pl.BlockSpec(memory_space=pl.ANY),
        scratch_shapes=[pltpu.VMEM((FFN_IN_DEPTH, FFN_TM, D), BF16), pltpu.VMEM((2, FFN_TM, D), BF16),
                        pltpu.SemaphoreType.DMA((FFN_IN_DEPTH,)), pltpu.SemaphoreType.DMA((2,)),
                        pltpu.VMEM((D, D_EXPERT), BF16), pltpu.VMEM((D, D_EXPERT), BF16),
                        pltpu.VMEM((D_EXPERT, D), BF16)],
    )
    y = pl.pallas_call(
        _moe_ffn_kernel,
        grid_spec=grid_spec,
        out_shape=jax.ShapeDtypeStruct(x_flat.shape, BF16),
        input_output_aliases={3: 0},
        compiler_params=_cp(("arbitrary",), vmem_mb=56),
        name="moe_ffn",
    )(tabs["src_read"], tabs["src_write"], tabs["chunk_start"], x_flat, wg, wu, wd)
    return y.reshape(x_tiles.shape)


def _shared_ffn_kernel(a_ref, wg_ref, wu_ref, wd_ref, o_ref):
    a = a_ref[...]
    g = jnp.dot(a, wg_ref[...], preferred_element_type=F32)
    u = jnp.dot(a, wu_ref[...], preferred_element_type=F32)
    o_ref[...] = jnp.dot((_silu(g) * u).astype(BF16), wd_ref[...], preferred_element_type=F32).astype(o_ref.dtype)


def shared_ffn(h, wg, wu, wd, n_rows, tm=512):
    return pl.pallas_call(
        _shared_ffn_kernel,
        grid=(n_rows // tm,),
        in_specs=[
            pl.BlockSpec((tm, D), lambda i: (i, 0)),
            pl.BlockSpec((D, D_EXPERT), lambda i: (0, 0)),
            pl.BlockSpec((D, D_EXPERT), lambda i: (0, 0)),
            pl.BlockSpec((D_EXPERT, D), lambda i: (0, 0)),
        ],
        out_specs=pl.BlockSpec((tm, D), lambda i: (i, 0)),
        out_shape=jax.ShapeDtypeStruct((n_rows, D), BF16),
        compiler_params=_cp(("parallel",)),
        name="shared_ffn",
    )(h, wg, wu, wd)


def _moe_combine_kernel(y_ref, slot_ref, w_ref, sh_ref, res_ref, gm_ref, o_ref):
    lane = lax.broadcasted_iota(jnp.int32, (TOK_TILE, SLOTS), 1)
    slots = slot_ref[0]
    w = w_ref[0]
    pw = jnp.zeros((TOK_TILE, SLOTS), F32)
    for k in range(TOP_K):
        pw = jnp.where(lane == slots[:, k:k + 1], w[:, k:k + 1], pw)
    routed = jnp.dot(pw.astype(BF16), y_ref[0], preferred_element_type=F32)
    o_ref[...] = res_ref[...] + gm_ref[0] * (routed + sh_ref[...].astype(F32))


def moe_combine(y_tiles, tabs, shared, res, mods, n_rows):
    nt = tabs["nt"]
    tm = TOK_TILE
    return pl.pallas_call(
        _moe_combine_kernel,
        grid=(nt,),
        in_specs=[
            pl.BlockSpec((1, SLOTS, D), lambda i: (i, 0, 0)),
            pl.BlockSpec((1, tm, 8), lambda i: (i, 0, 0)),
            pl.BlockSpec((1, tm, 8), lambda i: (i, 0, 0)),
            pl.BlockSpec((tm, D), lambda i: (i, 0)),
            pl.BlockSpec((tm, D), lambda i: (i, 0)),
            pl.BlockSpec((1, 1, D), lambda i: (_seg_of_tile(i, tm) * 6 + 5, 0, 0)),
        ],
        out_specs=pl.BlockSpec((tm, D), lambda i: (i, 0)),
        out_shape=jax.ShapeDtypeStruct((n_rows, D), F32),
        compiler_params=_cp(("parallel",), vmem_mb=56),
        name="moe_combine",
    )(y_tiles, tabs["slot6"], tabs["w6"], shared, res, mods)


def moe_block(xa, gain, mods, router_w, router_b, wg, wu, wd, sg, su, sd, n_rows):
    h, slot6_t, w6_t, units = norm_route(xa, gain, mods, router_w, router_b, n_rows)
    tabs = dispatch_tables(slot6_t, w6_t, units, n_rows)
    x_tiles = moe_gather(h, tabs["slot6_t"], tabs["nt"])
    y_tiles = moe_ffn(x_tiles, tabs, wg, wu, wd)
    shared = shared_ffn(h, sg.astype(BF16), su.astype(BF16), sd.astype(BF16), n_rows)
    return moe_combine(y_tiles, tabs, shared, xa, mods, n_rows)


def rope_tables():
    t = jnp.arange(SEQ)
    row = (t // GRID_W).astype(F32)
    col = (t % GRID_W).astype(F32)
    n_freq = HEAD_DIM // 4
    inv_freq = ROPE_THETA ** (-jnp.arange(n_freq, dtype=F32) / n_freq)
    ang = jnp.concatenate([row[:, None] * inv_freq, col[:, None] * inv_freq], axis=-1)
    ang = jnp.concatenate([ang, ang], axis=-1)
    sign = jnp.where(jnp.arange(HEAD_DIM) < HEAD_DIM // 2, -1.0, 1.0)
    cos = jnp.concatenate([jnp.cos(ang)] * BATCH + [jnp.ones((BATCH * CTX, HEAD_DIM), F32)], axis=0)
    sin = jnp.concatenate([jnp.sin(ang) * sign] * BATCH + [jnp.zeros((BATCH * CTX, HEAD_DIM), F32)], axis=0)
    return cos, sin


def even_layer(xa, mods, norm1, w_in, na_qnorm, na_knorm, na_rpb, conv_w, conv_b, gate_b, hnorm, w_out):
    h = norm_mod(xa, norm1, mods, 0, NT)
    w = w_in.astype(BF16)
    c0 = 3 * NA_WIDTH
    c1 = c0 + 2 * ML_WIDTH
    c2 = c1 + 2 * ML_WIDTH
    scale = HEAD_DIM ** -0.5
    gain = jnp.concatenate([jnp.tile(na_qnorm.astype(F32) * scale, NA_HEADS), jnp.tile(na_knorm.astype(F32), NA_HEADS),
                            jnp.ones((NA_WIDTH,), F32)]).reshape(1, c0)
    qkv = proj(h, w[:, :c0], BF16, gain=gain, n_norm_cols=2 * NA_WIDTH)
    na_h = neighbourhood_attention(qkv, na_bias_table(na_rpb))

    ml_qk = proj(h, w[:, c0:c1], F32)
    ml_vo = proj(h, w[:, c1:c2], BF16)
    n_gate = w_in.shape[1] - c2
    w_gate = jnp.zeros((D, HEAD_DIM), BF16).at[:, :n_gate].set(w[:, c2:])
    g = proj(h, w_gate, F32, tn=HEAD_DIM)[:, :n_gate]
    col_scale = jnp.concatenate([jnp.ones((ML_WIDTH,), F32), jnp.full((ML_WIDTH,), HEAD_DIM ** -0.5, F32)])
    qk = conv_silu(ml_qk, conv_w.astype(F32), conv_b.astype(F32), col_scale)
    kt = qk[:, ML_WIDTH:].T
    g = g.reshape(NT, 4, ML_HEADS)
    g = jnp.concatenate([g[NX:].reshape(BATCH, CTX, 4, ML_HEADS), g[:NX].reshape(BATCH, SEQ, 4, ML_HEADS)], axis=1)
    g = g.transpose(0, 3, 1, 2).reshape(BATCH * ML_HEADS, CTX + SEQ, 4)
    g_col = jnp.zeros((BATCH * ML_HEADS, CTX + SEQ, HEAD_DIM), F32).at[:, :, :4].set(g)
    g_row = jnp.zeros((BATCH * ML_HEADS, 8, CTX + SEQ), F32).at[:, :4, :].set(g.transpose(0, 2, 1))
    ml_h = mlstm(qk, kt, ml_vo, g_col, g_row, gate_b, hnorm.astype(F32))

    wo = w_out.astype(BF16)
    return out_proj([na_h, ml_h], [wo[:NA_WIDTH], wo[NA_WIDTH:]], xa, mods, 2, NT)


def odd_layer(xa, mods, norm1, w_in, qnorm, knorm, lam, subln, w_out, lambda_init):
    h = norm_mod(xa, norm1, mods, 0, NT)
    scale = HEAD_DIM ** -0.5 * math.log2(math.e)
    n_qk = 2 * DA_HEADS * HEAD_DIM
    gain = jnp.concatenate([jnp.tile(qnorm.astype(F32) * scale, 2 * DA_HEADS), jnp.tile(knorm.astype(F32), 2 * DA_HEADS),
                            jnp.ones((DA_HEADS * DA_VDIM,), F32)]).reshape(1, -1)
    cos, sin = rope_tables()
    qkv = proj(h, w_in.astype(BF16), BF16, gain=gain, n_norm_cols=2 * n_qk, cos=cos, sin=sin)
    o = diff_attention(qkv, lam, subln, lambda_init)
    return out_proj([o], [w_out.astype(BF16)], xa, mods, 2, NX)


def diff_lambda_init(layer):
    return 0.8 - 0.6 * math.exp(-0.3 * layer)


def kernel(x, c, ctx, c_ctx, l0_ada_w, l0_ada_b, l0_norm1, l0_norm2, l0_w_in, l0_na_qnorm, l0_na_knorm, l0_na_rpb, l0_ml_conv_w, l0_ml_conv_b, l0_ml_gate_b, l0_ml_hnorm, l0_w_out, l0_router_w, l0_router_b, l0_exp_gate, l0_exp_up, l0_exp_down, l0_sh_gate, l0_sh_up, l0_sh_down, l1_ada_w, l1_ada_b, l1_norm1, l1_norm2, l1_w_in, l1_qnorm, l1_knorm, l1_lambda, l1_subln, l1_w_out, l1_router_w, l1_router_b, l1_exp_gate, l1_exp_up, l1_exp_down, l1_sh_gate, l1_sh_up, l1_sh_down):
    assert x.shape == (BATCH, SEQ, D) and ctx.shape == (BATCH, CTX, D)
    xa = jnp.concatenate([x.reshape(NX, D), ctx.reshape(BATCH * CTX, D)], axis=0).astype(F32)
    cvec = jnp.zeros((8, D), F32).at[:BATCH].set(c).at[BATCH].set(c_ctx)

    mods0 = adaln(cvec, l0_ada_w, l0_ada_b)
    xa = even_layer(xa, mods0, l0_norm1, l0_w_in, l0_na_qnorm, l0_na_knorm, l0_na_rpb, l0_ml_conv_w, l0_ml_conv_b,
                    l0_ml_gate_b, l0_ml_hnorm, l0_w_out)
    xa = moe_block(xa, l0_norm2, mods0, l0_router_w, l0_router_b, l0_exp_gate, l0_exp_up, l0_exp_down,
                   l0_sh_gate, l0_sh_up, l0_sh_down, NT)

    mods1 = adaln(cvec, l1_ada_w, l1_ada_b)
    xl = odd_layer(xa, mods1, l1_norm1, l1_w_in, l1_qnorm, l1_knorm, l1_lambda, l1_subln, l1_w_out, diff_lambda_init(1))
    xl = moe_block(xl, l1_norm2, mods1, l1_router_w, l1_router_b, l1_exp_gate, l1_exp_up, l1_exp_down,
                   l1_sh_gate, l1_sh_up, l1_sh_down, NX)
    return xl.reshape(BATCH, SEQ, D)
```

```python
import functools
import math

import jax
import jax.numpy as jnp
import numpy as np
from jax import lax
from jax.experimental import pallas as pl
from jax.experimental.pallas import tpu as pltpu

F32 = jnp.float32
BF16 = jnp.bfloat16

D = 2048
BATCH = 2
SEQ = 4096
CTX = 256
NX = BATCH * SEQ
NT = NX + BATCH * CTX
GRID_W = 64
EPS = 1e-6
NEG_INF = -1e30

NA_HEADS = 8
HEAD_DIM = 128
NA_WIDTH = NA_HEADS * HEAD_DIM
NA_WIN_H = 8
NA_WIN_W = 16
NA_QROWS = 4
NA_KROWS = 12
NA_HEADS_PER_STEP = 2
ML_HEADS = 8
ML_WIDTH = ML_HEADS * HEAD_DIM
ML_CONV = 5
ML_CHUNK = 256
ML_HEADS_PER_STEP = 2

DA_HEADS = 8
DA_VDIM = 256
DA_KCHUNK = 1024
ROPE_THETA = 10000.0

N_EXPERTS = 64
N_GROUPS = 8
TOPK_GROUPS = 4
TOP_K = 6
D_EXPERT = 512
ROUTED_SCALE = 2.5

TOK_TILE = 256
UNIT = 16
UNITS_PER_TILE = (TOK_TILE * TOP_K + N_EXPERTS * (UNIT - 1)) // UNIT + 1
UNITS_PER_TILE = -(-UNITS_PER_TILE // 32) * 32
SLOTS = UNITS_PER_TILE * UNIT
FFN_TM = 256
FFN_UNITS = FFN_TM // UNIT
FFN_IN_DEPTH = 6

V7X_VMEM_BYTES = 64 * 1024 * 1024


def _cp(sem, vmem_mb=48):
    assert vmem_mb * 1024 * 1024 < V7X_VMEM_BYTES
    return pltpu.CompilerParams(dimension_semantics=sem, vmem_limit_bytes=vmem_mb * 1024 * 1024)


def _silu(x):
    return x * jax.nn.sigmoid(x)


def _seg_of_tile(i, tm):
    return (i * tm) // SEQ


def _ada_kernel(c_ref, w_ref, b_ref, o_ref):
    s = _silu(c_ref[...]).astype(BF16)
    o_ref[...] = jnp.dot(s, w_ref[...].astype(BF16), preferred_element_type=F32) + b_ref[...]


def adaln(cvec, w, b):
    n = w.shape[1]
    tn = 1024
    out = pl.pallas_call(
        _ada_kernel,
        grid=(n // tn,),
        in_specs=[
            pl.BlockSpec((8, D), lambda j: (0, 0)),
            pl.BlockSpec((D, tn), lambda j: (0, j)),
            pl.BlockSpec((1, tn), lambda j: (0, j)),
        ],
        out_specs=pl.BlockSpec((8, tn), lambda j: (0, j)),
        out_shape=jax.ShapeDtypeStruct((8, n), F32),
        compiler_params=_cp(("arbitrary",)),
        name="adaln",
    )(cvec, w, b.reshape(1, n))
    return out[:3].reshape(18, 1, D)


def _normed(x_ref, g_ref, sh_ref, sc_ref):
    x = x_ref[...]
    ms = jnp.mean(x * x, axis=-1, keepdims=True)
    y = x * lax.rsqrt(ms + EPS) * g_ref[...]
    return y * (1.0 + sc_ref[0]) + sh_ref[0]


def _norm_mod_kernel(x_ref, g_ref, sh_ref, sc_ref, o_ref):
    o_ref[...] = _normed(x_ref, g_ref, sh_ref, sc_ref).astype(o_ref.dtype)


def _mod_spec(which, tm):
    return pl.BlockSpec((1, 1, D), lambda i: (_seg_of_tile(i, tm) * 6 + which, 0, 0))


def norm_mod(x, gain, mods, which_shift, n_rows, tm=512):
    return pl.pallas_call(
        _norm_mod_kernel,
        grid=(n_rows // tm,),
        in_specs=[
            pl.BlockSpec((tm, D), lambda i: (i, 0)),
            pl.BlockSpec((1, D), lambda i: (0, 0)),
            _mod_spec(which_shift, tm),
            _mod_spec(which_shift + 1, tm),
        ],
        out_specs=pl.BlockSpec((tm, D), lambda i: (i, 0)),
        out_shape=jax.ShapeDtypeStruct((n_rows, D), BF16),
        compiler_params=_cp(("parallel",)),
        name="norm_mod",
    )(x, gain.reshape(1, D), mods, mods)


def _route(logits, bias_col):
    tm = logits.shape[1]
    per_group = N_EXPERTS // N_GROUPS
    scores = jax.nn.sigmoid(logits)
    sel = scores + bias_col
    row8 = lax.broadcasted_iota(jnp.int32, (per_group, tm), 0)
    grp = jnp.zeros((N_GROUPS, tm), F32)
    for g in range(N_GROUPS):
        slab = sel[g * per_group:(g + 1) * per_group, :]
        m1 = jnp.max(slab, axis=0, keepdims=True)
        first = jnp.min(jnp.where(slab == m1, row8, per_group), axis=0, keepdims=True)
        m2 = jnp.max(jnp.where(row8 == first, -jnp.inf, slab), axis=0, keepdims=True)
        grp = jnp.where(row8 == g, m1 + m2, grp)
    rank = jnp.zeros((N_GROUPS, tm), jnp.int32)
    for g in range(N_GROUPS):
        vg = grp[g:g + 1, :]
        beats = (vg > grp) | ((vg == grp) & (g < row8))
        rank = rank + jnp.where(beats, 1, 0)
    keep = jnp.where(rank < TOPK_GROUPS, 1.0, 0.0)
    cur = jnp.concatenate(
        [jnp.where(keep[g:g + 1, :] > 0.5, sel[g * per_group:(g + 1) * per_group, :], NEG_INF) for g in range(N_GROUPS)],
        axis=0)
    e_iota = lax.broadcasted_iota(jnp.int32, (N_EXPERTS, tm), 0)
    picked = jnp.zeros((N_EXPERTS, tm), F32)
    for _ in range(TOP_K):
        m = jnp.max(cur, axis=0, keepdims=True)
        idx = jnp.min(jnp.where(cur == m, e_iota, N_EXPERTS), axis=0, keepdims=True)
        hit = e_iota == idx
        picked = jnp.where(hit, 1.0, picked)
        cur = jnp.where(hit, -jnp.inf, cur)
    w = scores * picked
    return w / jnp.sum(w, axis=0, keepdims=True) * ROUTED_SCALE, picked


def _tile_slots(gates, picked):
    tm = gates.shape[1]
    pick = picked.astype(BF16)
    e_r = lax.broadcasted_iota(jnp.int32, (N_EXPERTS, N_EXPERTS), 0)
    e_c = lax.broadcasted_iota(jnp.int32, (N_EXPERTS, N_EXPERTS), 1)
    lower_e = jnp.where(e_c < e_r, 1.0, 0.0).astype(BF16)
    t_r = lax.broadcasted_iota(jnp.int32, (tm, tm), 0)
    t_c = lax.broadcasted_iota(jnp.int32, (tm, tm), 1)
    before_t = jnp.where(t_r < t_c, 1.0, 0.0).astype(BF16)
    kr = jnp.dot(lower_e, pick, preferred_element_type=F32)
    rank = jnp.dot(pick, before_t, preferred_element_type=F32)
    cnt = jnp.sum(picked, axis=1, keepdims=True)
    units = jnp.floor((cnt + (UNIT - 1)) * (1.0 / UNIT))
    units_w = jnp.broadcast_to(units, (N_EXPERTS, HEAD_DIM))
    first_unit = jnp.dot(lower_e, units_w.astype(BF16), preferred_element_type=F32)[:, 0:1]
    slot = first_unit * UNIT + rank
    row8 = lax.broadcasted_iota(jnp.int32, (8, tm), 0)
    slots = jnp.full((8, tm), -1.0, F32)
    weights = jnp.zeros((8, tm), F32)
    for k in range(TOP_K):
        hit = (picked > 0.5) & (kr == k)
        found = jnp.sum(jnp.where(hit, 1.0, 0.0), axis=0, keepdims=True)
        slot_k = jnp.where(found > 0.5, jnp.sum(jnp.where(hit, slot, 0.0), axis=0, keepdims=True), -1.0)
        w_k = jnp.sum(jnp.where(hit, gates, 0.0), axis=0, keepdims=True)
        slots = jnp.where(row8 == k, slot_k, slots)
        weights = jnp.where(row8 == k, w_k, weights)
    return slots.astype(jnp.int32), weights, units_w


def _norm_route_kernel(x_ref, g_ref, sh_ref, sc_ref, rwh_ref, rwl_ref, rb_ref, o_ref, slot_ref, w_ref, units_ref):
    h = _normed(x_ref, g_ref, sh_ref, sc_ref)
    h_hi = h.astype(BF16)
    o_ref[...] = h_hi
    h_lo = (h - h_hi.astype(F32)).astype(BF16)
    nt = (((1,), (1,)), ((), ()))
    logits = (lax.dot_general(rwh_ref[...], h_hi, nt, preferred_element_type=F32)
              + lax.dot_general(rwh_ref[...], h_lo, nt, preferred_element_type=F32)
              + lax.dot_general(rwl_ref[...], h_hi, nt, preferred_element_type=F32))
    gates, picked = _route(logits, rb_ref[...])
    slot_ref[0], w_ref[0], units_ref[0] = _tile_slots(gates, picked)


def norm_route(x, gain, mods, router_w, router_b, n_rows):
    tm = TOK_TILE
    nt = n_rows // tm
    rwt = router_w.T
    rw_hi = rwt.astype(BF16)
    rw_lo = (rwt - rw_hi.astype(F32)).astype(BF16)
    return pl.pallas_call(
        _norm_route_kernel,
        grid=(nt,),
        in_specs=[
            pl.BlockSpec((tm, D), lambda i: (i, 0)),
            pl.BlockSpec((1, D), lambda i: (0, 0)),
            _mod_spec(3, tm),
            _mod_spec(4, tm),
            pl.BlockSpec((N_EXPERTS, D), lambda i: (0, 0)),
            pl.BlockSpec((N_EXPERTS, D), lambda i: (0, 0)),
            pl.BlockSpec((N_EXPERTS, 1), lambda i: (0, 0)),
        ],
        out_specs=[pl.BlockSpec((tm, D), lambda i: (i, 0)), pl.BlockSpec((1, 8, tm), lambda i: (i, 0, 0)),
                   pl.BlockSpec((1, 8, tm), lambda i: (i, 0, 0)),
                   pl.BlockSpec((1, N_EXPERTS, HEAD_DIM), lambda i: (i, 0, 0))],
        out_shape=[jax.ShapeDtypeStruct((n_rows, D), BF16), jax.ShapeDtypeStruct((nt, 8, tm), jnp.int32),
                   jax.ShapeDtypeStruct((nt, 8, tm), F32), jax.ShapeDtypeStruct((nt, N_EXPERTS, HEAD_DIM), F32)],
        compiler_params=_cp(("parallel",)),
        name="norm_route",
    )(x, gain.reshape(1, D), mods, mods, rw_hi, rw_lo, router_b.reshape(N_EXPERTS, 1))


def _head_norm(acc, gain, g):
    a = acc[:, g * HEAD_DIM:(g + 1) * HEAD_DIM]
    ms = jnp.mean(a * a, axis=-1, keepdims=True)
    return a * lax.rsqrt(ms + EPS) * gain[:, g * HEAD_DIM:(g + 1) * HEAD_DIM]


def _proj_kernel(*refs, n_norm_tiles, rope):
    if rope:
        a_ref, w_ref, gain_ref, cos_ref, sin_ref, o_ref = refs
    else:
        a_ref, w_ref, gain_ref, o_ref = refs
    j = pl.program_id(0)
    acc = jnp.dot(a_ref[...], w_ref[...], preferred_element_type=F32)
    tn = acc.shape[1]

    @pl.when(j < n_norm_tiles)
    def _():
        gain = gain_ref[...]
        for g in range(tn // HEAD_DIM):
            y = _head_norm(acc, gain, g)
            if rope:
                y = y * cos_ref[...] + pltpu.roll(y, HEAD_DIM // 2, axis=1) * sin_ref[...]
            o_ref[:, g * HEAD_DIM:(g + 1) * HEAD_DIM] = y.astype(o_ref.dtype)

    @pl.when(j >= n_norm_tiles)
    def _():
        o_ref[...] = acc.astype(o_ref.dtype)


def proj(a, w, out_dtype, *, gain=None, n_norm_cols=0, cos=None, sin=None, tm=512, tn=1024):
    m, k = a.shape
    n = w.shape[1]
    tn = min(tn, n)
    rope = cos is not None
    if gain is None:
        gain = jnp.ones((1, n), F32)
    in_specs = [
        pl.BlockSpec((tm, k), lambda j, i: (i, 0)),
        pl.BlockSpec((k, tn), lambda j, i: (0, j)),
        pl.BlockSpec((1, tn), lambda j, i: (0, j)),
    ]
    args = [a, w, gain]
    if rope:
        in_specs += [pl.BlockSpec((tm, HEAD_DIM), lambda j, i: (i, 0))] * 2
        args += [cos, sin]
    assert n_norm_cols % tn == 0
    return pl.pallas_call(
        functools.partial(_proj_kernel, n_norm_tiles=n_norm_cols // tn, rope=rope),
        grid=(n // tn, m // tm),
        in_specs=in_specs,
        out_specs=pl.BlockSpec((tm, tn), lambda j, i: (i, j)),
        out_shape=jax.ShapeDtypeStruct((m, n), out_dtype),
        compiler_params=_cp(("arbitrary", "arbitrary")),
        name="proj",
    )(*args)


def _out_proj_kernel(*refs, n_a):
    a_refs = refs[:n_a]
    w_refs = refs[n_a:2 * n_a]
    res_ref, gm_ref, o_ref = refs[2 * n_a:]
    acc = jnp.dot(a_refs[0][...], w_refs[0][...], preferred_element_type=F32)
    for a_ref, w_ref in zip(a_refs[1:], w_refs[1:]):
        acc = acc + jnp.dot(a_ref[...], w_ref[...], preferred_element_type=F32)
    o_ref[...] = res_ref[...] + gm_ref[0] * acc


def out_proj(a_list, w_list, res, mods, which_gate, n_rows, tm=512, tn=1024):
    n_a = len(a_list)
    n = w_list[0].shape[1]
    in_specs = [pl.BlockSpec((tm, a.shape[1]), lambda j, i: (i, 0)) for a in a_list]
    in_specs += [pl.BlockSpec((w.shape[0], tn), lambda j, i: (0, j)) for w in w_list]
    in_specs += [
        pl.BlockSpec((tm, tn), lambda j, i: (i, j)),
        pl.BlockSpec((1, 1, tn), lambda j, i: (_seg_of_tile(i, tm) * 6 + which_gate, 0, j)),
    ]
    return pl.pallas_call(
        functools.partial(_out_proj_kernel, n_a=n_a),
        grid=(n // tn, n_rows // tm),
        in_specs=in_specs,
        out_specs=pl.BlockSpec((tm, tn), lambda j, i: (i, j)),
        out_shape=jax.ShapeDtypeStruct((n_rows, n), F32),
        compiler_params=_cp(("arbitrary", "arbitrary")),
        name="out_proj",
    )(*a_list, *w_list, res, mods)


def na_bias_table(rpb):
    rows = SEQ // GRID_W
    n_dc = 2 * NA_WIN_W - 1
    cols = np.arange(GRID_W)
    col_start = np.clip(cols - NA_WIN_W // 2, 0, GRID_W - NA_WIN_W)
    col_ok = (cols[None, :] >= col_start[:, None]) & (cols[None, :] < col_start[:, None] + NA_WIN_W)
    col_idx = np.clip(cols[None, :] - cols[:, None], 1 - NA_WIN_W, NA_WIN_W - 1) + (NA_WIN_W - 1)
    col_pick = (col_idx[None] == np.arange(n_dc)[:, None, None]).astype(np.float32)
    by_col = jnp.einsum("hdk,kqc->hdqc", rpb.astype(F32), col_pick, precision=lax.Precision.HIGHEST)
    by_col = jnp.where(col_ok[None, None], by_col, NEG_INF)
    masked = jnp.full((NA_HEADS, GRID_W, GRID_W), NEG_INF, F32)
    tables = []
    for r0 in (0, 2 * NA_QROWS, rows - NA_QROWS):
        kstart = int(np.clip(r0 - NA_WIN_H // 2, 0, rows - NA_KROWS))
        q_rows = []
        for i in range(NA_QROWS):
            r = r0 + i
            win = int(np.clip(r - NA_WIN_H // 2, 0, rows - NA_WIN_H))
            blocks = []
            for j in range(NA_KROWS):
                kr = kstart + j
                in_window = win <= kr < win + NA_WIN_H
                blocks.append(by_col[:, kr - r + NA_WIN_H - 1] if in_window else masked)
            q_rows.append(jnp.concatenate(blocks, axis=-1))
        tables.append(jnp.concatenate(q_rows, axis=1))
    return jnp.stack(tables, axis=1)


def _softmax_pv(pieces):
    m = functools.reduce(jnp.maximum, [jnp.max(s, axis=-1, keepdims=True) for s, _ in pieces])
    ps = [jnp.exp(s - m) for s, _ in pieces]
    l = functools.reduce(lambda a, b: a + b, [jnp.sum(p, axis=-1, keepdims=True) for p in ps])
    o = functools.reduce(lambda a, b: a + b,
                         [jnp.dot(p.astype(BF16), v, preferred_element_type=F32) for p, (_, v) in zip(ps, pieces)])
    return o / l


_NT_DIMS = (((1,), (1,)), ((), ()))


def _na_kernel(q_ref, k_ref, v_ref, kc_ref, vc_ref, bias_ref, o_ref):
    qb = pl.program_id(2)
    rows = SEQ // GRID_W
    kstart = pl.multiple_of(jnp.clip(qb * NA_QROWS - NA_WIN_H // 2, 0, rows - NA_KROWS) * GRID_W, GRID_W)
    for hd in range(NA_HEADS_PER_STEP):
        cols = slice(hd * HEAD_DIM, (hd + 1) * HEAD_DIM)
        q = q_ref[:, cols]
        kw = k_ref[pl.ds(kstart, NA_KROWS * GRID_W), cols]
        vw = v_ref[pl.ds(kstart, NA_KROWS * GRID_W), cols]
        s_loc = lax.dot_general(q, kw, _NT_DIMS, preferred_element_type=F32) + bias_ref[hd, 0]
        s_ctx = lax.dot_general(q, kc_ref[:, cols], _NT_DIMS, preferred_element_type=F32)
        o_ref[:, cols] = _softmax_pv([(s_loc, vw), (s_ctx, vc_ref[:, cols])]).astype(o_ref.dtype)


def _ctx_attn_kernel(q_ref, k_ref, v_ref, o_ref):
    s = lax.dot_general(q_ref[...], k_ref[...], _NT_DIMS, preferred_element_type=F32)
    o_ref[...] = _softmax_pv([(s, v_ref[...])]).astype(o_ref.dtype)


def neighbourhood_attention(qkv, bias):
    nqb = SEQ // (NA_QROWS * GRID_W)
    tq = NA_QROWS * GRID_W
    ctx_blk = NX // CTX
    nh = NA_HEADS_PER_STEP
    hs = NA_HEADS // nh
    w = nh * HEAD_DIM
    lat = pl.pallas_call(
        _na_kernel,
        grid=(BATCH, hs, nqb),
        in_specs=[
            pl.BlockSpec((tq, w), lambda b, h, i: (b * nqb + i, h)),
            pl.BlockSpec((SEQ, w), lambda b, h, i: (b, hs + h)),
            pl.BlockSpec((SEQ, w), lambda b, h, i: (b, 2 * hs + h)),
            pl.BlockSpec((CTX, w), lambda b, h, i: (ctx_blk + b, hs + h)),
            pl.BlockSpec((CTX, w), lambda b, h, i: (ctx_blk + b, 2 * hs + h)),
            pl.BlockSpec((nh, 1, tq, NA_KROWS * GRID_W),
                         lambda b, h, i: (h, jnp.where(i == 0, 0, jnp.where(i == nqb - 1, 2, 1)), 0, 0)),
        ],
        out_specs=pl.BlockSpec((tq, w), lambda b, h, i: (b * nqb + i, h)),
        out_shape=jax.ShapeDtypeStruct((NX, NA_WIDTH), BF16),
        compiler_params=_cp(("parallel", "parallel", "arbitrary")),
        name="na_attn",
    )(qkv, qkv, qkv, qkv, qkv, bias)
    ctx = pl.pallas_call(
        _ctx_attn_kernel,
        grid=(BATCH, NA_HEADS),
        in_specs=[
            pl.BlockSpec((CTX, HEAD_DIM), lambda b, h: (ctx_blk + b, h)),
            pl.BlockSpec((CTX, HEAD_DIM), lambda b, h: (ctx_blk + b, NA_HEADS + h)),
            pl.BlockSpec((CTX, HEAD_DIM), lambda b, h: (ctx_blk + b, 2 * NA_HEADS + h)),
        ],
        out_specs=pl.BlockSpec((CTX, HEAD_DIM), lambda b, h: (b, h)),
        out_shape=jax.ShapeDtypeStruct((BATCH * CTX, NA_WIDTH), BF16),
        compiler_params=_cp(("parallel", "parallel")),
        name="na_ctx_attn",
    )(qkv, qkv, qkv)
    return jnp.concatenate([lat, ctx], axis=0)


_CONV_HALO = 8


def _conv_kernel(prev_ref, cur_ref, next_ref, w_ref, b_ref, cs_ref, o_ref, buf_ref):
    i = pl.program_id(0)
    tm = cur_ref.shape[0]
    tiles_per_seq = SEQ // tm
    n_lat = NX // tm
    first = (i % tiles_per_seq == 0) | (i >= n_lat)
    last = (i % tiles_per_seq == tiles_per_seq - 1) | (i >= n_lat)
    buf_ref[0:_CONV_HALO, :] = prev_ref[...] * jnp.where(first, 0.0, 1.0)
    buf_ref[_CONV_HALO:_CONV_HALO + tm, :] = cur_ref[...]
    buf_ref[_CONV_HALO + tm:, :] = next_ref[...] * jnp.where(last, 0.0, 1.0)
    acc = jnp.zeros(cur_ref.shape, F32) + b_ref[...]
    for j in range(ML_CONV):
        off = _CONV_HALO + j - ML_CONV // 2
        acc = acc + buf_ref[off:off + tm, :] * w_ref[j:j + 1, :]
    o_ref[...] = (_silu(acc) * cs_ref[...]).astype(o_ref.dtype)


def conv_silu(t, w, b, col_scale):
    tm = CTX
    c = t.shape[1]
    hb = tm // _CONV_HALO
    n_halo_blocks = NT // _CONV_HALO
    wp = jnp.zeros((8, c), F32).at[:ML_CONV].set(w)
    return pl.pallas_call(
        _conv_kernel,
        grid=(NT // tm,),
        in_specs=[
            pl.BlockSpec((_CONV_HALO, c), lambda i: (jnp.maximum(i * hb - 1, 0), 0)),
            pl.BlockSpec((tm, c), lambda i: (i, 0)),
            pl.BlockSpec((_CONV_HALO, c), lambda i: (jnp.minimum((i + 1) * hb, n_halo_blocks - 1), 0)),
            pl.BlockSpec((8, c), lambda i: (0, 0)),
            pl.BlockSpec((1, c), lambda i: (0, 0)),
            pl.BlockSpec((1, c), lambda i: (0, 0)),
        ],
        out_specs=pl.BlockSpec((tm, c), lambda i: (i, 0)),
        out_shape=jax.ShapeDtypeStruct((NT, c), BF16),
        scratch_shapes=[pltpu.VMEM((tm + 2 * _CONV_HALO, c), F32)],
        compiler_params=_cp(("parallel",)),
        name="conv_silu",
    )(t, t, t, wp, b.reshape(1, c), col_scale.reshape(1, c))


def _split3(x):
    hi = x.astype(BF16)
    r = x - hi.astype(F32)
    mid = r.astype(BF16)
    lo = (r - mid.astype(F32)).astype(BF16)
    return hi, mid, lo


def _log_sigmoid(x):
    return jnp.minimum(x, 0.0) - jnp.log1p(jnp.exp(-jnp.abs(x)))


def _mlstm_chunk(reverse, q, kt, v_ext, gc, gr, gb_col, gb_row, c_ref, m):
    ln = q.shape[0]
    d = 1 if reverse else 0
    i_col = gc[:, 2 * d:2 * d + 1] + gb_col[:, 2 * d:2 * d + 1]
    f_col = _log_sigmoid(gc[:, 2 * d + 1:2 * d + 2] + gb_col[:, 2 * d + 1:2 * d + 2])
    i_row = gr[2 * d:2 * d + 1, :] + gb_row[2 * d:2 * d + 1, :]
    f_row = _log_sigmoid(gr[2 * d + 1:2 * d + 2, :] + gb_row[2 * d + 1:2 * d + 2, :])
    t_idx = lax.broadcasted_iota(jnp.int32, (ln, ln), 0)
    s_idx = lax.broadcasted_iota(jnp.int32, (ln, ln), 1)
    causal = (s_idx >= t_idx) if reverse else (s_idx <= t_idx)
    tri = jnp.where(causal, 1.0, 0.0).astype(BF16)
    f_col_w = jnp.broadcast_to(f_col, (ln, HEAD_DIM))
    b_col = functools.reduce(lambda a, b: a + b,
                             [jnp.dot(tri, p, preferred_element_type=F32) for p in _split3(f_col_w)])[:, 0:1]
    f_row_w = jnp.broadcast_to(f_row, (16, ln))
    b_row = functools.reduce(lambda a, b: a + b,
                             [lax.dot_general(p, tri, _NT_DIMS, preferred_element_type=F32) for p in _split3(f_row_w)])[0:1, :]
    total = jnp.sum(f_col, axis=0, keepdims=True)

    dmat = jnp.where(causal, b_col - b_row + i_row, -jnp.inf)
    inter = b_col + m
    m_t = jnp.maximum(inter, jnp.max(dmat, axis=-1, keepdims=True))
    w_inter = jnp.exp(inter - m_t)
    s = jnp.dot(q, kt, preferred_element_type=F32) * jnp.exp(dmat - m_t)
    numden = (w_inter * jnp.dot(q, c_ref[...].astype(BF16), preferred_element_type=F32)
              + jnp.dot(s.astype(BF16), v_ext, preferred_element_type=F32))
    den = numden[:, HEAD_DIM:HEAD_DIM + 1]
    h = numden[:, :HEAD_DIM] / jnp.maximum(jnp.abs(den), jnp.exp(-m_t))

    g = total - b_col + i_col
    m_new = jnp.maximum(total + m, jnp.max(g, axis=0, keepdims=True))
    decay = jnp.exp(total + m - m_new)
    wg = jnp.exp(g - m_new)
    upd = jnp.dot(kt, (wg * v_ext.astype(F32)).astype(BF16), preferred_element_type=F32)
    c_ref[...] = decay * c_ref[...] + upd
    return h, m_new


def _mlstm_kernel(q_ref, kt_ref, v_ref, o_ref, qc_ref, ktc_ref, vc_ref, oc_ref, gc_ref, gr_ref, gbc_ref, gbr_ref,
                  hn_ref, out_ref, outc_ref, cf_ref, cb_ref, hf_ref, hb_ref):
    ln = ML_CHUNK
    n_chunks = SEQ // ln
    nh = ML_HEADS_PER_STEP
    ones_col = jnp.where(lax.broadcasted_iota(jnp.int32, (ln, HEAD_DIM), 1) == 0, 1.0, 0.0).astype(BF16)

    def v_ext(v):
        return jnp.concatenate([v, ones_col], axis=1)

    def cols(hd):
        return slice(hd * HEAD_DIM, (hd + 1) * HEAD_DIM)

    def finish(h, o_gate, hn):
        ms = jnp.mean(h * h, axis=-1, keepdims=True)
        return (h * lax.rsqrt(ms + EPS) * hn * jax.nn.sigmoid(o_gate.astype(F32))).astype(BF16)

    def finish_heads(h, o_gate):
        return jnp.concatenate([finish(h[:, cols(hd)], o_gate[:, cols(hd)], hn_ref[:, cols(hd)]) for hd in range(nh)],
                               axis=1)

    cf_ref[...] = jnp.zeros(cf_ref.shape, F32)
    cb_ref[...] = jnp.zeros(cb_ref.shape, F32)
    m0 = jnp.zeros((1, 1), F32)

    ms = []
    for hd in range(nh):
        gc = gc_ref[hd, 0:ln, :]
        gr = gr_ref[hd, :, 0:ln]
        vx = v_ext(vc_ref[:, cols(hd)])
        hf, mf = _mlstm_chunk(False, qc_ref[:, cols(hd)], ktc_ref[cols(hd), :], vx, gc, gr, gbc_ref[hd], gbr_ref[hd],
                              cf_ref.at[hd], m0)
        hb, mb = _mlstm_chunk(True, qc_ref[:, cols(hd)], ktc_ref[cols(hd), :], vx, gc, gr, gbc_ref[hd], gbr_ref[hd],
                              cb_ref.at[hd], m0)
        outc_ref[:, cols(hd)] = finish(hf + hb, oc_ref[:, cols(hd)], hn_ref[:, cols(hd)])
        ms += [mf, mb]

    def body(c, carry):
        carry = list(carry)
        for hd in range(nh):
            for reverse, c_ref, h_ref in ((False, cf_ref, hf_ref), (True, cb_ref, hb_ref)):
                cc = (n_chunks - 1 - c) if reverse else c
                r0 = pl.multiple_of(cc * ln, ln)
                g0 = pl.multiple_of(cc * ln + CTX, ln)
                k = 2 * hd + int(reverse)
                h, carry[k] = _mlstm_chunk(reverse, q_ref[pl.ds(r0, ln), cols(hd)], kt_ref[cols(hd), pl.ds(r0, ln)],
                                           v_ext(v_ref[pl.ds(r0, ln), cols(hd)]), gc_ref[hd, pl.ds(g0, ln), :],
                                           gr_ref[hd, :, pl.ds(g0, ln)], gbc_ref[hd], gbr_ref[hd], c_ref.at[hd],
                                           carry[k])
                h_ref[pl.ds(r0, ln), cols(hd)] = h
        return tuple(carry)

    lax.fori_loop(0, n_chunks, body, tuple(ms), unroll=2)

    def fin_body(c, _):
        r0 = pl.multiple_of(c * ln, ln)
        out_ref[pl.ds(r0, ln), :] = finish_heads(hf_ref[pl.ds(r0, ln), :] + hb_ref[pl.ds(r0, ln), :],
                                                 o_ref[pl.ds(r0, ln), :])
        return 0

    lax.fori_loop(0, n_chunks, fin_body, 0)


def mlstm(qk, kt, vo, gates_col, gates_row, gate_b, hnorm):
    hh = ML_HEADS
    nh = ML_HEADS_PER_STEP
    hs = hh // nh
    w = nh * HEAD_DIM
    ctx_blk = NX // CTX
    tot = CTX + SEQ
    gb = gate_b.astype(F32).transpose(2, 0, 1).reshape(hh, 4)
    gb_col = jnp.zeros((hh, 1, HEAD_DIM), F32).at[:, 0, :4].set(gb)
    gb_row = jnp.zeros((hh, 8, 1), F32).at[:, :4, 0].set(gb)
    lat, ctx = pl.pallas_call(
        _mlstm_kernel,
        grid=(BATCH, hs),
        in_specs=[
            pl.BlockSpec((SEQ, w), lambda b, h: (b, h)),
            pl.BlockSpec((w, SEQ), lambda b, h: (h, b)),
            pl.BlockSpec((SEQ, w), lambda b, h: (b, h)),
            pl.BlockSpec((SEQ, w), lambda b, h: (b, hs + h)),
            pl.BlockSpec((CTX, w), lambda b, h: (ctx_blk + b, h)),
            pl.BlockSpec((w, CTX), lambda b, h: (h, ctx_blk + b)),
            pl.BlockSpec((CTX, w), lambda b, h: (ctx_blk + b, h)),
            pl.BlockSpec((CTX, w), lambda b, h: (ctx_blk + b, hs + h)),
            pl.BlockSpec((nh, tot, HEAD_DIM), lambda b, h: (b * hs + h, 0, 0)),
            pl.BlockSpec((nh, 8, tot), lambda b, h: (b * hs + h, 0, 0)),
            pl.BlockSpec((nh, 1, HEAD_DIM), lambda b, h: (h, 0, 0)),
            pl.BlockSpec((nh, 8, 1), lambda b, h: (h, 0, 0)),
            pl.BlockSpec((1, w), lambda b, h: (0, h)),
        ],
        out_specs=[pl.BlockSpec((SEQ, w), lambda b, h: (b, h)), pl.BlockSpec((CTX, w), lambda b, h: (b, h))],
        out_shape=[jax.ShapeDtypeStruct((NX, ML_WIDTH), BF16), jax.ShapeDtypeStruct((BATCH * CTX, ML_WIDTH), BF16)],
        scratch_shapes=[
            pltpu.VMEM((nh, HEAD_DIM, 2 * HEAD_DIM), F32),
            pltpu.VMEM((nh, HEAD_DIM, 2 * HEAD_DIM), F32),
            pltpu.VMEM((SEQ, w), F32),
            pltpu.VMEM((SEQ, w), F32),
        ],
        compiler_params=_cp(("parallel", "parallel")),
        name="mlstm",
    )(qk, kt, vo, vo, qk, kt, vo, vo, gates_col, gates_row, gb_col, gb_row, hnorm.reshape(1, ML_WIDTH))
    return jnp.concatenate([lat, ctx], axis=0)


def _diff_attn_kernel(q0_ref, q1_ref, k0_ref, k1_ref, k0c_ref, k1c_ref, v_ref, vc_ref, lam_ref, sub_ref, o_ref,
                      *, lambda_init):
    lam = lam_ref[...]
    lam_full = (jnp.exp(jnp.sum(lam[0:1] * lam[1:2], axis=-1, keepdims=True))
                - jnp.exp(jnp.sum(lam[2:3] * lam[3:4], axis=-1, keepdims=True)) + lambda_init)

    tq = q0_ref.shape[0]
    qs = (q0_ref[...], q1_ref[...])

    def step(carry, ks, v):
        out = []
        for (m, l, acc), q, k in zip(carry, qs, ks):
            s = lax.dot_general(q, k, _NT_DIMS, preferred_element_type=F32)
            m_new = jnp.maximum(m, jnp.max(s, axis=-1, keepdims=True))
            alpha = jnp.exp2(m - m_new)
            p = jnp.exp2(s - m_new)
            l = alpha * l + jnp.sum(p, axis=-1, keepdims=True)
            acc = alpha * acc + jnp.dot(p.astype(BF16), v, preferred_element_type=F32)
            out.append((m_new, l, acc))
        return tuple(out)

    init = tuple((jnp.full((tq, 1), -jnp.inf, F32), jnp.zeros((tq, 1), F32), jnp.zeros((tq, DA_VDIM), F32))
                 for _ in range(2))

    carry = init
    for c in range(SEQ // DA_KCHUNK):
        rows = slice(c * DA_KCHUNK, (c + 1) * DA_KCHUNK)
        carry = step(carry, (k0_ref[rows, :], k1_ref[rows, :]), v_ref[rows, :])
    (_, l0, acc0), (_, l1, acc1) = step(carry, (k0c_ref[...], k1c_ref[...]), vc_ref[...])
    o = acc0 / l0 - lam_full * (acc1 / l1)
    ms = jnp.mean(o * o, axis=-1, keepdims=True)
    o_ref[...] = (o * lax.rsqrt(ms + EPS) * sub_ref[...] * (1.0 - lambda_init)).astype(o_ref.dtype)


def diff_attention(qkv, lam, subln, lambda_init, tq=512):
    nq = SEQ // tq
    ctx_blk = NX // CTX
    kcol = 2 * DA_HEADS
    vcol = 2 * DA_HEADS
    return pl.pallas_call(
        functools.partial(_diff_attn_kernel, lambda_init=lambda_init),
        grid=(BATCH, DA_HEADS, nq),
        in_specs=[
            pl.BlockSpec((tq, HEAD_DIM), lambda b, h, i: (b * nq + i, 2 * h)),
            pl.BlockSpec((tq, HEAD_DIM), lambda b, h, i: (b * nq + i, 2 * h + 1)),
            pl.BlockSpec((SEQ, HEAD_DIM), lambda b, h, i: (b, kcol + 2 * h)),
            pl.BlockSpec((SEQ, HEAD_DIM), lambda b, h, i: (b, kcol + 2 * h + 1)),
            pl.BlockSpec((CTX, HEAD_DIM), lambda b, h, i: (ctx_blk + b, kcol + 2 * h)),
            pl.BlockSpec((CTX, HEAD_DIM), lambda b, h, i: (ctx_blk + b, kcol + 2 * h + 1)),
            pl.BlockSpec((SEQ, DA_VDIM), lambda b, h, i: (b, vcol + h)),
            pl.BlockSpec((CTX, DA_VDIM), lambda b, h, i: (ctx_blk + b, vcol + h)),
            pl.BlockSpec((4, HEAD_DIM), lambda b, h, i: (0, 0)),
            pl.BlockSpec((1, DA_VDIM), lambda b, h, i: (0, 0)),
        ],
        out_specs=pl.BlockSpec((tq, DA_VDIM), lambda b, h, i: (b * nq + i, h)),
        out_shape=jax.ShapeDtypeStruct((NX, DA_HEADS * DA_VDIM), BF16),
        compiler_params=_cp(("parallel", "parallel", "arbitrary"), vmem_mb=56),
        name="diff_attn",
    )(qkv, qkv, qkv, qkv, qkv, qkv, qkv, qkv, lam.astype(F32), subln.reshape(1, DA_VDIM).astype(F32))


def dispatch_tables(slot6_t, w6_t, units, n_rows):
    nt = n_rows // TOK_TILE
    max_units = (n_rows * TOP_K + nt * N_EXPERTS * (UNIT - 1)) // UNIT
    n_ffn_tiles = (max_units + N_EXPERTS * (FFN_UNITS - 1)) // FFN_UNITS + 1
    nun = units[:, :, 0].astype(jnp.int32)
    loc_off = jnp.cumsum(nun, axis=1) - nun
    slot6 = slot6_t.transpose(0, 2, 1)
    w6 = w6_t.transpose(0, 2, 1)

    seg_un = nun.sum(axis=0)
    seg_pad = (seg_un + FFN_UNITS - 1) // FFN_UNITS * FFN_UNITS
    seg_end = jnp.cumsum(seg_pad)
    seg_start = seg_end - seg_pad
    gstart = seg_start[None, :] + jnp.cumsum(nun, axis=0) - nun
    u = jnp.arange(UNITS_PER_TILE, dtype=jnp.int32)
    loc_end = loc_off + nun
    ue = (loc_end[:, None, :] <= u[None, :, None]).sum(axis=-1)
    onehot = ue[:, :, None] == jnp.arange(N_EXPERTS)[None, None, :]
    dst = jnp.sum(jnp.where(onehot, (gstart - loc_off)[:, None, :], 0), axis=-1) + u[None, :]
    n_units_total = n_ffn_tiles * FFN_UNITS
    flat_dst = jnp.where(ue < N_EXPERTS, dst, n_units_total).reshape(-1).astype(jnp.int32)
    src_write = jnp.full((n_units_total + 1,), -1, jnp.int32).at[flat_dst].set(
        jnp.arange(nt * UNITS_PER_TILE, dtype=jnp.int32))[:n_units_total]
    src_read = jnp.where(src_write >= 0, src_write, UNITS_PER_TILE - 1)
    chunk_start = jnp.concatenate([seg_start, seg_end[-1:]]).astype(jnp.int32) // FFN_UNITS
    return dict(slot6=slot6, w6=w6, slot6_t=slot6_t, src_read=src_read, src_write=src_write,
                chunk_start=chunk_start, nt=nt)


def _moe_gather_kernel(h_ref, slot_ref, o_ref):
    s_iota = lax.broadcasted_iota(jnp.int32, (SLOTS, TOK_TILE), 0)
    slots = slot_ref[0]
    p = jnp.zeros((SLOTS, TOK_TILE), F32)
    for k in range(TOP_K):
        p = jnp.where(s_iota == slots[k:k + 1, :], 1.0, p)
    p = p.astype(BF16)
    o_ref[0] = jnp.dot(p, h_ref[...], preferred_element_type=F32).astype(BF16)


def moe_gather(h, slot6_t, nt):
    return pl.pallas_call(
        _moe_gather_kernel,
        grid=(nt,),
        in_specs=[pl.BlockSpec((TOK_TILE, D), lambda i: (i, 0)), pl.BlockSpec((1, 8, TOK_TILE), lambda i: (i, 0, 0))],
        out_specs=pl.BlockSpec((1, SLOTS, D), lambda i: (i, 0, 0)),
        out_shape=jax.ShapeDtypeStruct((nt, SLOTS, D), BF16),
        compiler_params=_cp(("parallel",)),
        name="moe_gather",
    )(h, slot6_t)


def _unit_copy(src_hbm, buf_ref, sem_ref, slot, src_unit, j):
    return pltpu.make_async_copy(src_hbm.at[pl.ds(pl.multiple_of(src_unit * UNIT, UNIT), UNIT)],
                                 buf_ref.at[slot, pl.ds(j * UNIT, UNIT)], sem_ref.at[slot])


def _fetch_units(table_ref, base, n_units, src_hbm, buf_ref, sem_ref, slot):
    def body(j, _):
        _unit_copy(src_hbm, buf_ref, sem_ref, slot, table_ref[base + j], j).start()
        return 0

    lax.fori_loop(0, n_units, body, 0, unroll=8)


def _wait_units(n_units, src_hbm, buf_ref, sem_ref, slot):
    pltpu.make_async_copy(src_hbm.at[pl.ds(0, n_units * UNIT)], buf_ref.at[slot], sem_ref.at[slot]).wait()


def _moe_ffn_kernel(srcr_ref, srcw_ref, cs_ref, x_hbm, wg_ref, wu_ref, wd_ref, y_hbm, xbuf_ref, ybuf_ref,
                    sem_in, sem_out, wgb_ref, wub_ref, wdb_ref):
    e = pl.program_id(0)
    lo = cs_ref[e]
    hi = cs_ref[e + 1]
    total = cs_ref[N_EXPERTS]

    def out_units(c, slot, start):
        for j in range(FFN_UNITS):
            su = srcw_ref[c * FFN_UNITS + j]

            @pl.when(su >= 0)
            def _():
                cp = pltpu.make_async_copy(ybuf_ref.at[slot, pl.ds(j * UNIT, UNIT)],
                                           y_hbm.at[pl.ds(pl.multiple_of(su * UNIT, UNIT), UNIT)], sem_out.at[slot])
                if start:
                    cp.start()
                else:
                    cp.wait()

    @pl.when(hi > lo)
    def _():
        wgb_ref[...] = wg_ref[0].astype(BF16)
        wub_ref[...] = wu_ref[0].astype(BF16)
        wdb_ref[...] = wd_ref[0].astype(BF16)

    def fetch(c):
        @pl.when(c < total)
        def _():
            _fetch_units(srcr_ref, c * FFN_UNITS, FFN_UNITS, x_hbm, xbuf_ref, sem_in, c % FFN_IN_DEPTH)

    @pl.when(e == 0)
    def _():
        for c in range(FFN_IN_DEPTH - 1):
            fetch(c)

    def chunk(c, _):
        slot = c % 2
        in_slot = c % FFN_IN_DEPTH
        fetch(c + FFN_IN_DEPTH - 1)
        _wait_units(FFN_UNITS, x_hbm, xbuf_ref, sem_in, in_slot)

        @pl.when(c >= 2)
        def _():
            out_units(c - 2, slot, False)

        x = xbuf_ref[in_slot]
        g = jnp.dot(x, wgb_ref[...], preferred_element_type=F32)
        u = jnp.dot(x, wub_ref[...], preferred_element_type=F32)
        a = (_silu(g) * u).astype(BF16)
        ybuf_ref[slot] = jnp.dot(a, wdb_ref[...], preferred_element_type=F32).astype(BF16)
        out_units(c, slot, True)
        return 0

    lax.fori_loop(lo, hi, chunk, 0)

    @pl.when(e == N_EXPERTS - 1)
    def _():
        for back in (2, 1):
            c = total - back

            @pl.when(c >= 0)
            def _():
                out_units(c, c % 2, False)


def moe_ffn(x_tiles, tabs, wg, wu, wd):
    x_flat = x_tiles.reshape(-1, D)
    idx = lambda e, *_: (e, 0, 0)
    grid_spec = pltpu.PrefetchScalarGridSpec(
        num_scalar_prefetch=3,
        grid=(N_EXPERTS,),
        in_specs=[
            pl.BlockSpec(memory_space=pl.ANY),
            pl.BlockSpec((1, D, D_EXPERT), idx),
            pl.BlockSpec((1, D, D_EXPERT), idx),
            pl.BlockSpec((1, D_EXPERT, D), idx),
        ],
        out_specs=pl.BlockSpec(memory_space=pl.ANY),
        scratch_shapes=[pltpu.VMEM((FFN_IN_DEPTH, FFN_TM, D), BF16), pltpu.VMEM((2, FFN_TM, D), BF16),
                        pltpu.SemaphoreType.DMA((FFN_IN_DEPTH,)), pltpu.SemaphoreType.DMA((2,)),
                        pltpu.VMEM((D, D_EXPERT), BF16), pltpu.VMEM((D, D_EXPERT), BF16),
                        pltpu.VMEM((D_EXPERT, D), BF16)],
    )
    y = pl.pallas_call(
        _moe_ffn_kernel,
        grid_spec=grid_spec,
        out_shape=jax.ShapeDtypeStruct(x_flat.shape, BF16),
        input_output_aliases={3: 0},
        compiler_params=_cp(("arbitrary",), vmem_mb=56),
        name="moe_ffn",
    )(tabs["src_read"], tabs["src_write"], tabs["chunk_start"], x_flat, wg, wu, wd)
    return y.reshape(x_tiles.shape)


def _shared_ffn_kernel(a_ref, wg_ref, wu_ref, wd_ref, o_ref):
    a = a_ref[...]
    g = jnp.dot(a, wg_ref[...], preferred_element_type=F32)
    u = jnp.dot(a, wu_ref[...], preferred_element_type=F32)
    o_ref[...] = jnp.dot((_silu(g) * u).astype(BF16), wd_ref[...], preferred_element_type=F32).astype(o_ref.dtype)


def shared_ffn(h, wg, wu, wd, n_rows, tm=512):
    return pl.pallas_call(
        _shared_ffn_kernel,
        grid=(n_rows // tm,),
        in_specs=[
            pl.BlockSpec((tm, D), lambda i: (i, 0)),
            pl.BlockSpec((D, D_EXPERT), lambda i: (0, 0)),
            pl.BlockSpec((D, D_EXPERT), lambda i: (0, 0)),
            pl.BlockSpec((D_EXPERT, D), lambda i: (0, 0)),
        ],
        out_specs=pl.BlockSpec((tm, D), lambda i: (i, 0)),
        out_shape=jax.ShapeDtypeStruct((n_rows, D), BF16),
        compiler_params=_cp(("parallel",)),
        name="shared_ffn",
    )(h, wg, wu, wd)


def _moe_combine_kernel(y_ref, slot_ref, w_ref, sh_ref, res_ref, gm_ref, o_ref):
    lane = lax.broadcasted_iota(jnp.int32, (TOK_TILE, SLOTS), 1)
    slots = slot_ref[0]
    w = w_ref[0]
    pw = jnp.zeros((TOK_TILE, SLOTS), F32)
    for k in range(TOP_K):
        pw = jnp.where(lane == slots[:, k:k + 1], w[:, k:k + 1], pw)
    routed = jnp.dot(pw.astype(BF16), y_ref[0], preferred_element_type=F32)
    o_ref[...] = res_ref[...] + gm_ref[0] * (routed + sh_ref[...].astype(F32))


def moe_combine(y_tiles, tabs, shared, res, mods, n_rows):
    nt = tabs["nt"]
    tm = TOK_TILE
    return pl.pallas_call(
        _moe_combine_kernel,
        grid=(nt,),
        in_specs=[
            pl.BlockSpec((1, SLOTS, D), lambda i: (i, 0, 0)),
            pl.BlockSpec((1, tm, 8), lambda i: (i, 0, 0)),
            pl.BlockSpec((1, tm, 8), lambda i: (i, 0, 0)),
            pl.BlockSpec((tm, D), lambda i: (i, 0)),
            pl.BlockSpec((tm, D), lambda i: (i, 0)),
            pl.BlockSpec((1, 1, D), lambda i: (_seg_of_tile(i, tm) * 6 + 5, 0, 0)),
        ],
        out_specs=pl.BlockSpec((tm, D), lambda i: (i, 0)),
        out_shape=jax.ShapeDtypeStruct((n_rows, D), F32),
        compiler_params=_cp(("parallel",), vmem_mb=56),
        name="moe_combine",
    )(y_tiles, tabs["slot6"], tabs["w6"], shared, res, mods)


def moe_block(xa, gain, mods, router_w, router_b, wg, wu, wd, sg, su, sd, n_rows):
    h, slot6_t, w6_t, units = norm_route(xa, gain, mods, router_w, router_b, n_rows)
    tabs = dispatch_tables(slot6_t, w6_t, units, n_rows)
    x_tiles = moe_gather(h, tabs["slot6_t"], tabs["nt"])
    y_tiles = moe_ffn(x_tiles, tabs, wg, wu, wd)
    shared = shared_ffn(h, sg.astype(BF16), su.astype(BF16), sd.astype(BF16), n_rows)
    return moe_combine(y_tiles, tabs, shared, xa, mods, n_rows)


def rope_tables():
    t = jnp.arange(SEQ)
    row = (t // GRID_W).astype(F32)
    col = (t % GRID_W).astype(F32)
    n_freq = HEAD_DIM // 4
    inv_freq = ROPE_THETA ** (-jnp.arange(n_freq, dtype=F32) / n_freq)
    ang = jnp.concatenate([row[:, None] * inv_freq, col[:, None] * inv_freq], axis=-1)
    ang = jnp.concatenate([ang, ang], axis=-1)
    sign = jnp.where(jnp.arange(HEAD_DIM) < HEAD_DIM // 2, -1.0, 1.0)
    cos = jnp.concatenate([jnp.cos(ang)] * BATCH + [jnp.ones((BATCH * CTX, HEAD_DIM), F32)], axis=0)
    sin = jnp.concatenate([jnp.sin(ang) * sign] * BATCH + [jnp.zeros((BATCH * CTX, HEAD_DIM), F32)], axis=0)
    return cos, sin


def even_layer(xa, mods, norm1, w_in, na_qnorm, na_knorm, na_rpb, conv_w, conv_b, gate_b, hnorm, w_out):
    h = norm_mod(xa, norm1, mods, 0, NT)
    w = w_in.astype(BF16)
    c0 = 3 * NA_WIDTH
    c1 = c0 + 2 * ML_WIDTH
    c2 = c1 + 2 * ML_WIDTH
    scale = HEAD_DIM ** -0.5
    gain = jnp.concatenate([jnp.tile(na_qnorm.astype(F32) * scale, NA_HEADS), jnp.tile(na_knorm.astype(F32), NA_HEADS),
                            jnp.ones((NA_WIDTH,), F32)]).reshape(1, c0)
    qkv = proj(h, w[:, :c0], BF16, gain=gain, n_norm_cols=2 * NA_WIDTH)
    na_h = neighbourhood_attention(qkv, na_bias_table(na_rpb))

    ml_qk = proj(h, w[:, c0:c1], F32)
    ml_vo = proj(h, w[:, c1:c2], BF16)
    n_gate = w_in.shape[1] - c2
    w_gate = jnp.zeros((D, HEAD_DIM), BF16).at[:, :n_gate].set(w[:, c2:])
    g = proj(h, w_gate, F32, tn=HEAD_DIM)[:, :n_gate]
    col_scale = jnp.concatenate([jnp.ones((ML_WIDTH,), F32), jnp.full((ML_WIDTH,), HEAD_DIM ** -0.5, F32)])
    qk = conv_silu(ml_qk, conv_w.astype(F32), conv_b.astype(F32), col_scale)
    kt = qk[:, ML_WIDTH:].T
    g = g.reshape(NT, 4, ML_HEADS)
    g = jnp.concatenate([g[NX:].reshape(BATCH, CTX, 4, ML_HEADS), g[:NX].reshape(BATCH, SEQ, 4, ML_HEADS)], axis=1)
    g = g.transpose(0, 3, 1, 2).reshape(BATCH * ML_HEADS, CTX + SEQ, 4)
    g_col = jnp.zeros((BATCH * ML_HEADS, CTX + SEQ, HEAD_DIM), F32).at[:, :, :4].set(g)
    g_row = jnp.zeros((BATCH * ML_HEADS, 8, CTX + SEQ), F32).at[:, :4, :].set(g.transpose(0, 2, 1))
    ml_h = mlstm(qk, kt, ml_vo, g_col, g_row, gate_b, hnorm.astype(F32))

    wo = w_out.astype(BF16)
    return out_proj([na_h, ml_h], [wo[:NA_WIDTH], wo[NA_WIDTH:]], xa, mods, 2, NT)


def odd_layer(xa, mods, norm1, w_in, qnorm, knorm, lam, subln, w_out, lambda_init):
    h = norm_mod(xa, norm1, mods, 0, NT)
    scale = HEAD_DIM ** -0.5 * math.log2(math.e)
    n_qk = 2 * DA_HEADS * HEAD_DIM
    gain = jnp.concatenate([jnp.tile(qnorm.astype(F32) * scale, 2 * DA_HEADS), jnp.tile(knorm.astype(F32), 2 * DA_HEADS),
                            jnp.ones((DA_HEADS * DA_VDIM,), F32)]).reshape(1, -1)
    cos, sin = rope_tables()
    qkv = proj(h, w_in.astype(BF16), BF16, gain=gain, n_norm_cols=2 * n_qk, cos=cos, sin=sin)
    o = diff_attention(qkv, lam, subln, lambda_init)
    return out_proj([o], [w_out.astype(BF16)], xa, mods, 2, NX)


def diff_lambda_init(layer):
    return 0.8 - 0.6 * math.exp(-0.3 * layer)


def kernel(x, c, ctx, c_ctx, l0_ada_w, l0_ada_b, l0_norm1, l0_norm2, l0_w_in, l0_na_qnorm, l0_na_knorm, l0_na_rpb, l0_ml_conv_w, l0_ml_conv_b, l0_ml_gate_b, l0_ml_hnorm, l0_w_out, l0_router_w, l0_router_b, l0_exp_gate, l0_exp_up, l0_exp_down, l0_sh_gate, l0_sh_up, l0_sh_down, l1_ada_w, l1_ada_b, l1_norm1, l1_norm2, l1_w_in, l1_qnorm, l1_knorm, l1_lambda, l1_subln, l1_w_out, l1_router_w, l1_router_b, l1_exp_gate, l1_exp_up, l1_exp_down, l1_sh_gate, l1_sh_up, l1_sh_down):
    assert x.shape == (BATCH, SEQ, D) and ctx.shape == (BATCH, CTX, D)
    xa = jnp.concatenate([x.reshape(NX, D), ctx.reshape(BATCH * CTX, D)], axis=0).astype(F32)
    cvec = jnp.zeros((8, D), F32).at[:BATCH].set(c).at[BATCH].set(c_ctx)

    mods0 = adaln(cvec, l0_ada_w, l0_ada_b)
    xa = even_layer(xa, mods0, l0_norm1, l0_w_in, l0_na_qnorm, l0_na_knorm, l0_na_rpb, l0_ml_conv_w, l0_ml_conv_b,
                    l0_ml_gate_b, l0_ml_hnorm, l0_w_out)
    xa = moe_block(xa, l0_norm2, mods0, l0_router_w, l0_router_b, l0_exp_gate, l0_exp_up, l0_exp_down,
                   l0_sh_gate, l0_sh_up, l0_sh_down, NT)

    mods1 = adaln(cvec, l1_ada_w, l1_ada_b)
    xl = odd_layer(xa, mods1, l1_norm1, l1_w_in, l1_qnorm, l1_knorm, l1_lambda, l1_subln, l1_w_out, diff_lambda_init(1))
    xl = moe_block(xl, l1_norm2, mods1, l1_router_w, l1_router_b, l1_exp_gate, l1_exp_up, l1_exp_down,
                   l1_sh_gate, l1_sh_up, l1_sh_down, NX)
    return xl.reshape(BATCH, SEQ, D)
```

```python
import functools
import math

import jax
import jax.numpy as jnp
import numpy as np
from jax import lax
from jax.experimental import pallas as pl
from jax.experimental.pallas import tpu as pltpu

F32 = jnp.float32
BF16 = jnp.bfloat16

D = 2048
BATCH = 2
SEQ = 4096
CTX = 256
NX = BATCH * SEQ
NT = NX + BATCH * CTX
GRID_W = 64
EPS = 1e-6
NEG_INF = -1e30

NA_HEADS = 8
HEAD_DIM = 128
NA_WIDTH = NA_HEADS * HEAD_DIM
NA_WIN_H = 8
NA_WIN_W = 16
NA_QROWS = 4
NA_KROWS = 12
NA_HEADS_PER_STEP = 4
ML_HEADS = 8
ML_WIDTH = ML_HEADS * HEAD_DIM
ML_CONV = 5
ML_CHUNK = 256
ML_HEADS_PER_STEP = 2

DA_HEADS = 8
DA_VDIM = 256
DA_KCHUNK = 1024
ROPE_THETA = 10000.0

N_EXPERTS = 64
N_GROUPS = 8
TOPK_GROUPS = 4
TOP_K = 6
D_EXPERT = 512
ROUTED_SCALE = 2.5

TOK_TILE = 256
UNIT = 16
UNITS_PER_TILE = (TOK_TILE * TOP_K + N_EXPERTS * (UNIT - 1)) // UNIT + 1
UNITS_PER_TILE = -(-UNITS_PER_TILE // 32) * 32
SLOTS = UNITS_PER_TILE * UNIT
FFN_TM = 256
FFN_UNITS = FFN_TM // UNIT
FFN_IN_DEPTH = 6

V7X_VMEM_BYTES = 64 * 1024 * 1024


def _cp(sem, vmem_mb=48):
    assert vmem_mb * 1024 * 1024 < V7X_VMEM_BYTES
    return pltpu.CompilerParams(dimension_semantics=sem, vmem_limit_bytes=vmem_mb * 1024 * 1024)


def _silu(x):
    return x * jax.nn.sigmoid(x)


def _seg_of_tile(i, tm):
    return (i * tm) // SEQ


def _ada_kernel(c_ref, w_ref, b_ref, o_ref):
    s = _silu(c_ref[...]).astype(BF16)
    o_ref[...] = jnp.dot(s, w_ref[...].astype(BF16), preferred_element_type=F32) + b_ref[...]


def adaln(cvec, w, b):
    n = w.shape[1]
    tn = 1024
    out = pl.pallas_call(
        _ada_kernel,
        grid=(n // tn,),
        in_specs=[
            pl.BlockSpec((8, D), lambda j: (0, 0)),
            pl.BlockSpec((D, tn), lambda j: (0, j)),
            pl.BlockSpec((1, tn), lambda j: (0, j)),
        ],
        out_specs=pl.BlockSpec((8, tn), lambda j: (0, j)),
        out_shape=jax.ShapeDtypeStruct((8, n), F32),
        compiler_params=_cp(("arbitrary",)),
        name="adaln",
    )(cvec, w, b.reshape(1, n))
    return out[:3].reshape(18, 1, D)


def _normed(x_ref, g_ref, sh_ref, sc_ref):
    x = x_ref[...]
    ms = jnp.mean(x * x, axis=-1, keepdims=True)
    y = x * lax.rsqrt(ms + EPS) * g_ref[...]
    return y * (1.0 + sc_ref[0]) + sh_ref[0]


def _norm_mod_kernel(x_ref, g_ref, sh_ref, sc_ref, o_ref):
    o_ref[...] = _normed(x_ref, g_ref, sh_ref, sc_ref).astype(o_ref.dtype)


def _mod_spec(which, tm):
    return pl.BlockSpec((1, 1, D), lambda i: (_seg_of_tile(i, tm) * 6 + which, 0, 0))


def norm_mod(x, gain, mods, which_shift, n_rows, tm=512):
    return pl.pallas_call(
        _norm_mod_kernel,
        grid=(n_rows // tm,),
        in_specs=[
            pl.BlockSpec((tm, D), lambda i: (i, 0)),
            pl.BlockSpec((1, D), lambda i: (0, 0)),
            _mod_spec(which_shift, tm),
            _mod_spec(which_shift + 1, tm),
        ],
        out_specs=pl.BlockSpec((tm, D), lambda i: (i, 0)),
        out_shape=jax.ShapeDtypeStruct((n_rows, D), BF16),
        compiler_params=_cp(("parallel",)),
        name="norm_mod",
    )(x, gain.reshape(1, D), mods, mods)


def _route(logits, bias_col):
    tm = logits.shape[1]
    per_group = N_EXPERTS // N_GROUPS
    scores = jax.nn.sigmoid(logits)
    sel = scores + bias_col
    row8 = lax.broadcasted_iota(jnp.int32, (per_group, tm), 0)
    grp = jnp.zeros((N_GROUPS, tm), F32)
    for g in range(N_GROUPS):
        slab = sel[g * per_group:(g + 1) * per_group, :]
        m1 = jnp.max(slab, axis=0, keepdims=True)
        first = jnp.min(jnp.where(slab == m1, row8, per_group), axis=0, keepdims=True)
        m2 = jnp.max(jnp.where(row8 == first, -jnp.inf, slab), axis=0, keepdims=True)
        grp = jnp.where(row8 == g, m1 + m2, grp)
    rank = jnp.zeros((N_GROUPS, tm), jnp.int32)
    for g in range(N_GROUPS):
        vg = grp[g:g + 1, :]
        beats = (vg > grp) | ((vg == grp) & (g < row8))
        rank = rank + jnp.where(beats, 1, 0)
    keep = jnp.where(rank < TOPK_GROUPS, 1.0, 0.0)
    cur = jnp.concatenate(
        [jnp.where(keep[g:g + 1, :] > 0.5, sel[g * per_group:(g + 1) * per_group, :], NEG_INF) for g in range(N_GROUPS)],
        axis=0)
    e_iota = lax.broadcasted_iota(jnp.int32, (N_EXPERTS, tm), 0)
    picked = jnp.zeros((N_EXPERTS, tm), F32)
    for _ in range(TOP_K):
        m = jnp.max(cur, axis=0, keepdims=True)
        idx = jnp.min(jnp.where(cur == m, e_iota, N_EXPERTS), axis=0, keepdims=True)
        hit = e_iota == idx
        picked = jnp.where(hit, 1.0, picked)
        cur = jnp.where(hit, -jnp.inf, cur)
    w = scores * picked
    return w / jnp.sum(w, axis=0, keepdims=True) * ROUTED_SCALE, picked


def _tile_slots(gates, picked):
    tm = gates.shape[1]
    pick = picked.astype(BF16)
    e_r = lax.broadcasted_iota(jnp.int32, (N_EXPERTS, N_EXPERTS), 0)
    e_c = lax.broadcasted_iota(jnp.int32, (N_EXPERTS, N_EXPERTS), 1)
    lower_e = jnp.where(e_c < e_r, 1.0, 0.0).astype(BF16)
    t_r = lax.broadcasted_iota(jnp.int32, (tm, tm), 0)
    t_c = lax.broadcasted_iota(jnp.int32, (tm, tm), 1)
    before_t = jnp.where(t_r < t_c, 1.0, 0.0).astype(BF16)
    kr = jnp.dot(lower_e, pick, preferred_element_type=F32)
    rank = jnp.dot(pick, before_t, preferred_element_type=F32)
    cnt = jnp.sum(picked, axis=1, keepdims=True)
    units = jnp.floor((cnt + (UNIT - 1)) * (1.0 / UNIT))
    units_w = jnp.broadcast_to(units, (N_EXPERTS, HEAD_DIM))
    first_unit = jnp.dot(lower_e, units_w.astype(BF16), preferred_element_type=F32)[:, 0:1]
    slot = first_unit * UNIT + rank
    row8 = lax.broadcasted_iota(jnp.int32, (8, tm), 0)
    slots = jnp.full((8, tm), -1.0, F32)
    weights = jnp.zeros((8, tm), F32)
    for k in range(TOP_K):
        hit = (picked > 0.5) & (kr == k)
        found = jnp.sum(jnp.where(hit, 1.0, 0.0), axis=0, keepdims=True)
        slot_k = jnp.where(found > 0.5, jnp.sum(jnp.where(hit, slot, 0.0), axis=0, keepdims=True), -1.0)
        w_k = jnp.sum(jnp.where(hit, gates, 0.0), axis=0, keepdims=True)
        slots = jnp.where(row8 == k, slot_k, slots)
        weights = jnp.where(row8 == k, w_k, weights)
    return slots.astype(jnp.int32), weights, units_w


def _norm_route_kernel(x_ref, g_ref, sh_ref, sc_ref, rwh_ref, rwl_ref, rb_ref, o_ref, slot_ref, w_ref, units_ref):
    h = _normed(x_ref, g_ref, sh_ref, sc_ref)
    h_hi = h.astype(BF16)
    o_ref[...] = h_hi
    h_lo = (h - h_hi.astype(F32)).astype(BF16)
    nt = (((1,), (1,)), ((), ()))
    logits = (lax.dot_general(rwh_ref[...], h_hi, nt, preferred_element_type=F32)
              + lax.dot_general(rwh_ref[...], h_lo, nt, preferred_element_type=F32)
              + lax.dot_general(rwl_ref[...], h_hi, nt, preferred_element_type=F32))
    gates, picked = _route(logits, rb_ref[...])
    slot_ref[0], w_ref[0], units_ref[0] = _tile_slots(gates, picked)


def norm_route(x, gain, mods, router_w, router_b, n_rows):
    tm = TOK_TILE
    nt = n_rows // tm
    rwt = router_w.T
    rw_hi = rwt.astype(BF16)
    rw_lo = (rwt - rw_hi.astype(F32)).astype(BF16)
    return pl.pallas_call(
        _norm_route_kernel,
        grid=(nt,),
        in_specs=[
            pl.BlockSpec((tm, D), lambda i: (i, 0)),
            pl.BlockSpec((1, D), lambda i: (0, 0)),
            _mod_spec(3, tm),
            _mod_spec(4, tm),
            pl.BlockSpec((N_EXPERTS, D), lambda i: (0, 0)),
            pl.BlockSpec((N_EXPERTS, D), lambda i: (0, 0)),
            pl.BlockSpec((N_EXPERTS, 1), lambda i: (0, 0)),
        ],
        out_specs=[pl.BlockSpec((tm, D), lambda i: (i, 0)), pl.BlockSpec((1, 8, tm), lambda i: (i, 0, 0)),
                   pl.BlockSpec((1, 8, tm), lambda i: (i, 0, 0)),
                   pl.BlockSpec((1, N_EXPERTS, HEAD_DIM), lambda i: (i, 0, 0))],
        out_shape=[jax.ShapeDtypeStruct((n_rows, D), BF16), jax.ShapeDtypeStruct((nt, 8, tm), jnp.int32),
                   jax.ShapeDtypeStruct((nt, 8, tm), F32), jax.ShapeDtypeStruct((nt, N_EXPERTS, HEAD_DIM), F32)],
        compiler_params=_cp(("parallel",)),
        name="norm_route",
    )(x, gain.reshape(1, D), mods, mods, rw_hi, rw_lo, router_b.reshape(N_EXPERTS, 1))


def _head_norm(acc, gain, g):
    a = acc[:, g * HEAD_DIM:(g + 1) * HEAD_DIM]
    ms = jnp.mean(a * a, axis=-1, keepdims=True)
    return a * lax.rsqrt(ms + EPS) * gain[:, g * HEAD_DIM:(g + 1) * HEAD_DIM]


def _proj_kernel(*refs, n_norm_tiles, rope):
    if rope:
        a_ref, w_ref, gain_ref, cos_ref, sin_ref, o_ref = refs
    else:
        a_ref, w_ref, gain_ref, o_ref = refs
    j = pl.program_id(0)
    acc = jnp.dot(a_ref[...], w_ref[...], preferred_element_type=F32)
    tn = acc.shape[1]

    @pl.when(j < n_norm_tiles)
    def _():
        gain = gain_ref[...]
        for g in range(tn // HEAD_DIM):
            y = _head_norm(acc, gain, g)
            if rope:
                y = y * cos_ref[...] + pltpu.roll(y, HEAD_DIM // 2, axis=1) * sin_ref[...]
            o_ref[:, g * HEAD_DIM:(g + 1) * HEAD_DIM] = y.astype(o_ref.dtype)

    @pl.when(j >= n_norm_tiles)
    def _():
        o_ref[...] = acc.astype(o_ref.dtype)


def proj(a, w, out_dtype, *, gain=None, n_norm_cols=0, cos=None, sin=None, tm=512, tn=1024):
    m, k = a.shape
    n = w.shape[1]
    tn = min(tn, n)
    rope = cos is not None
    if gain is None:
        gain = jnp.ones((1, n), F32)
    in_specs = [
        pl.BlockSpec((tm, k), lambda j, i: (i, 0)),
        pl.BlockSpec((k, tn), lambda j, i: (0, j)),
        pl.BlockSpec((1, tn), lambda j, i: (0, j)),
    ]
    args = [a, w, gain]
    if rope:
        in_specs += [pl.BlockSpec((tm, HEAD_DIM), lambda j, i: (i, 0))] * 2
        args += [cos, sin]
    assert n_norm_cols % tn == 0
    return pl.pallas_call(
        functools.partial(_proj_kernel, n_norm_tiles=n_norm_cols // tn, rope=rope),
        grid=(n // tn, m // tm),
        in_specs=in_specs,
        out_specs=pl.BlockSpec((tm, tn), lambda j, i: (i, j)),
        out_shape=jax.ShapeDtypeStruct((m, n), out_dtype),
        compiler_params=_cp(("arbitrary", "arbitrary")),
        name="proj",
    )(*args)


def _out_proj_kernel(*refs, n_a):
    a_refs = refs[:n_a]
    w_refs = refs[n_a:2 * n_a]
    res_ref, gm_ref, o_ref = refs[2 * n_a:]
    acc = jnp.dot(a_refs[0][...], w_refs[0][...], preferred_element_type=F32)
    for a_ref, w_ref in zip(a_refs[1:], w_refs[1:]):
        acc = acc + jnp.dot(a_ref[...], w_ref[...], preferred_element_type=F32)
    o_ref[...] = res_ref[...] + gm_ref[0] * acc


def out_proj(a_list, w_list, res, mods, which_gate, n_rows, tm=512, tn=1024):
    n_a = len(a_list)
    n = w_list[0].shape[1]
    in_specs = [pl.BlockSpec((tm, a.shape[1]), lambda j, i: (i, 0)) for a in a_list]
    in_specs += [pl.BlockSpec((w.shape[0], tn), lambda j, i: (0, j)) for w in w_list]
    in_specs += [
        pl.BlockSpec((tm, tn), lambda j, i: (i, j)),
        pl.BlockSpec((1, 1, tn), lambda j, i: (_seg_of_tile(i, tm) * 6 + which_gate, 0, j)),
    ]
    return pl.pallas_call(
        functools.partial(_out_proj_kernel, n_a=n_a),
        grid=(n // tn, n_rows // tm),
        in_specs=in_specs,
        out_specs=pl.BlockSpec((tm, tn), lambda j, i: (i, j)),
        out_shape=jax.ShapeDtypeStruct((n_rows, n), F32),
        compiler_params=_cp(("arbitrary", "arbitrary")),
        name="out_proj",
    )(*a_list, *w_list, res, mods)


def na_bias_table(rpb):
    rows = SEQ // GRID_W
    n_dc = 2 * NA_WIN_W - 1
    cols = np.arange(GRID_W)
    col_start = np.clip(cols - NA_WIN_W // 2, 0, GRID_W - NA_WIN_W)
    col_ok = (cols[None, :] >= col_start[:, None]) & (cols[None, :] < col_start[:, None] + NA_WIN_W)
    col_idx = np.clip(cols[None, :] - cols[:, None], 1 - NA_WIN_W, NA_WIN_W - 1) + (NA_WIN_W - 1)
    col_pick = (col_idx[None] == np.arange(n_dc)[:, None, None]).astype(np.float32)
    by_col = jnp.einsum("hdk,kqc->hdqc", rpb.astype(F32), col_pick, precision=lax.Precision.HIGHEST)
    by_col = jnp.where(col_ok[None, None], by_col, NEG_INF)
    masked = jnp.full((NA_HEADS, GRID_W, GRID_W), NEG_INF, F32)
    tables = []
    for r0 in (0, 2 * NA_QROWS, rows - NA_QROWS):
        kstart = int(np.clip(r0 - NA_WIN_H // 2, 0, rows - NA_KROWS))
        q_rows = []
        for i in range(NA_QROWS):
            r = r0 + i
            win = int(np.clip(r - NA_WIN_H // 2, 0, rows - NA_WIN_H))
            blocks = []
            for j in range(NA_KROWS):
                kr = kstart + j
                in_window = win <= kr < win + NA_WIN_H
                blocks.append(by_col[:, kr - r + NA_WIN_H - 1] if in_window else masked)
            q_rows.append(jnp.concatenate(blocks, axis=-1))
        tables.append(jnp.concatenate(q_rows, axis=1))
    return jnp.stack(tables, axis=1)


def _softmax_pv(pieces):
    m = functools.reduce(jnp.maximum, [jnp.max(s, axis=-1, keepdims=True) for s, _ in pieces])
    ps = [jnp.exp(s - m) for s, _ in pieces]
    l = functools.reduce(lambda a, b: a + b, [jnp.sum(p, axis=-1, keepdims=True) for p in ps])
    o = functools.reduce(lambda a, b: a + b,
                         [jnp.dot(p.astype(BF16), v, preferred_element_type=F32) for p, (_, v) in zip(ps, pieces)])
    return o / l


_NT_DIMS = (((1,), (1,)), ((), ()))


def _na_kernel(q_ref, k_ref, v_ref, kc_ref, vc_ref, bias_ref, o_ref):
    qb = pl.program_id(2)
    rows = SEQ // GRID_W
    kstart = pl.multiple_of(jnp.clip(qb * NA_QROWS - NA_WIN_H // 2, 0, rows - NA_KROWS) * GRID_W, GRID_W)
    for hd in range(NA_HEADS_PER_STEP):
        cols = slice(hd * HEAD_DIM, (hd + 1) * HEAD_DIM)
        q = q_ref[:, cols]
        kw = k_ref[pl.ds(kstart, NA_KROWS * GRID_W), cols]
        vw = v_ref[pl.ds(kstart, NA_KROWS * GRID_W), cols]
        s_loc = lax.dot_general(q, kw, _NT_DIMS, preferred_element_type=F32) + bias_ref[hd, 0]
        s_ctx = lax.dot_general(q, kc_ref[:, cols], _NT_DIMS, preferred_element_type=F32)
        o_ref[:, cols] = _softmax_pv([(s_loc, vw), (s_ctx, vc_ref[:, cols])]).astype(o_ref.dtype)


def _ctx_attn_kernel(q_ref, k_ref, v_ref, o_ref):
    s = lax.dot_general(q_ref[...], k_ref[...], _NT_DIMS, preferred_element_type=F32)
    o_ref[...] = _softmax_pv([(s, v_ref[...])]).astype(o_ref.dtype)


def neighbourhood_attention(qkv, bias):
    nqb = SEQ // (NA_QROWS * GRID_W)
    tq = NA_QROWS * GRID_W
    ctx_blk = NX // CTX
    nh = NA_HEADS_PER_STEP
    hs = NA_HEADS // nh
    w = nh * HEAD_DIM
    lat = pl.pallas_call(
        _na_kernel,
        grid=(BATCH, hs, nqb),
        in_specs=[
            pl.BlockSpec((tq, w), lambda b, h, i: (b * nqb + i, h)),
            pl.BlockSpec((SEQ, w), lambda b, h, i: (b, hs + h)),
            pl.BlockSpec((SEQ, w), lambda b, h, i: (b, 2 * hs + h)),
            pl.BlockSpec((CTX, w), lambda b, h, i: (ctx_blk + b, hs + h)),
            pl.BlockSpec((CTX, w), lambda b, h, i: (ctx_blk + b, 2 * hs + h)),
            pl.BlockSpec((nh, 1, tq, NA_KROWS * GRID_W),
                         lambda b, h, i: (h, jnp.where(i == 0, 0, jnp.where(i == nqb - 1, 2, 1)), 0, 0)),
        ],
        out_specs=pl.BlockSpec((tq, w), lambda b, h, i: (b * nqb + i, h)),
        out_shape=jax.ShapeDtypeStruct((NX, NA_WIDTH), BF16),
        compiler_params=_cp(("parallel", "parallel", "arbitrary")),
        name="na_attn",
    )(qkv, qkv, qkv, qkv, qkv, bias)
    ctx = pl.pallas_call(
        _ctx_attn_kernel,
        grid=(BATCH, NA_HEADS),
        in_specs=[
            pl.BlockSpec((CTX, HEAD_DIM), lambda b, h: (ctx_blk + b, h)),
            pl.BlockSpec((CTX, HEAD_DIM), lambda b, h: (ctx_blk + b, NA_HEADS + h)),
            pl.BlockSpec((CTX, HEAD_DIM), lambda b, h: (ctx_blk + b, 2 * NA_HEADS + h)),
        ],
        out_specs=pl.BlockSpec((CTX, HEAD_DIM), lambda b, h: (b, h)),
        out_shape=jax.ShapeDtypeStruct((BATCH * CTX, NA_WIDTH), BF16),
        compiler_params=_cp(("parallel", "parallel")),
        name="na_ctx_attn",
    )(qkv, qkv, qkv)
    return jnp.concatenate([lat, ctx], axis=0)


_CONV_HALO = 8


def _conv_kernel(prev_ref, cur_ref, next_ref, w_ref, b_ref, cs_ref, o_ref, buf_ref):
    i = pl.program_id(0)
    tm = cur_ref.shape[0]
    tiles_per_seq = SEQ // tm
    n_lat = NX // tm
    first = (i % tiles_per_seq == 0) | (i >= n_lat)
    last = (i % tiles_per_seq == tiles_per_seq - 1) | (i >= n_lat)
    buf_ref[0:_CONV_HALO, :] = prev_ref[...] * jnp.where(first, 0.0, 1.0)
    buf_ref[_CONV_HALO:_CONV_HALO + tm, :] = cur_ref[...]
    buf_ref[_CONV_HALO + tm:, :] = next_ref[...] * jnp.where(last, 0.0, 1.0)
    acc = jnp.zeros(cur_ref.shape, F32) + b_ref[...]
    for j in range(ML_CONV):
        off = _CONV_HALO + j - ML_CONV // 2
        acc = acc + buf_ref[off:off + tm, :] * w_ref[j:j + 1, :]
    o_ref[...] = (_silu(acc) * cs_ref[...]).astype(o_ref.dtype)


def conv_silu(t, w, b, col_scale):
    tm = CTX
    c = t.shape[1]
    hb = tm // _CONV_HALO
    n_halo_blocks = NT // _CONV_HALO
    wp = jnp.zeros((8, c), F32).at[:ML_CONV].set(w)
    return pl.pallas_call(
        _conv_kernel,
        grid=(NT // tm,),
        in_specs=[
            pl.BlockSpec((_CONV_HALO, c), lambda i: (jnp.maximum(i * hb - 1, 0), 0)),
            pl.BlockSpec((tm, c), lambda i: (i, 0)),
            pl.BlockSpec((_CONV_HALO, c), lambda i: (jnp.minimum((i + 1) * hb, n_halo_blocks - 1), 0)),
            pl.BlockSpec((8, c), lambda i: (0, 0)),
            pl.BlockSpec((1, c), lambda i: (0, 0)),
            pl.BlockSpec((1, c), lambda i: (0, 0)),
        ],
        out_specs=pl.BlockSpec((tm, c), lambda i: (i, 0)),
        out_shape=jax.ShapeDtypeStruct((NT, c), BF16),
        scratch_shapes=[pltpu.VMEM((tm + 2 * _CONV_HALO, c), F32)],
        compiler_params=_cp(("parallel",)),
        name="conv_silu",
    )(t, t, t, wp, b.reshape(1, c), col_scale.reshape(1, c))


def _split3(x):
    hi = x.astype(BF16)
    r = x - hi.astype(F32)
    mid = r.astype(BF16)
    lo = (r - mid.astype(F32)).astype(BF16)
    return hi, mid, lo


def _log_sigmoid(x):
    return jnp.minimum(x, 0.0) - jnp.log1p(jnp.exp(-jnp.abs(x)))


def _mlstm_chunk(reverse, q, kt, v_ext, gc, gr, gb_col, gb_row, c_ref, m):
    ln = q.shape[0]
    d = 1 if reverse else 0
    i_col = gc[:, 2 * d:2 * d + 1] + gb_col[:, 2 * d:2 * d + 1]
    f_col = _log_sigmoid(gc[:, 2 * d + 1:2 * d + 2] + gb_col[:, 2 * d + 1:2 * d + 2])
    i_row = gr[2 * d:2 * d + 1, :] + gb_row[2 * d:2 * d + 1, :]
    f_row = _log_sigmoid(gr[2 * d + 1:2 * d + 2, :] + gb_row[2 * d + 1:2 * d + 2, :])
    t_idx = lax.broadcasted_iota(jnp.int32, (ln, ln), 0)
    s_idx = lax.broadcasted_iota(jnp.int32, (ln, ln), 1)
    causal = (s_idx >= t_idx) if reverse else (s_idx <= t_idx)
    tri = jnp.where(causal, 1.0, 0.0).astype(BF16)
    f_col_w = jnp.broadcast_to(f_col, (ln, HEAD_DIM))
    b_col = functools.reduce(lambda a, b: a + b,
                             [jnp.dot(tri, p, preferred_element_type=F32) for p in _split3(f_col_w)])[:, 0:1]
    f_row_w = jnp.broadcast_to(f_row, (16, ln))
    b_row = functools.reduce(lambda a, b: a + b,
                             [lax.dot_general(p, tri, _NT_DIMS, preferred_element_type=F32) for p in _split3(f_row_w)])[0:1, :]
    total = jnp.sum(f_col, axis=0, keepdims=True)

    dmat = jnp.where(causal, b_col - b_row + i_row, -jnp.inf)
    inter = b_col + m
    m_t = jnp.maximum(inter, jnp.max(dmat, axis=-1, keepdims=True))
    w_inter = jnp.exp(inter - m_t)
    s = jnp.dot(q, kt, preferred_element_type=F32) * jnp.exp(dmat - m_t)
    numden = (w_inter * jnp.dot(q, c_ref[...].astype(BF16), preferred_element_type=F32)
              + jnp.dot(s.astype(BF16), v_ext, preferred_element_type=F32))
    den = numden[:, HEAD_DIM:HEAD_DIM + 1]
    h = numden[:, :HEAD_DIM] / jnp.maximum(jnp.abs(den), jnp.exp(-m_t))

    g = total - b_col + i_col
    m_new = jnp.maximum(total + m, jnp.max(g, axis=0, keepdims=True))
    decay = jnp.exp(total + m - m_new)
    wg = jnp.exp(g - m_new)
    upd = jnp.dot(kt, (wg * v_ext.astype(F32)).astype(BF16), preferred_element_type=F32)
    c_ref[...] = decay * c_ref[...] + upd
    return h, m_new


def _mlstm_kernel(q_ref, kt_ref, v_ref, o_ref, qc_ref, ktc_ref, vc_ref, oc_ref, gc_ref, gr_ref, gbc_ref, gbr_ref,
                  hn_ref, out_ref, outc_ref, cf_ref, cb_ref, hf_ref, hb_ref):
    ln = ML_CHUNK
    n_chunks = SEQ // ln
    nh = ML_HEADS_PER_STEP
    ones_col = jnp.where(lax.broadcasted_iota(jnp.int32, (ln, HEAD_DIM), 1) == 0, 1.0, 0.0).astype(BF16)

    def v_ext(v):
        return jnp.concatenate([v, ones_col], axis=1)

    def cols(hd):
        return slice(hd * HEAD_DIM, (hd + 1) * HEAD_DIM)

    def finish(h, o_gate, hn):
        ms = jnp.mean(h * h, axis=-1, keepdims=True)
        return (h * lax.rsqrt(ms + EPS) * hn * jax.nn.sigmoid(o_gate.astype(F32))).astype(BF16)

    def finish_heads(h, o_gate):
        return jnp.concatenate([finish(h[:, cols(hd)], o_gate[:, cols(hd)], hn_ref[:, cols(hd)]) for hd in range(nh)],
                               axis=1)

    cf_ref[...] = jnp.zeros(cf_ref.shape, F32)
    cb_ref[...] = jnp.zeros(cb_ref.shape, F32)
    m0 = jnp.zeros((1, 1), F32)

    ms = []
    for hd in range(nh):
        gc = gc_ref[hd, 0:ln, :]
        gr = gr_ref[hd, :, 0:ln]
        vx = v_ext(vc_ref[:, cols(hd)])
        hf, mf = _mlstm_chunk(False, qc_ref[:, cols(hd)], ktc_ref[cols(hd), :], vx, gc, gr, gbc_ref[hd], gbr_ref[hd],
                              cf_ref.at[hd], m0)
        hb, mb = _mlstm_chunk(True, qc_ref[:, cols(hd)], ktc_ref[cols(hd), :], vx, gc, gr, gbc_ref[hd], gbr_ref[hd],
                              cb_ref.at[hd], m0)
        outc_ref[:, cols(hd)] = finish(hf + hb, oc_ref[:, cols(hd)], hn_ref[:, cols(hd)])
        ms += [mf, mb]

    def body(c, carry):
        carry = list(carry)
        for hd in range(nh):
            for reverse, c_ref, h_ref in ((False, cf_ref, hf_ref), (True, cb_ref, hb_ref)):
                cc = (n_chunks - 1 - c) if reverse else c
                r0 = pl.multiple_of(cc * ln, ln)
                g0 = pl.multiple_of(cc * ln + CTX, ln)
                k = 2 * hd + int(reverse)
                h, carry[k] = _mlstm_chunk(reverse, q_ref[pl.ds(r0, ln), cols(hd)], kt_ref[cols(hd), pl.ds(r0, ln)],
                                           v_ext(v_ref[pl.ds(r0, ln), cols(hd)]), gc_ref[hd, pl.ds(g0, ln), :],
                                           gr_ref[hd, :, pl.ds(g0, ln)], gbc_ref[hd], gbr_ref[hd], c_ref.at[hd],
                                           carry[k])
                h_ref[pl.ds(r0, ln), cols(hd)] = h
        return tuple(carry)

    lax.fori_loop(0, n_chunks, body, tuple(ms), unroll=2)

    def fin_body(c, _):
        r0 = pl.multiple_of(c * ln, ln)
        out_ref[pl.ds(r0, ln), :] = finish_heads(hf_ref[pl.ds(r0, ln), :] + hb_ref[pl.ds(r0, ln), :],
                                                 o_ref[pl.ds(r0, ln), :])
        return 0

    lax.fori_loop(0, n_chunks, fin_body, 0)


def mlstm(qk, kt, vo, gates_col, gates_row, gate_b, hnorm):
    hh = ML_HEADS
    nh = ML_HEADS_PER_STEP
    hs = hh // nh
    w = nh * HEAD_DIM
    ctx_blk = NX // CTX
    tot = CTX + SEQ
    gb = gate_b.astype(F32).transpose(2, 0, 1).reshape(hh, 4)
    gb_col = jnp.zeros((hh, 1, HEAD_DIM), F32).at[:, 0, :4].set(gb)
    gb_row = jnp.zeros((hh, 8, 1), F32).at[:, :4, 0].set(gb)
    lat, ctx = pl.pallas_call(
        _mlstm_kernel,
        grid=(BATCH, hs),
        in_specs=[
            pl.BlockSpec((SEQ, w), lambda b, h: (b, h)),
            pl.BlockSpec((w, SEQ), lambda b, h: (h, b)),
            pl.BlockSpec((SEQ, w), lambda b, h: (b, h)),
            pl.BlockSpec((SEQ, w), lambda b, h: (b, hs + h)),
            pl.BlockSpec((CTX, w), lambda b, h: (ctx_blk + b, h)),
            pl.BlockSpec((w, CTX), lambda b, h: (h, ctx_blk + b)),
            pl.BlockSpec((CTX, w), lambda b, h: (ctx_blk + b, h)),
            pl.BlockSpec((CTX, w), lambda b, h: (ctx_blk + b, hs + h)),
            pl.BlockSpec((nh, tot, HEAD_DIM), lambda b, h: (b * hs + h, 0, 0)),
            pl.BlockSpec((nh, 8, tot), lambda b, h: (b * hs + h, 0, 0)),
            pl.BlockSpec((nh, 1, HEAD_DIM), lambda b, h: (h, 0, 0)),
            pl.BlockSpec((nh, 8, 1), lambda b, h: (h, 0, 0)),
            pl.BlockSpec((1, w), lambda b, h: (0, h)),
        ],
        out_specs=[pl.BlockSpec((SEQ, w), lambda b, h: (b, h)), pl.BlockSpec((CTX, w), lambda b, h: (b, h))],
        out_shape=[jax.ShapeDtypeStruct((NX, ML_WIDTH), BF16), jax.ShapeDtypeStruct((BATCH * CTX, ML_WIDTH), BF16)],
        scratch_shapes=[
            pltpu.VMEM((nh, HEAD_DIM, 2 * HEAD_DIM), F32),
            pltpu.VMEM((nh, HEAD_DIM, 2 * HEAD_DIM), F32),
            pltpu.VMEM((SEQ, w), F32),
            pltpu.VMEM((SEQ, w), F32),
        ],
        compiler_params=_cp(("parallel", "parallel")),
        name="mlstm",
    )(qk, kt, vo, vo, qk, kt, vo, vo, gates_col, gates_row, gb_col, gb_row, hnorm.reshape(1, ML_WIDTH))
    return jnp.concatenate([lat, ctx], axis=0)


def _diff_attn_kernel(q0_ref, q1_ref, k0_ref, k1_ref, k0c_ref, k1c_ref, v_ref, vc_ref, lam_ref, sub_ref, o_ref,
                      *, lambda_init):
    lam = lam_ref[...]
    lam_full = (jnp.exp(jnp.sum(lam[0:1] * lam[1:2], axis=-1, keepdims=True))
                - jnp.exp(jnp.sum(lam[2:3] * lam[3:4], axis=-1, keepdims=True)) + lambda_init)

    tq = q0_ref.shape[0]
    qs = (q0_ref[...], q1_ref[...])

    def step(carry, ks, v):
        out = []
        for (m, l, acc), q, k in zip(carry, qs, ks):
            s = lax.dot_general(q, k, _NT_DIMS, preferred_element_type=F32)
            m_new = jnp.maximum(m, jnp.max(s, axis=-1, keepdims=True))
            alpha = jnp.exp2(m - m_new)
            p = jnp.exp2(s - m_new)
            l = alpha * l + jnp.sum(p, axis=-1, keepdims=True)
            acc = alpha * acc + jnp.dot(p.astype(BF16), v, preferred_element_type=F32)
            out.append((m_new, l, acc))
        return tuple(out)

    init = tuple((jnp.full((tq, 1), -jnp.inf, F32), jnp.zeros((tq, 1), F32), jnp.zeros((tq, DA_VDIM), F32))
                 for _ in range(2))

    carry = init
    for c in range(SEQ // DA_KCHUNK):
        rows = slice(c * DA_KCHUNK, (c + 1) * DA_KCHUNK)
        carry = step(carry, (k0_ref[rows, :], k1_ref[rows, :]), v_ref[rows, :])
    (_, l0, acc0), (_, l1, acc1) = step(carry, (k0c_ref[...], k1c_ref[...]), vc_ref[...])
    o = acc0 / l0 - lam_full * (acc1 / l1)
    ms = jnp.mean(o * o, axis=-1, keepdims=True)
    o_ref[...] = (o * lax.rsqrt(ms + EPS) * sub_ref[...] * (1.0 - lambda_init)).astype(o_ref.dtype)


def diff_attention(qkv, lam, subln, lambda_init, tq=1024):
    nq = SEQ // tq
    ctx_blk = NX // CTX
    kcol = 2 * DA_HEADS
    vcol = 2 * DA_HEADS
    return pl.pallas_call(
        functools.partial(_diff_attn_kernel, lambda_init=lambda_init),
        grid=(BATCH, DA_HEADS, nq),
        in_specs=[
            pl.BlockSpec((tq, HEAD_DIM), lambda b, h, i: (b * nq + i, 2 * h)),
            pl.BlockSpec((tq, HEAD_DIM), lambda b, h, i: (b * nq + i, 2 * h + 1)),
            pl.BlockSpec((SEQ, HEAD_DIM), lambda b, h, i: (b, kcol + 2 * h)),
            pl.BlockSpec((SEQ, HEAD_DIM), lambda b, h, i: (b, kcol + 2 * h + 1)),
            pl.BlockSpec((CTX, HEAD_DIM), lambda b, h, i: (ctx_blk + b, kcol + 2 * h)),
            pl.BlockSpec((CTX, HEAD_DIM), lambda b, h, i: (ctx_blk + b, kcol + 2 * h + 1)),
            pl.BlockSpec((SEQ, DA_VDIM), lambda b, h, i: (b, vcol + h)),
            pl.BlockSpec((CTX, DA_VDIM), lambda b, h, i: (ctx_blk + b, vcol + h)),
            pl.BlockSpec((4, HEAD_DIM), lambda b, h, i: (0, 0)),
            pl.BlockSpec((1, DA_VDIM), lambda b, h, i: (0, 0)),
        ],
        out_specs=pl.BlockSpec((tq, DA_VDIM), lambda b, h, i: (b * nq + i, h)),
        out_shape=jax.ShapeDtypeStruct((NX, DA_HEADS * DA_VDIM), BF16),
        compiler_params=_cp(("parallel", "parallel", "arbitrary"), vmem_mb=56),
        name="diff_attn",
    )(qkv, qkv, qkv, qkv, qkv, qkv, qkv, qkv, lam.astype(F32), subln.reshape(1, DA_VDIM).astype(F32))


def dispatch_tables(slot6_t, w6_t, units, n_rows):
    nt = n_rows // TOK_TILE
    max_units = (n_rows * TOP_K + nt * N_EXPERTS * (UNIT - 1)) // UNIT
    n_ffn_tiles = (max_units + N_EXPERTS * (FFN_UNITS - 1)) // FFN_UNITS + 1
    nun = units[:, :, 0].astype(jnp.int32)
    loc_off = jnp.cumsum(nun, axis=1) - nun
    slot6 = slot6_t.transpose(0, 2, 1)
    w6 = w6_t.transpose(0, 2, 1)

    seg_un = nun.sum(axis=0)
    seg_pad = (seg_un + FFN_UNITS - 1) // FFN_UNITS * FFN_UNITS
    seg_end = jnp.cumsum(seg_pad)
    seg_start = seg_end - seg_pad
    gstart = seg_start[None, :] + jnp.cumsum(nun, axis=0) - nun
    u = jnp.arange(UNITS_PER_TILE, dtype=jnp.int32)
    loc_end = loc_off + nun
    ue = (loc_end[:, None, :] <= u[None, :, None]).sum(axis=-1)
    onehot = ue[:, :, None] == jnp.arange(N_EXPERTS)[None, None, :]
    dst = jnp.sum(jnp.where(onehot, (gstart - loc_off)[:, None, :], 0), axis=-1) + u[None, :]
    n_units_total = n_ffn_tiles * FFN_UNITS
    flat_dst = jnp.where(ue < N_EXPERTS, dst, n_units_total).reshape(-1).astype(jnp.int32)
    src_write = jnp.full((n_units_total + 1,), -1, jnp.int32).at[flat_dst].set(
        jnp.arange(nt * UNITS_PER_TILE, dtype=jnp.int32))[:n_units_total]
    src_read = jnp.where(src_write >= 0, src_write, UNITS_PER_TILE - 1)
    chunk_start = jnp.concatenate([seg_start, seg_end[-1:]]).astype(jnp.int32) // FFN_UNITS
    return dict(slot6=slot6, w6=w6, slot6_t=slot6_t, src_read=src_read, src_write=src_write,
                chunk_start=chunk_start, nt=nt)


def _moe_gather_kernel(h_ref, slot_ref, o_ref):
    s_iota = lax.broadcasted_iota(jnp.int32, (SLOTS, TOK_TILE), 0)
    slots = slot_ref[0]
    p = jnp.zeros((SLOTS, TOK_TILE), F32)
    for k in range(TOP_K):
        p = jnp.where(s_iota == slots[k:k + 1, :], 1.0, p)
    p = p.astype(BF16)
    o_ref[0] = jnp.dot(p, h_ref[...], preferred_element_type=F32).astype(BF16)


def moe_gather(h, slot6_t, nt):
    return pl.pallas_call(
        _moe_gather_kernel,
        grid=(nt,),
        in_specs=[pl.BlockSpec((TOK_TILE, D), lambda i: (i, 0)), pl.BlockSpec((1, 8, TOK_TILE), lambda i: (i, 0, 0))],
        out_specs=pl.BlockSpec((1, SLOTS, D), lambda i: (i, 0, 0)),
        out_shape=jax.ShapeDtypeStruct((nt, SLOTS, D), BF16),
        compiler_params=_cp(("parallel",)),
        name="moe_gather",
    )(h, slot6_t)


def _unit_copy(src_hbm, buf_ref, sem_ref, slot, src_unit, j):
    return pltpu.make_async_copy(src_hbm.at[pl.ds(pl.multiple_of(src_unit * UNIT, UNIT), UNIT)],
                                 buf_ref.at[slot, pl.ds(j * UNIT, UNIT)], sem_ref.at[slot])


def _fetch_units(table_ref, base, n_units, src_hbm, buf_ref, sem_ref, slot):
    def body(j, _):
        _unit_copy(src_hbm, buf_ref, sem_ref, slot, table_ref[base + j], j).start()
        return 0

    lax.fori_loop(0, n_units, body, 0, unroll=8)


def _wait_units(n_units, src_hbm, buf_ref, sem_ref, slot):
    pltpu.make_async_copy(src_hbm.at[pl.ds(0, n_units * UNIT)], buf_ref.at[slot], sem_ref.at[slot]).wait()


def _moe_ffn_kernel(srcr_ref, srcw_ref, cs_ref, x_hbm, wg_ref, wu_ref, wd_ref, y_hbm, xbuf_ref, ybuf_ref,
                    sem_in, sem_out, wgb_ref, wub_ref, wdb_ref):
    e = pl.program_id(0)
    lo = cs_ref[e]
    hi = cs_ref[e + 1]
    total = cs_ref[N_EXPERTS]

    def out_units(c, slot, start):
        for j in range(FFN_UNITS):
            su = srcw_ref[c * FFN_UNITS + j]

            @pl.when(su >= 0)
            def _():
                cp = pltpu.make_async_copy(ybuf_ref.at[slot, pl.ds(j * UNIT, UNIT)],
                                           y_hbm.at[pl.ds(pl.multiple_of(su * UNIT, UNIT), UNIT)], sem_out.at[slot])
                if start:
                    cp.start()
                else:
                    cp.wait()

    @pl.when(hi > lo)
    def _():
        wgb_ref[...] = wg_ref[0].astype(BF16)
        wub_ref[...] = wu_ref[0].astype(BF16)
        wdb_ref[...] = wd_ref[0].astype(BF16)

    def fetch(c):
        @pl.when(c < total)
        def _():
            _fetch_units(srcr_ref, c * FFN_UNITS, FFN_UNITS, x_hbm, xbuf_ref, sem_in, c % FFN_IN_DEPTH)

    @pl.when(e == 0)
    def _():
        for c in range(FFN_IN_DEPTH - 1):
            fetch(c)

    def chunk(c, _):
        slot = c % 2
        in_slot = c % FFN_IN_DEPTH
        fetch(c + FFN_IN_DEPTH - 1)
        _wait_units(FFN_UNITS, x_hbm, xbuf_ref, sem_in, in_slot)

        @pl.when(c >= 2)
        def _():
            out_units(c - 2, slot, False)

        x = xbuf_ref[in_slot]
        g = jnp.dot(x, wgb_ref[...], preferred_element_type=F32)
        u = jnp.dot(x, wub_ref[...], preferred_element_type=F32)
        a = (_silu(g) * u).astype(BF16)
        ybuf_ref[slot] = jnp.dot(a, wdb_ref[...], preferred_element_type=F32).astype(BF16)
        out_units(c, slot, True)
        return 0

    lax.fori_loop(lo, hi, chunk, 0)

    @pl.when(e == N_EXPERTS - 1)
    def _():
        for back in (2, 1):
            c = total - back

            @pl.when(c >= 0)
            def _():
                out_units(c, c % 2, False)


def moe_ffn(x_tiles, tabs, wg, wu, wd):
    x_flat = x_tiles.reshape(-1, D)
    idx = lambda e, *_: (e, 0, 0)
    grid_spec = pltpu.PrefetchScalarGridSpec(
        num_scalar_prefetch=3,
        grid=(N_EXPERTS,),
        in_specs=[
            pl.BlockSpec(memory_space=pl.ANY),
            pl.BlockSpec((1, D, D_EXPERT), idx),
            pl.BlockSpec((1, D, D_EXPERT), idx),
            pl.BlockSpec((1, D_EXPERT, D), idx),
        ],
        out_specs=pl.BlockSpec(memory_space=pl.ANY),
        scratch_shapes=[pltpu.VMEM((FFN_IN_DEPTH, FFN_TM, D), BF16), pltpu.VMEM((2, FFN_TM, D), BF16),
                        pltpu.SemaphoreType.DMA((FFN_IN_DEPTH,)), pltpu.SemaphoreType.DMA((2,)),
                        pltpu.VMEM((D, D_EXPERT), BF16), pltpu.VMEM((D, D_EXPERT), BF16),
                        pltpu.VMEM((D_EXPERT, D), BF16)],
    )
    y = pl.pallas_call(
        _moe_ffn_kernel,
        grid_spec=grid_spec,
        out_shape=jax.ShapeDtypeStruct(x_flat.shape, BF16),
        input_output_aliases={3: 0},
        compiler_params=_cp(("arbitrary",), vmem_mb=56),
        name="moe_ffn",
    )(tabs["src_read"], tabs["src_write"], tabs["chunk_start"], x_flat, wg, wu, wd)
    return y.reshape(x_tiles.shape)


def _shared_ffn_kernel(a_ref, wg_ref, wu_ref, wd_ref, o_ref):
    a = a_ref[...]
    g = jnp.dot(a, wg_ref[...], preferred_element_type=F32)
    u = jnp.dot(a, wu_ref[...], preferred_element_type=F32)
    o_ref[...] = jnp.dot((_silu(g) * u).astype(BF16), wd_ref[...], preferred_element_type=F32).astype(o_ref.dtype)


def shared_ffn(h, wg, wu, wd, n_rows, tm=512):
    return pl.pallas_call(
        _shared_ffn_kernel,
        grid=(n_rows // tm,),
        in_specs=[
            pl.BlockSpec((tm, D), lambda i: (i, 0)),
            pl.BlockSpec((D, D_EXPERT), lambda i: (0, 0)),
            pl.BlockSpec((D, D_EXPERT), lambda i: (0, 0)),
            pl.BlockSpec((D_EXPERT, D), lambda i: (0, 0)),
        ],
        out_specs=pl.BlockSpec((tm, D), lambda i: (i, 0)),
        out_shape=jax.ShapeDtypeStruct((n_rows, D), BF16),
        compiler_params=_cp(("parallel",)),
        name="shared_ffn",
    )(h, wg, wu, wd)


def _moe_combine_kernel(y_ref, slot_ref, w_ref, sh_ref, res_ref, gm_ref, o_ref):
    lane = lax.broadcasted_iota(jnp.int32, (TOK_TILE, SLOTS), 1)
    slots = slot_ref[0]
    w = w_ref[0]
    pw = jnp.zeros((TOK_TILE, SLOTS), F32)
    for k in range(TOP_K):
        pw = jnp.where(lane == slots[:, k:k + 1], w[:, k:k + 1], pw)
    routed = jnp.dot(pw.astype(BF16), y_ref[0], preferred_element_type=F32)
    o_ref[...] = res_ref[...] + gm_ref[0] * (routed + sh_ref[...].astype(F32))


def moe_combine(y_tiles, tabs, shared, res, mods, n_rows):
    nt = tabs["nt"]
    tm = TOK_TILE
    return pl.pallas_call(
        _moe_combine_kernel,
        grid=(nt,),
        in_specs=[
            pl.BlockSpec((1, SLOTS, D), lambda i: (i, 0, 0)),
            pl.BlockSpec((1, tm, 8), lambda i: (i, 0, 0)),
            pl.BlockSpec((1, tm, 8), lambda i: (i, 0, 0)),
            pl.BlockSpec((tm, D), lambda i: (i, 0)),
            pl.BlockSpec((tm, D), lambda i: (i, 0)),
            pl.BlockSpec((1, 1, D), lambda i: (_seg_of_tile(i, tm) * 6 + 5, 0, 0)),
        ],
        out_specs=pl.BlockSpec((tm, D), lambda i: (i, 0)),
        out_shape=jax.ShapeDtypeStruct((n_rows, D), F32),
        compiler_params=_cp(("parallel",), vmem_mb=56),
        name="moe_combine",
    )(y_tiles, tabs["slot6"], tabs["w6"], shared, res, mods)


def moe_block(xa, gain, mods, router_w, router_b, wg, wu, wd, sg, su, sd, n_rows):
    h, slot6_t, w6_t, units = norm_route(xa, gain, mods, router_w, router_b, n_rows)
    tabs = dispatch_tables(slot6_t, w6_t, units, n_rows)
    x_tiles = moe_gather(h, tabs["slot6_t"], tabs["nt"])
    y_tiles = moe_ffn(x_tiles, tabs, wg, wu, wd)
    shared = shared_ffn(h, sg.astype(BF16), su.astype(BF16), sd.astype(BF16), n_rows)
    return moe_combine(y_tiles, tabs, shared, xa, mods, n_rows)


def rope_tables():
    t = jnp.arange(SEQ)
    row = (t // GRID_W).astype(F32)
    col = (t % GRID_W).astype(F32)
    n_freq = HEAD_DIM // 4
    inv_freq = ROPE_THETA ** (-jnp.arange(n_freq, dtype=F32) / n_freq)
    ang = jnp.concatenate([row[:, None] * inv_freq, col[:, None] * inv_freq], axis=-1)
    ang = jnp.concatenate([ang, ang], axis=-1)
    sign = jnp.where(jnp.arange(HEAD_DIM) < HEAD_DIM // 2, -1.0, 1.0)
    cos = jnp.concatenate([jnp.cos(ang)] * BATCH + [jnp.ones((BATCH * CTX, HEAD_DIM), F32)], axis=0)
    sin = jnp.concatenate([jnp.sin(ang) * sign] * BATCH + [jnp.zeros((BATCH * CTX, HEAD_DIM), F32)], axis=0)
    return cos, sin


def even_layer(xa, mods, norm1, w_in, na_qnorm, na_knorm, na_rpb, conv_w, conv_b, gate_b, hnorm, w_out):
    h = norm_mod(xa, norm1, mods, 0, NT)
    w = w_in.astype(BF16)
    c0 = 3 * NA_WIDTH
    c1 = c0 + 2 * ML_WIDTH
    c2 = c1 + 2 * ML_WIDTH
    scale = HEAD_DIM ** -0.5
    gain = jnp.concatenate([jnp.tile(na_qnorm.astype(F32) * scale, NA_HEADS), jnp.tile(na_knorm.astype(F32), NA_HEADS),
                            jnp.ones((NA_WIDTH,), F32)]).reshape(1, c0)
    qkv = proj(h, w[:, :c0], BF16, gain=gain, n_norm_cols=2 * NA_WIDTH)
    na_h = neighbourhood_attention(qkv, na_bias_table(na_rpb))

    ml_qk = proj(h, w[:, c0:c1], F32)
    ml_vo = proj(h, w[:, c1:c2], BF16)
    n_gate = w_in.shape[1] - c2
    w_gate = jnp.zeros((D, HEAD_DIM), BF16).at[:, :n_gate].set(w[:, c2:])
    g = proj(h, w_gate, F32, tn=HEAD_DIM)[:, :n_gate]
    col_scale = jnp.concatenate([jnp.ones((ML_WIDTH,), F32), jnp.full((ML_WIDTH,), HEAD_DIM ** -0.5, F32)])
    qk = conv_silu(ml_qk, conv_w.astype(F32), conv_b.astype(F32), col_scale)
    kt = qk[:, ML_WIDTH:].T
    g = g.reshape(NT, 4, ML_HEADS)
    g = jnp.concatenate([g[NX:].reshape(BATCH, CTX, 4, ML_HEADS), g[:NX].reshape(BATCH, SEQ, 4, ML_HEADS)], axis=1)
    g = g.transpose(0, 3, 1, 2).reshape(BATCH * ML_HEADS, CTX + SEQ, 4)
    g_col = jnp.zeros((BATCH * ML_HEADS, CTX + SEQ, HEAD_DIM), F32).at[:, :, :4].set(g)
    g_row = jnp.zeros((BATCH * ML_HEADS, 8, CTX + SEQ), F32).at[:, :4, :].set(g.transpose(0, 2, 1))
    ml_h = mlstm(qk, kt, ml_vo, g_col, g_row, gate_b, hnorm.astype(F32))

    wo = w_out.astype(BF16)
    return out_proj([na_h, ml_h], [wo[:NA_WIDTH], wo[NA_WIDTH:]], xa, mods, 2, NT)


def odd_layer(xa, mods, norm1, w_in, qnorm, knorm, lam, subln, w_out, lambda_init):
    h = norm_mod(xa, norm1, mods, 0, NT)
    scale = HEAD_DIM ** -0.5 * math.log2(math.e)
    n_qk = 2 * DA_HEADS * HEAD_DIM
    gain = jnp.concatenate([jnp.tile(qnorm.astype(F32) * scale, 2 * DA_HEADS), jnp.tile(knorm.astype(F32), 2 * DA_HEADS),
                            jnp.ones((DA_HEADS * DA_VDIM,), F32)]).reshape(1, -1)
    cos, sin = rope_tables()
    qkv = proj(h, w_in.astype(BF16), BF16, gain=gain, n_norm_cols=2 * n_qk, cos=cos, sin=sin)
    o = diff_attention(qkv, lam, subln, lambda_init)
    return out_proj([o], [w_out.astype(BF16)], xa, mods, 2, NX)


def diff_lambda_init(layer):
    return 0.8 - 0.6 * math.exp(-0.3 * layer)


def kernel(x, c, ctx, c_ctx, l0_ada_w, l0_ada_b, l0_norm1, l0_norm2, l0_w_in, l0_na_qnorm, l0_na_knorm, l0_na_rpb, l0_ml_conv_w, l0_ml_conv_b, l0_ml_gate_b, l0_ml_hnorm, l0_w_out, l0_router_w, l0_router_b, l0_exp_gate, l0_exp_up, l0_exp_down, l0_sh_gate, l0_sh_up, l0_sh_down, l1_ada_w, l1_ada_b, l1_norm1, l1_norm2, l1_w_in, l1_qnorm, l1_knorm, l1_lambda, l1_subln, l1_w_out, l1_router_w, l1_router_b, l1_exp_gate, l1_exp_up, l1_exp_down, l1_sh_gate, l1_sh_up, l1_sh_down):
    assert x.shape == (BATCH, SEQ, D) and ctx.shape == (BATCH, CTX, D)
    xa = jnp.concatenate([x.reshape(NX, D), ctx.reshape(BATCH * CTX, D)], axis=0).astype(F32)
    cvec = jnp.zeros((8, D), F32).at[:BATCH].set(c).at[BATCH].set(c_ctx)

    mods0 = adaln(cvec, l0_ada_w, l0_ada_b)
    xa = even_layer(xa, mods0, l0_norm1, l0_w_in, l0_na_qnorm, l0_na_knorm, l0_na_rpb, l0_ml_conv_w, l0_ml_conv_b,
                    l0_ml_gate_b, l0_ml_hnorm, l0_w_out)
    xa = moe_block(xa, l0_norm2, mods0, l0_router_w, l0_router_b, l0_exp_gate, l0_exp_up, l0_exp_down,
                   l0_sh_gate, l0_sh_up, l0_sh_down, NT)

    mods1 = adaln(cvec, l1_ada_w, l1_ada_b)
    xl = odd_layer(xa, mods1, l1_norm1, l1_w_in, l1_qnorm, l1_knorm, l1_lambda, l1_subln, l1_w_out, diff_lambda_init(1))
    xl = moe_block(xl, l1_norm2, mods1, l1_router_w, l1_router_b, l1_exp_gate, l1_exp_up, l1_exp_down,
                   l1_sh_gate, l1_sh_up, l1_sh_down, NX)
    return xl.reshape(BATCH, SEQ, D)
```

```python
import functools
import math

import jax
import jax.numpy as jnp
import numpy as np
from jax import lax
from jax.experimental import pallas as pl
from jax.experimental.pallas import tpu as pltpu

F32 = jnp.float32
BF16 = jnp.bfloat16

D = 2048
BATCH = 2
SEQ = 4096
CTX = 256
NX = BATCH * SEQ
NT = NX + BATCH * CTX
GRID_W = 64
EPS = 1e-6
NEG_INF = -1e30

NA_HEADS = 8
HEAD_DIM = 128
NA_WIDTH = NA_HEADS * HEAD_DIM
NA_WIN_H = 8
NA_WIN_W = 16
NA_QROWS = 4
NA_KROWS = 12
NA_HEADS_PER_STEP = 4
ML_HEADS = 8
ML_WIDTH = ML_HEADS * HEAD_DIM
ML_CONV = 5
ML_CHUNK = 256
ML_HEADS_PER_STEP = 2

DA_HEADS = 8
DA_VDIM = 256
DA_KCHUNK = 1024
ROPE_THETA = 10000.0

N_EXPERTS = 64
N_GROUPS = 8
TOPK_GROUPS = 4
TOP_K = 6
D_EXPERT = 512
ROUTED_SCALE = 2.5

TOK_TILE = 256
UNIT = 16
UNITS_PER_TILE = (TOK_TILE * TOP_K + N_EXPERTS * (UNIT - 1)) // UNIT + 1
UNITS_PER_TILE = -(-UNITS_PER_TILE // 32) * 32
SLOTS = UNITS_PER_TILE * UNIT
FFN_TM = 256
FFN_UNITS = FFN_TM // UNIT
FFN_IN_DEPTH = 6

V7X_VMEM_BYTES = 64 * 1024 * 1024


def _cp(sem, vmem_mb=48):
    assert vmem_mb * 1024 * 1024 < V7X_VMEM_BYTES
    return pltpu.CompilerParams(dimension_semantics=sem, vmem_limit_bytes=vmem_mb * 1024 * 1024)


def _silu(x):
    return x * jax.nn.sigmoid(x)


def _seg_of_tile(i, tm):
    return (i * tm) // SEQ


def _ada_kernel(c_ref, w_ref, b_ref, o_ref):
    s = _silu(c_ref[...]).astype(BF16)
    o_ref[...] = jnp.dot(s, w_ref[...].astype(BF16), preferred_element_type=F32) + b_ref[...]


def adaln(cvec, w, b):
    n = w.shape[1]
    tn = 1024
    out = pl.pallas_call(
        _ada_kernel,
        grid=(n // tn,),
        in_specs=[
            pl.BlockSpec((8, D), lambda j: (0, 0)),
            pl.BlockSpec((D, tn), lambda j: (0, j)),
            pl.BlockSpec((1, tn), lambda j: (0, j)),
        ],
        out_specs=pl.BlockSpec((8, tn), lambda j: (0, j)),
        out_shape=jax.ShapeDtypeStruct((8, n), F32),
        compiler_params=_cp(("arbitrary",)),
        name="adaln",
    )(cvec, w, b.reshape(1, n))
    return out[:3].reshape(18, 1, D)


def _normed(x_ref, g_ref, sh_ref, sc_ref):
    x = x_ref[...]
    ms = jnp.mean(x * x, axis=-1, keepdims=True)
    y = x * lax.rsqrt(ms + EPS) * g_ref[...]
    return y * (1.0 + sc_ref[0]) + sh_ref[0]


def _norm_mod_kernel(x_ref, g_ref, sh_ref, sc_ref, o_ref):
    o_ref[...] = _normed(x_ref, g_ref, sh_ref, sc_ref).astype(o_ref.dtype)


def _mod_spec(which, tm):
    return pl.BlockSpec((1, 1, D), lambda i: (_seg_of_tile(i, tm) * 6 + which, 0, 0))


def norm_mod(x, gain, mods, which_shift, n_rows, tm=512):
    return pl.pallas_call(
        _norm_mod_kernel,
        grid=(n_rows // tm,),
        in_specs=[
            pl.BlockSpec((tm, D), lambda i: (i, 0)),
            pl.BlockSpec((1, D), lambda i: (0, 0)),
            _mod_spec(which_shift, tm),
            _mod_spec(which_shift + 1, tm),
        ],
        out_specs=pl.BlockSpec((tm, D), lambda i: (i, 0)),
        out_shape=jax.ShapeDtypeStruct((n_rows, D), BF16),
        compiler_params=_cp(("parallel",)),
        name="norm_mod",
    )(x, gain.reshape(1, D), mods, mods)


def _route(logits, bias_col):
    tm = logits.shape[1]
    per_group = N_EXPERTS // N_GROUPS
    scores = jax.nn.sigmoid(logits)
    sel = scores + bias_col
    row8 = lax.broadcasted_iota(jnp.int32, (per_group, tm), 0)
    grp = jnp.zeros((N_GROUPS, tm), F32)
    for g in range(N_GROUPS):
        slab = sel[g * per_group:(g + 1) * per_group, :]
        m1 = jnp.max(slab, axis=0, keepdims=True)
        first = jnp.min(jnp.where(slab == m1, row8, per_group), axis=0, keepdims=True)
        m2 = jnp.max(jnp.where(row8 == first, -jnp.inf, slab), axis=0, keepdims=True)
        grp = jnp.where(row8 == g, m1 + m2, grp)
    rank = jnp.zeros((N_GROUPS, tm), jnp.int32)
    for g in range(N_GROUPS):
        vg = grp[g:g + 1, :]
        beats = (vg > grp) | ((vg == grp) & (g < row8))
        rank = rank + jnp.where(beats, 1, 0)
    keep = jnp.where(rank < TOPK_GROUPS, 1.0, 0.0)
    cur = jnp.concatenate(
        [jnp.where(keep[g:g + 1, :] > 0.5, sel[g * per_group:(g + 1) * per_group, :], NEG_INF) for g in range(N_GROUPS)],
        axis=0)
    e_iota = lax.broadcasted_iota(jnp.int32, (N_EXPERTS, tm), 0)
    picked = jnp.zeros((N_EXPERTS, tm), F32)
    for _ in range(TOP_K):
        m = jnp.max(cur, axis=0, keepdims=True)
        idx = jnp.min(jnp.where(cur == m, e_iota, N_EXPERTS), axis=0, keepdims=True)
        hit = e_iota == idx
        picked = jnp.where(hit, 1.0, picked)
        cur = jnp.where(hit, -jnp.inf, cur)
    w = scores * picked
    return w / jnp.sum(w, axis=0, keepdims=True) * ROUTED_SCALE, picked


def _tile_slots(gates, picked):
    tm = gates.shape[1]
    pick = picked.astype(BF16)
    e_r = lax.broadcasted_iota(jnp.int32, (N_EXPERTS, N_EXPERTS), 0)
    e_c = lax.broadcasted_iota(jnp.int32, (N_EXPERTS, N_EXPERTS), 1)
    lower_e = jnp.where(e_c < e_r, 1.0, 0.0).astype(BF16)
    t_r = lax.broadcasted_iota(jnp.int32, (tm, tm), 0)
    t_c = lax.broadcasted_iota(jnp.int32, (tm, tm), 1)
    before_t = jnp.where(t_r < t_c, 1.0, 0.0).astype(BF16)
    kr = jnp.dot(lower_e, pick, preferred_element_type=F32)
    rank = jnp.dot(pick, before_t, preferred_element_type=F32)
    cnt = jnp.sum(picked, axis=1, keepdims=True)
    units = jnp.floor((cnt + (UNIT - 1)) * (1.0 / UNIT))
    units_w = jnp.broadcast_to(units, (N_EXPERTS, HEAD_DIM))
    first_unit = jnp.dot(lower_e, units_w.astype(BF16), preferred_element_type=F32)[:, 0:1]
    slot = first_unit * UNIT + rank
    row8 = lax.broadcasted_iota(jnp.int32, (8, tm), 0)
    slots = jnp.full((8, tm), -1.0, F32)
    weights = jnp.zeros((8, tm), F32)
    for k in range(TOP_K):
        hit = (picked > 0.5) & (kr == k)
        found = jnp.sum(jnp.where(hit, 1.0, 0.0), axis=0, keepdims=True)
        slot_k = jnp.where(found > 0.5, jnp.sum(jnp.where(hit, slot, 0.0), axis=0, keepdims=True), -1.0)
        w_k = jnp.sum(jnp.where(hit, gates, 0.0), axis=0, keepdims=True)
        slots = jnp.where(row8 == k, slot_k, slots)
        weights = jnp.where(row8 == k, w_k, weights)
    return slots.astype(jnp.int32), weights, units_w


def _norm_route_kernel(x_ref, g_ref, sh_ref, sc_ref, rwh_ref, rwl_ref, rb_ref, o_ref, slot_ref, w_ref, units_ref):
    h = _normed(x_ref, g_ref, sh_ref, sc_ref)
    h_hi = h.astype(BF16)
    o_ref[...] = h_hi
    h_lo = (h - h_hi.astype(F32)).astype(BF16)
    nt = (((1,), (1,)), ((), ()))
    logits = (lax.dot_general(rwh_ref[...], h_hi, nt, preferred_element_type=F32)
              + lax.dot_general(rwh_ref[...], h_lo, nt, preferred_element_type=F32)
              + lax.dot_general(rwl_ref[...], h_hi, nt, preferred_element_type=F32))
    gates, picked = _route(logits, rb_ref[...])
    slot_ref[0], w_ref[0], units_ref[0] = _tile_slots(gates, picked)


def norm_route(x, gain, mods, router_w, router_b, n_rows):
    tm = TOK_TILE
    nt = n_rows // tm
    rwt = router_w.T
    rw_hi = rwt.astype(BF16)
    rw_lo = (rwt - rw_hi.astype(F32)).astype(BF16)
    return pl.pallas_call(
        _norm_route_kernel,
        grid=(nt,),
        in_specs=[
            pl.BlockSpec((tm, D), lambda i: (i, 0)),
            pl.BlockSpec((1, D), lambda i: (0, 0)),
            _mod_spec(3, tm),
            _mod_spec(4, tm),
            pl.BlockSpec((N_EXPERTS, D), lambda i: (0, 0)),
            pl.BlockSpec((N_EXPERTS, D), lambda i: (0, 0)),
            pl.BlockSpec((N_EXPERTS, 1), lambda i: (0, 0)),
        ],
        out_specs=[pl.BlockSpec((tm, D), lambda i: (i, 0)), pl.BlockSpec((1, 8, tm), lambda i: (i, 0, 0)),
                   pl.BlockSpec((1, 8, tm), lambda i: (i, 0, 0)),
                   pl.BlockSpec((1, N_EXPERTS, HEAD_DIM), lambda i: (i, 0, 0))],
        out_shape=[jax.ShapeDtypeStruct((n_rows, D), BF16), jax.ShapeDtypeStruct((nt, 8, tm), jnp.int32),
                   jax.ShapeDtypeStruct((nt, 8, tm), F32), jax.ShapeDtypeStruct((nt, N_EXPERTS, HEAD_DIM), F32)],
        compiler_params=_cp(("parallel",)),
        name="norm_route",
    )(x, gain.reshape(1, D), mods, mods, rw_hi, rw_lo, router_b.reshape(N_EXPERTS, 1))


def _head_norm(acc, gain, g):
    a = acc[:, g * HEAD_DIM:(g + 1) * HEAD_DIM]
    ms = jnp.mean(a * a, axis=-1, keepdims=True)
    return a * lax.rsqrt(ms + EPS) * gain[:, g * HEAD_DIM:(g + 1) * HEAD_DIM]


def _proj_kernel(*refs, n_norm_tiles, rope):
    if rope:
        a_ref, w_ref, gain_ref, cos_ref, sin_ref, o_ref = refs
    else:
        a_ref, w_ref, gain_ref, o_ref = refs
    j = pl.program_id(0)
    acc = jnp.dot(a_ref[...], w_ref[...], preferred_element_type=F32)
    tn = acc.shape[1]

    @pl.when(j < n_norm_tiles)
    def _():
        gain = gain_ref[...]
        for g in range(tn // HEAD_DIM):
            y = _head_norm(acc, gain, g)
            if rope:
                y = y * cos_ref[...] + pltpu.roll(y, HEAD_DIM // 2, axis=1) * sin_ref[...]
            o_ref[:, g * HEAD_DIM:(g + 1) * HEAD_DIM] = y.astype(o_ref.dtype)

    @pl.when(j >= n_norm_tiles)
    def _():
        o_ref[...] = acc.astype(o_ref.dtype)


def proj(a, w, out_dtype, *, gain=None, n_norm_cols=0, cos=None, sin=None, tm=512, tn=1024):
    m, k = a.shape
    n = w.shape[1]
    tn = min(tn, n)
    rope = cos is not None
    if gain is None:
        gain = jnp.ones((1, n), F32)
    in_specs = [
        pl.BlockSpec((tm, k), lambda j, i: (i, 0)),
        pl.BlockSpec((k, tn), lambda j, i: (0, j)),
        pl.BlockSpec((1, tn), lambda j, i: (0, j)),
    ]
    args = [a, w, gain]
    if rope:
        in_specs += [pl.BlockSpec((tm, HEAD_DIM), lambda j, i: (i, 0))] * 2
        args += [cos, sin]
    assert n_norm_cols % tn == 0
    return pl.pallas_call(
        functools.partial(_proj_kernel, n_norm_tiles=n_norm_cols // tn, rope=rope),
        grid=(n // tn, m // tm),
        in_specs=in_specs,
        out_specs=pl.BlockSpec((tm, tn), lambda j, i: (i, j)),
        out_shape=jax.ShapeDtypeStruct((m, n), out_dtype),
        compiler_params=_cp(("arbitrary", "arbitrary")),
        name="proj",
    )(*args)


def _out_proj_kernel(*refs, n_a):
    a_refs = refs[:n_a]
    w_refs = refs[n_a:2 * n_a]
    res_ref, gm_ref, o_ref = refs[2 * n_a:]
    acc = jnp.dot(a_refs[0][...], w_refs[0][...], preferred_element_type=F32)
    for a_ref, w_ref in zip(a_refs[1:], w_refs[1:]):
        acc = acc + jnp.dot(a_ref[...], w_ref[...], preferred_element_type=F32)
    o_ref[...] = res_ref[...] + gm_ref[0] * acc


def out_proj(a_list, w_list, res, mods, which_gate, n_rows, tm=512, tn=1024):
    n_a = len(a_list)
    n = w_list[0].shape[1]
    in_specs = [pl.BlockSpec((tm, a.shape[1]), lambda j, i: (i, 0)) for a in a_list]
    in_specs += [pl.BlockSpec((w.shape[0], tn), lambda j, i: (0, j)) for w in w_list]
    in_specs += [
        pl.BlockSpec((tm, tn), lambda j, i: (i, j)),
        pl.BlockSpec((1, 1, tn), lambda j, i: (_seg_of_tile(i, tm) * 6 + which_gate, 0, j)),
    ]
    return pl.pallas_call(
        functools.partial(_out_proj_kernel, n_a=n_a),
        grid=(n // tn, n_rows // tm),
        in_specs=in_specs,
        out_specs=pl.BlockSpec((tm, tn), lambda j, i: (i, j)),
        out_shape=jax.ShapeDtypeStruct((n_rows, n), F32),
        compiler_params=_cp(("arbitrary", "arbitrary")),
        name="out_proj",
    )(*a_list, *w_list, res, mods)


def na_bias_table(rpb):
    rows = SEQ // GRID_W
    n_dc = 2 * NA_WIN_W - 1
    cols = np.arange(GRID_W)
    col_start = np.clip(cols - NA_WIN_W // 2, 0, GRID_W - NA_WIN_W)
    col_ok = (cols[None, :] >= col_start[:, None]) & (cols[None, :] < col_start[:, None] + NA_WIN_W)
    col_idx = np.clip(cols[None, :] - cols[:, None], 1 - NA_WIN_W, NA_WIN_W - 1) + (NA_WIN_W - 1)
    col_pick = (col_idx[None] == np.arange(n_dc)[:, None, None]).astype(np.float32)
    by_col = jnp.einsum("hdk,kqc->hdqc", rpb.astype(F32), col_pick, precision=lax.Precision.HIGHEST)
    by_col = jnp.where(col_ok[None, None], by_col, NEG_INF)
    masked = jnp.full((NA_HEADS, GRID_W, GRID_W), NEG_INF, F32)
    tables = []
    for r0 in (0, 2 * NA_QROWS, rows - NA_QROWS):
        kstart = int(np.clip(r0 - NA_WIN_H // 2, 0, rows - NA_KROWS))
        q_rows = []
        for i in range(NA_QROWS):
            r = r0 + i
            win = int(np.clip(r - NA_WIN_H // 2, 0, rows - NA_WIN_H))
            blocks = []
            for j in range(NA_KROWS):
                kr = kstart + j
                in_window = win <= kr < win + NA_WIN_H
                blocks.append(by_col[:, kr - r + NA_WIN_H - 1] if in_window else masked)
            q_rows.append(jnp.concatenate(blocks, axis=-1))
        tables.append(jnp.concatenate(q_rows, axis=1))
    return jnp.stack(tables, axis=1)


def _softmax_pv(pieces):
    m = functools.reduce(jnp.maximum, [jnp.max(s, axis=-1, keepdims=True) for s, _ in pieces])
    ps = [jnp.exp(s - m) for s, _ in pieces]
    l = functools.reduce(lambda a, b: a + b, [jnp.sum(p, axis=-1, keepdims=True) for p in ps])
    o = functools.reduce(lambda a, b: a + b,
                         [jnp.dot(p.astype(BF16), v, preferred_element_type=F32) for p, (_, v) in zip(ps, pieces)])
    return o / l


_NT_DIMS = (((1,), (1,)), ((), ()))


def _na_kernel(q_ref, k_ref, v_ref, kc_ref, vc_ref, bias_ref, o_ref):
    qb = pl.program_id(2)
    rows = SEQ // GRID_W
    kstart = pl.multiple_of(jnp.clip(qb * NA_QROWS - NA_WIN_H // 2, 0, rows - NA_KROWS) * GRID_W, GRID_W)
    for hd in range(NA_HEADS_PER_STEP):
        cols = slice(hd * HEAD_DIM, (hd + 1) * HEAD_DIM)
        q = q_ref[:, cols]
        kw = k_ref[pl.ds(kstart, NA_KROWS * GRID_W), cols]
        vw = v_ref[pl.ds(kstart, NA_KROWS * GRID_W), cols]
        s_loc = lax.dot_general(q, kw, _NT_DIMS, preferred_element_type=F32) + bias_ref[hd, 0]
        s_ctx = lax.dot_general(q, kc_ref[:, cols], _NT_DIMS, preferred_element_type=F32)
        o_ref[:, cols] = _softmax_pv([(s_loc, vw), (s_ctx, vc_ref[:, cols])]).astype(o_ref.dtype)


def _ctx_attn_kernel(q_ref, k_ref, v_ref, o_ref):
    s = lax.dot_general(q_ref[...], k_ref[...], _NT_DIMS, preferred_element_type=F32)
    o_ref[...] = _softmax_pv([(s, v_ref[...])]).astype(o_ref.dtype)


def neighbourhood_attention(qkv, bias):
    nqb = SEQ // (NA_QROWS * GRID_W)
    tq = NA_QROWS * GRID_W
    ctx_blk = NX // CTX
    nh = NA_HEADS_PER_STEP
    hs = NA_HEADS // nh
    w = nh * HEAD_DIM
    lat = pl.pallas_call(
        _na_kernel,
        grid=(BATCH, hs, nqb),
        in_specs=[
            pl.BlockSpec((tq, w), lambda b, h, i: (b * nqb + i, h)),
            pl.BlockSpec((SEQ, w), lambda b, h, i: (b, hs + h)),
            pl.BlockSpec((SEQ, w), lambda b, h, i: (b, 2 * hs + h)),
            pl.BlockSpec((CTX, w), lambda b, h, i: (ctx_blk + b, hs + h)),
            pl.BlockSpec((CTX, w), lambda b, h, i: (ctx_blk + b, 2 * hs + h)),
            pl.BlockSpec((nh, 1, tq, NA_KROWS * GRID_W),
                         lambda b, h, i: (h, jnp.where(i == 0, 0, jnp.where(i == nqb - 1, 2, 1)), 0, 0)),
        ],
        out_specs=pl.BlockSpec((tq, w), lambda b, h, i: (b * nqb + i, h)),
        out_shape=jax.ShapeDtypeStruct((NX, NA_WIDTH), BF16),
        compiler_params=_cp(("parallel", "parallel", "arbitrary")),
        name="na_attn",
    )(qkv, qkv, qkv, qkv, qkv, bias)
    ctx = pl.pallas_call(
        _ctx_attn_kernel,
        grid=(BATCH, NA_HEADS),
        in_specs=[
            pl.BlockSpec((CTX, HEAD_DIM), lambda b, h: (ctx_blk + b, h)),
            pl.BlockSpec((CTX, HEAD_DIM), lambda b, h: (ctx_blk + b, NA_HEADS + h)),
            pl.BlockSpec((CTX, HEAD_DIM), lambda b, h: (ctx_blk + b, 2 * NA_HEADS + h)),
        ],
        out_specs=pl.BlockSpec((CTX, HEAD_DIM), lambda b, h: (b, h)),
        out_shape=jax.ShapeDtypeStruct((BATCH * CTX, NA_WIDTH), BF16),
        compiler_params=_cp(("parallel", "parallel")),
        name="na_ctx_attn",
    )(qkv, qkv, qkv)
    return jnp.concatenate([lat, ctx], axis=0)


_CONV_HALO = 8


def _conv_kernel(prev_ref, cur_ref, next_ref, w_ref, b_ref, cs_ref, o_ref, buf_ref):
    i = pl.program_id(0)
    tm = cur_ref.shape[0]
    tiles_per_seq = SEQ // tm
    n_lat = NX // tm
    first = (i % tiles_per_seq == 0) | (i >= n_lat)
    last = (i % tiles_per_seq == tiles_per_seq - 1) | (i >= n_lat)
    buf_ref[0:_CONV_HALO, :] = prev_ref[...] * jnp.where(first, 0.0, 1.0)
    buf_ref[_CONV_HALO:_CONV_HALO + tm, :] = cur_ref[...]
    buf_ref[_CONV_HALO + tm:, :] = next_ref[...] * jnp.where(last, 0.0, 1.0)
    acc = jnp.zeros(cur_ref.shape, F32) + b_ref[...]
    for j in range(ML_CONV):
        off = _CONV_HALO + j - ML_CONV // 2
        acc = acc + buf_ref[off:off + tm, :] * w_ref[j:j + 1, :]
    o_ref[...] = (_silu(acc) * cs_ref[...]).astype(o_ref.dtype)


def conv_silu(t, w, b, col_scale):
    tm = CTX
    c = t.shape[1]
    hb = tm // _CONV_HALO
    n_halo_blocks = NT // _CONV_HALO
    wp = jnp.zeros((8, c), F32).at[:ML_CONV].set(w)
    return pl.pallas_call(
        _conv_kernel,
        grid=(NT // tm,),
        in_specs=[
            pl.BlockSpec((_CONV_HALO, c), lambda i: (jnp.maximum(i * hb - 1, 0), 0)),
            pl.BlockSpec((tm, c), lambda i: (i, 0)),
            pl.BlockSpec((_CONV_HALO, c), lambda i: (jnp.minimum((i + 1) * hb, n_halo_blocks - 1), 0)),
            pl.BlockSpec((8, c), lambda i: (0, 0)),
            pl.BlockSpec((1, c), lambda i: (0, 0)),
            pl.BlockSpec((1, c), lambda i: (0, 0)),
        ],
        out_specs=pl.BlockSpec((tm, c), lambda i: (i, 0)),
        out_shape=jax.ShapeDtypeStruct((NT, c), BF16),
        scratch_shapes=[pltpu.VMEM((tm + 2 * _CONV_HALO, c), F32)],
        compiler_params=_cp(("parallel",)),
        name="conv_silu",
    )(t, t, t, wp, b.reshape(1, c), col_scale.reshape(1, c))


def _split3(x):
    hi = x.astype(BF16)
    r = x - hi.astype(F32)
    mid = r.astype(BF16)
    lo = (r - mid.astype(F32)).astype(BF16)
    return hi, mid, lo


def _log_sigmoid(x):
    return jnp.minimum(x, 0.0) - jnp.log1p(jnp.exp(-jnp.abs(x)))


def _mlstm_chunk(reverse, q, kt, v_ext, gc, gr, gb_col, gb_row, c_ref, m):
    ln = q.shape[0]
    d = 1 if reverse else 0
    i_col = gc[:, 2 * d:2 * d + 1] + gb_col[:, 2 * d:2 * d + 1]
    f_col = _log_sigmoid(gc[:, 2 * d + 1:2 * d + 2] + gb_col[:, 2 * d + 1:2 * d + 2])
    i_row = gr[2 * d:2 * d + 1, :] + gb_row[2 * d:2 * d + 1, :]
    f_row = _log_sigmoid(gr[2 * d + 1:2 * d + 2, :] + gb_row[2 * d + 1:2 * d + 2, :])
    t_idx = lax.broadcasted_iota(jnp.int32, (ln, ln), 0)
    s_idx = lax.broadcasted_iota(jnp.int32, (ln, ln), 1)
    causal = (s_idx >= t_idx) if reverse else (s_idx <= t_idx)
    tri = jnp.where(causal, 1.0, 0.0).astype(BF16)
    f_col_w = jnp.broadcast_to(f_col, (ln, HEAD_DIM))
    b_col = functools.reduce(lambda a, b: a + b,
                             [jnp.dot(tri, p, preferred_element_type=F32) for p in _split3(f_col_w)])[:, 0:1]
    f_row_w = jnp.broadcast_to(f_row, (16, ln))
    b_row = functools.reduce(lambda a, b: a + b,
                             [lax.dot_general(p, tri, _NT_DIMS, preferred_element_type=F32) for p in _split3(f_row_w)])[0:1, :]
    total = jnp.sum(f_col, axis=0, keepdims=True)

    dmat = jnp.where(causal, b_col - b_row + i_row, -jnp.inf)
    inter = b_col + m
    m_t = jnp.maximum(inter, jnp.max(dmat, axis=-1, keepdims=True))
    w_inter = jnp.exp(inter - m_t)
    s = jnp.dot(q, kt, preferred_element_type=F32) * jnp.exp(dmat - m_t)
    numden = (w_inter * jnp.dot(q, c_ref[...].astype(BF16), preferred_element_type=F32)
              + jnp.dot(s.astype(BF16), v_ext, preferred_element_type=F32))
    den = numden[:, HEAD_DIM:HEAD_DIM + 1]
    h = numden[:, :HEAD_DIM] / jnp.maximum(jnp.abs(den), jnp.exp(-m_t))

    g = total - b_col + i_col
    m_new = jnp.maximum(total + m, jnp.max(g, axis=0, keepdims=True))
    decay = jnp.exp(total + m - m_new)
    wg = jnp.exp(g - m_new)
    upd = jnp.dot(kt, (wg * v_ext.astype(F32)).astype(BF16), preferred_element_type=F32)
    c_ref[...] = decay * c_ref[...] + upd
    return h, m_new


def _mlstm_kernel(q_ref, kt_ref, v_ref, o_ref, qc_ref, ktc_ref, vc_ref, oc_ref, gc_ref, gr_ref, gbc_ref, gbr_ref,
                  hn_ref, out_ref, outc_ref, cf_ref, cb_ref, hf_ref, hb_ref):
    ln = ML_CHUNK
    n_chunks = SEQ // ln
    nh = ML_HEADS_PER_STEP
    ones_col = jnp.where(lax.broadcasted_iota(jnp.int32, (ln, HEAD_DIM), 1) == 0, 1.0, 0.0).astype(BF16)

    def v_ext(v):
        return jnp.concatenate([v, ones_col], axis=1)

    def cols(hd):
        return slice(hd * HEAD_DIM, (hd + 1) * HEAD_DIM)

    def finish(h, o_gate, hn):
        ms = jnp.mean(h * h, axis=-1, keepdims=True)
        return (h * lax.rsqrt(ms + EPS) * hn * jax.nn.sigmoid(o_gate.astype(F32))).astype(BF16)

    def finish_heads(h, o_gate):
        return jnp.concatenate([finish(h[:, cols(hd)], o_gate[:, cols(hd)], hn_ref[:, cols(hd)]) for hd in range(nh)],
                               axis=1)

    cf_ref[...] = jnp.zeros(cf_ref.shape, F32)
    cb_ref[...] = jnp.zeros(cb_ref.shape, F32)
    m0 = jnp.zeros((1, 1), F32)

    ms = []
    for hd in range(nh):
        gc = gc_ref[hd, 0:ln, :]
        gr = gr_ref[hd, :, 0:ln]
        vx = v_ext(vc_ref[:, cols(hd)])
        hf, mf = _mlstm_chunk(False, qc_ref[:, cols(hd)], ktc_ref[cols(hd), :], vx, gc, gr, gbc_ref[hd], gbr_ref[hd],
                              cf_ref.at[hd], m0)
        hb, mb = _mlstm_chunk(True, qc_ref[:, cols(hd)], ktc_ref[cols(hd), :], vx, gc, gr, gbc_ref[hd], gbr_ref[hd],
                              cb_ref.at[hd], m0)
        outc_ref[:, cols(hd)] = finish(hf + hb, oc_ref[:, cols(hd)], hn_ref[:, cols(hd)])
        ms += [mf, mb]

    def body(c, carry):
        carry = list(carry)
        for hd in range(nh):
            for reverse, c_ref, h_ref in ((False, cf_ref, hf_ref), (True, cb_ref, hb_ref)):
                cc = (n_chunks - 1 - c) if reverse else c
                r0 = pl.multiple_of(cc * ln, ln)
                g0 = pl.multiple_of(cc * ln + CTX, ln)
                k = 2 * hd + int(reverse)
                h, carry[k] = _mlstm_chunk(reverse, q_ref[pl.ds(r0, ln), cols(hd)], kt_ref[cols(hd), pl.ds(r0, ln)],
                                           v_ext(v_ref[pl.ds(r0, ln), cols(hd)]), gc_ref[hd, pl.ds(g0, ln), :],
                                           gr_ref[hd, :, pl.ds(g0, ln)], gbc_ref[hd], gbr_ref[hd], c_ref.at[hd],
                                           carry[k])
                h_ref[pl.ds(r0, ln), cols(hd)] = h
        return tuple(carry)

    lax.fori_loop(0, n_chunks, body, tuple(ms), unroll=2)

    def fin_body(c, _):
        r0 = pl.multiple_of(c * ln, ln)
        out_ref[pl.ds(r0, ln), :] = finish_heads(hf_ref[pl.ds(r0, ln), :] + hb_ref[pl.ds(r0, ln), :],
                                                 o_ref[pl.ds(r0, ln), :])
        return 0

    lax.fori_loop(0, n_chunks, fin_body, 0)


def mlstm(qk, kt, vo, gates_col, gates_row, gate_b, hnorm):
    hh = ML_HEADS
    nh = ML_HEADS_PER_STEP
    hs = hh // nh
    w = nh * HEAD_DIM
    ctx_blk = NX // CTX
    tot = CTX + SEQ
    gb = gate_b.astype(F32).transpose(2, 0, 1).reshape(hh, 4)
    gb_col = jnp.zeros((hh, 1, HEAD_DIM), F32).at[:, 0, :4].set(gb)
    gb_row = jnp.zeros((hh, 8, 1), F32).at[:, :4, 0].set(gb)
    lat, ctx = pl.pallas_call(
        _mlstm_kernel,
        grid=(BATCH, hs),
        in_specs=[
            pl.BlockSpec((SEQ, w), lambda b, h: (b, h)),
            pl.BlockSpec((w, SEQ), lambda b, h: (h, b)),
            pl.BlockSpec((SEQ, w), lambda b, h: (b, h)),
            pl.BlockSpec((SEQ, w), lambda b, h: (b, hs + h)),
            pl.BlockSpec((CTX, w), lambda b, h: (ctx_blk + b, h)),
            pl.BlockSpec((w, CTX), lambda b, h: (h, ctx_blk + b)),
            pl.BlockSpec((CTX, w), lambda b, h: (ctx_blk + b, h)),
            pl.BlockSpec((CTX, w), lambda b, h: (ctx_blk + b, hs + h)),
            pl.BlockSpec((nh, tot, HEAD_DIM), lambda b, h: (b * hs + h, 0, 0)),
            pl.BlockSpec((nh, 8, tot), lambda b, h: (b * hs + h, 0, 0)),
            pl.BlockSpec((nh, 1, HEAD_DIM), lambda b, h: (h, 0, 0)),
            pl.BlockSpec((nh, 8, 1), lambda b, h: (h, 0, 0)),
            pl.BlockSpec((1, w), lambda b, h: (0, h)),
        ],
        out_specs=[pl.BlockSpec((SEQ, w), lambda b, h: (b, h)), pl.BlockSpec((CTX, w), lambda b, h: (b, h))],
        out_shape=[jax.ShapeDtypeStruct((NX, ML_WIDTH), BF16), jax.ShapeDtypeStruct((BATCH * CTX, ML_WIDTH), BF16)],
        scratch_shapes=[
            pltpu.VMEM((nh, HEAD_DIM, 2 * HEAD_DIM), F32),
            pltpu.VMEM((nh, HEAD_DIM, 2 * HEAD_DIM), F32),
            pltpu.VMEM((SEQ, w), F32),
            pltpu.VMEM((SEQ, w), F32),
        ],
        compiler_params=_cp(("parallel", "parallel")),
        name="mlstm",
    )(qk, kt, vo, vo, qk, kt, vo, vo, gates_col, gates_row, gb_col, gb_row, hnorm.reshape(1, ML_WIDTH))
    return jnp.concatenate([lat, ctx], axis=0)


def _diff_attn_kernel(q0_ref, q1_ref, k0_ref, k1_ref, k0c_ref, k1c_ref, v_ref, vc_ref, lam_ref, sub_ref, o_ref,
                      *, lambda_init):
    lam = lam_ref[...]
    lam_full = (jnp.exp(jnp.sum(lam[0:1] * lam[1:2], axis=-1, keepdims=True))
                - jnp.exp(jnp.sum(lam[2:3] * lam[3:4], axis=-1, keepdims=True)) + lambda_init)

    tq = q0_ref.shape[0]
    qs = (q0_ref[...], q1_ref[...])

    def step(carry, ks, v):
        out = []
        for (m, l, acc), q, k in zip(carry, qs, ks):
            s = lax.dot_general(q, k, _NT_DIMS, preferred_element_type=F32)
            m_new = jnp.maximum(m, jnp.max(s, axis=-1, keepdims=True))
            alpha = jnp.exp2(m - m_new)
            p = jnp.exp2(s - m_new)
            l = alpha * l + jnp.sum(p, axis=-1, keepdims=True)
            acc = alpha * acc + jnp.dot(p.astype(BF16), v, preferred_element_type=F32)
            out.append((m_new, l, acc))
        return tuple(out)

    init = tuple((jnp.full((tq, 1), -jnp.inf, F32), jnp.zeros((tq, 1), F32), jnp.zeros((tq, DA_VDIM), F32))
                 for _ in range(2))

    carry = init
    for c in range(SEQ // DA_KCHUNK):
        rows = slice(c * DA_KCHUNK, (c + 1) * DA_KCHUNK)
        carry = step(carry, (k0_ref[rows, :], k1_ref[rows, :]), v_ref[rows, :])
    (_, l0, acc0), (_, l1, acc1) = step(carry, (k0c_ref[...], k1c_ref[...]), vc_ref[...])
    o = acc0 / l0 - lam_full * (acc1 / l1)
    ms = jnp.mean(o * o, axis=-1, keepdims=True)
    o_ref[...] = (o * lax.rsqrt(ms + EPS) * sub_ref[...] * (1.0 - lambda_init)).astype(o_ref.dtype)


def diff_attention(qkv, lam, subln, lambda_init, tq=1024):
    nq = SEQ // tq
    ctx_blk = NX // CTX
    kcol = 2 * DA_HEADS
    vcol = 2 * DA_HEADS
    return pl.pallas_call(
        functools.partial(_diff_attn_kernel, lambda_init=lambda_init),
        grid=(BATCH, DA_HEADS, nq),
        in_specs=[
            pl.BlockSpec((tq, HEAD_DIM), lambda b, h, i: (b * nq + i, 2 * h)),
            pl.BlockSpec((tq, HEAD_DIM), lambda b, h, i: (b * nq + i, 2 * h + 1)),
            pl.BlockSpec((SEQ, HEAD_DIM), lambda b, h, i: (b, kcol + 2 * h)),
            pl.BlockSpec((SEQ, HEAD_DIM), lambda b, h, i: (b, kcol + 2 * h + 1)),
            pl.BlockSpec((CTX, HEAD_DIM), lambda b, h, i: (ctx_blk + b, kcol + 2 * h)),
            pl.BlockSpec((CTX, HEAD_DIM), lambda b, h, i: (ctx_blk + b, kcol + 2 * h + 1)),
            pl.BlockSpec((SEQ, DA_VDIM), lambda b, h, i: (b, vcol + h)),
            pl.BlockSpec((CTX, DA_VDIM), lambda b, h, i: (ctx_blk + b, vcol + h)),
            pl.BlockSpec((4, HEAD_DIM), lambda b, h, i: (0, 0)),
            pl.BlockSpec((1, DA_VDIM), lambda b, h, i: (0, 0)),
        ],
        out_specs=pl.BlockSpec((tq, DA_VDIM), lambda b, h, i: (b * nq + i, h)),
        out_shape=jax.ShapeDtypeStruct((NX, DA_HEADS * DA_VDIM), BF16),
        compiler_params=_cp(("parallel", "parallel", "arbitrary"), vmem_mb=56),
        name="diff_attn",
    )(qkv, qkv, qkv, qkv, qkv, qkv, qkv, qkv, lam.astype(F32), subln.reshape(1, DA_VDIM).astype(F32))


def dispatch_tables(slot6_t, w6_t, units, n_rows):
    nt = n_rows // TOK_TILE
    max_units = (n_rows * TOP_K + nt * N_EXPERTS * (UNIT - 1)) // UNIT
    n_ffn_tiles = (max_units + N_EXPERTS * (FFN_UNITS - 1)) // FFN_UNITS + 1
    nun = units[:, :, 0].astype(jnp.int32)
    loc_off = jnp.cumsum(nun, axis=1) - nun
    slot6 = slot6_t.transpose(0, 2, 1)
    w6 = w6_t.transpose(0, 2, 1)

    seg_un = nun.sum(axis=0)
    seg_pad = (seg_un + FFN_UNITS - 1) // FFN_UNITS * FFN_UNITS
    seg_end = jnp.cumsum(seg_pad)
    seg_start = seg_end - seg_pad
    gstart = seg_start[None, :] + jnp.cumsum(nun, axis=0) - nun
    u = jnp.arange(UNITS_PER_TILE, dtype=jnp.int32)
    loc_end = loc_off + nun
    ue = (loc_end[:, None, :] <= u[None, :, None]).sum(axis=-1)
    onehot = ue[:, :, None] == jnp.arange(N_EXPERTS)[None, None, :]
    dst = jnp.sum(jnp.where(onehot, (gstart - loc_off)[:, None, :], 0), axis=-1) + u[None, :]
    n_units_total = n_ffn_tiles * FFN_UNITS
    flat_dst = jnp.where(ue < N_EXPERTS, dst, n_units_total).reshape(-1).astype(jnp.int32)
    src_write = jnp.full((n_units_total + 1,), -1, jnp.int32).at[flat_dst].set(
        jnp.arange(nt * UNITS_PER_TILE, dtype=jnp.int32))[:n_units_total]
    src_read = jnp.where(src_write >= 0, src_write, UNITS_PER_TILE - 1)
    chunk_start = jnp.concatenate([seg_start, seg_end[-1:]]).astype(jnp.int32) // FFN_UNITS
    return dict(slot6=slot6, w6=w6, slot6_t=slot6_t, src_read=src_read, src_write=src_write,
                chunk_start=chunk_start, nt=nt)


def _moe_gather_kernel(h_ref, slot_ref, o_ref):
    s_iota = lax.broadcasted_iota(jnp.int32, (SLOTS, TOK_TILE), 0)
    slots = slot_ref[0]
    p = jnp.zeros((SLOTS, TOK_TILE), F32)
    for k in range(TOP_K):
        p = jnp.where(s_iota == slots[k:k + 1, :], 1.0, p)
    p = p.astype(BF16)
    o_ref[0] = jnp.dot(p, h_ref[...], preferred_element_type=F32).astype(BF16)


def moe_gather(h, slot6_t, nt):
    return pl.pallas_call(
        _moe_gather_kernel,
        grid=(nt,),
        in_specs=[pl.BlockSpec((TOK_TILE, D), lambda i: (i, 0)), pl.BlockSpec((1, 8, TOK_TILE), lambda i: (i, 0, 0))],
        out_specs=pl.BlockSpec((1, SLOTS, D), lambda i: (i, 0, 0)),
        out_shape=jax.ShapeDtypeStruct((nt, SLOTS, D), BF16),
        compiler_params=_cp(("parallel",)),
        name="moe_gather",
    )(h, slot6_t)


def _unit_copy(src_hbm, buf_ref, sem_ref, slot, src_unit, j):
    return pltpu.make_async_copy(src_hbm.at[pl.ds(pl.multiple_of(src_unit * UNIT, UNIT), UNIT)],
                                 buf_ref.at[slot, pl.ds(j * UNIT, UNIT)], sem_ref.at[slot])


def _fetch_units(table_ref, base, n_units, src_hbm, buf_ref, sem_ref, slot):
    def body(j, _):
        _unit_copy(src_hbm, buf_ref, sem_ref, slot, table_ref[base + j], j).start()
        return 0

    lax.fori_loop(0, n_units, body, 0, unroll=8)


def _wait_units(n_units, src_hbm, buf_ref, sem_ref, slot):
    pltpu.make_async_copy(src_hbm.at[pl.ds(0, n_units * UNIT)], buf_ref.at[slot], sem_ref.at[slot]).wait()


def _moe_ffn_kernel(srcr_ref, srcw_ref, cs_ref, x_hbm, wg_ref, wu_ref, wd_ref, y_hbm, xbuf_ref, ybuf_ref,
                    sem_in, sem_out, wgb_ref, wub_ref, wdb_ref):
    e = pl.program_id(0)
    lo = cs_ref[e]
    hi = cs_ref[e + 1]
    total = cs_ref[N_EXPERTS]

    def out_copy(slot, su, j):
        return pltpu.make_async_copy(ybuf_ref.at[slot, pl.ds(j * UNIT, UNIT)],
                                     y_hbm.at[pl.ds(pl.multiple_of(su * UNIT, UNIT), UNIT)], sem_out.at[slot])

    def out_units(c, slot, start):
        full = srcw_ref[c * FFN_UNITS + FFN_UNITS - 1] >= 0

        @pl.when(full)
        def _():
            if start:
                for j in range(FFN_UNITS):
                    out_copy(slot, srcw_ref[c * FFN_UNITS + j], j).start()
            else:
                pltpu.make_async_copy(ybuf_ref.at[slot], y_hbm.at[pl.ds(0, FFN_TM)], sem_out.at[slot]).wait()

        @pl.when(jnp.logical_not(full))
        def _():
            for j in range(FFN_UNITS):
                su = srcw_ref[c * FFN_UNITS + j]

                @pl.when(su >= 0)
                def _():
                    cp = out_copy(slot, su, j)
                    if start:
                        cp.start()
                    else:
                        cp.wait()

    @pl.when(hi > lo)
    def _():
        wgb_ref[...] = wg_ref[0].astype(BF16)
        wub_ref[...] = wu_ref[0].astype(BF16)
        wdb_ref[...] = wd_ref[0].astype(BF16)

    def fetch(c):
        @pl.when(c < total)
        def _():
            _fetch_units(srcr_ref, c * FFN_UNITS, FFN_UNITS, x_hbm, xbuf_ref, sem_in, c % FFN_IN_DEPTH)

    @pl.when(e == 0)
    def _():
        for c in range(FFN_IN_DEPTH - 1):
            fetch(c)

    def chunk(c, _):
        slot = c % 2
        in_slot = c % FFN_IN_DEPTH
        fetch(c + FFN_IN_DEPTH - 1)
        _wait_units(FFN_UNITS, x_hbm, xbuf_ref, sem_in, in_slot)

        @pl.when(c >= 2)
        def _():
            out_units(c - 2, slot, False)

        x = xbuf_ref[in_slot]
        g = jnp.dot(x, wgb_ref[...], preferred_element_type=F32)
        u = jnp.dot(x, wub_ref[...], preferred_element_type=F32)
        a = (_silu(g) * u).astype(BF16)
        ybuf_ref[slot] = jnp.dot(a, wdb_ref[...], preferred_element_type=F32).astype(BF16)
        out_units(c, slot, True)
        return 0

    lax.fori_loop(lo, hi, chunk, 0)

    @pl.when(e == N_EXPERTS - 1)
    def _():
        for back in (2, 1):
            c = total - back

            @pl.when(c >= 0)
            def _():
                out_units(c, c % 2, False)


def moe_ffn(x_tiles, tabs, wg, wu, wd):
    x_flat = x_tiles.reshape(-1, D)
    idx = lambda e, *_: (e, 0, 0)
    grid_spec = pltpu.PrefetchScalarGridSpec(
        num_scalar_prefetch=3,
        grid=(N_EXPERTS,),
        in_specs=[
            pl.BlockSpec(memory_space=pl.ANY),
            pl.BlockSpec((1, D, D_EXPERT), idx),
            pl.BlockSpec((1, D, D_EXPERT), idx),
            pl.BlockSpec((1, D_EXPERT, D), idx),
        ],
        out_specs=pl.BlockSpec(memory_space=pl.ANY),
        scratch_shapes=[pltpu.VMEM((FFN_IN_DEPTH, FFN_TM, D), BF16), pltpu.VMEM((2, FFN_TM, D), BF16),
                        pltpu.SemaphoreType.DMA((FFN_IN_DEPTH,)), pltpu.SemaphoreType.DMA((2,)),
                        pltpu.VMEM((D, D_EXPERT), BF16), pltpu.VMEM((D, D_EXPERT), BF16),
                        pltpu.VMEM((D_EXPERT, D), BF16)],
    )
    y = pl.pallas_call(
        _moe_ffn_kernel,
        grid_spec=grid_spec,
        out_shape=jax.ShapeDtypeStruct(x_flat.shape, BF16),
        input_output_aliases={3: 0},
        compiler_params=_cp(("arbitrary",), vmem_mb=56),
        name="moe_ffn",
    )(tabs["src_read"], tabs["src_write"], tabs["chunk_start"], x_flat, wg, wu, wd)
    return y.reshape(x_tiles.shape)


def _shared_ffn_kernel(a_ref, wg_ref, wu_ref, wd_ref, o_ref):
    a = a_ref[...]
    g = jnp.dot(a, wg_ref[...], preferred_element_type=F32)
    u = jnp.dot(a, wu_ref[...], preferred_element_type=F32)
    o_ref[...] = jnp.dot((_silu(g) * u).astype(BF16), wd_ref[...], preferred_element_type=F32).astype(o_ref.dtype)


def shared_ffn(h, wg, wu, wd, n_rows, tm=512):
    return pl.pallas_call(
        _shared_ffn_kernel,
        grid=(n_rows // tm,),
        in_specs=[
            pl.BlockSpec((tm, D), lambda i: (i, 0)),
            pl.BlockSpec((D, D_EXPERT), lambda i: (0, 0)),
            pl.BlockSpec((D, D_EXPERT), lambda i: (0, 0)),
            pl.BlockSpec((D_EXPERT, D), lambda i: (0, 0)),
        ],
        out_specs=pl.BlockSpec((tm, D), lambda i: (i, 0)),
        out_shape=jax.ShapeDtypeStruct((n_rows, D), BF16),
        compiler_params=_cp(("parallel",)),
        name="shared_ffn",
    )(h, wg, wu, wd)


def _moe_combine_kernel(y_ref, slot_ref, w_ref, sh_ref, res_ref, gm_ref, o_ref):
    lane = lax.broadcasted_iota(jnp.int32, (TOK_TILE, SLOTS), 1)
    slots = slot_ref[0]
    w = w_ref[0]
    pw = jnp.zeros((TOK_TILE, SLOTS), F32)
    for k in range(TOP_K):
        pw = jnp.where(lane == slots[:, k:k + 1], w[:, k:k + 1], pw)
    routed = jnp.dot(pw.astype(BF16), y_ref[0], preferred_element_type=F32)
    o_ref[...] = res_ref[...] + gm_ref[0] * (routed + sh_ref[...].astype(F32))


def moe_combine(y_tiles, tabs, shared, res, mods, n_rows):
    nt = tabs["nt"]
    tm = TOK_TILE
    return pl.pallas_call(
        _moe_combine_kernel,
        grid=(nt,),
        in_specs=[
            pl.BlockSpec((1, SLOTS, D), lambda i: (i, 0, 0)),
            pl.BlockSpec((1, tm, 8), lambda i: (i, 0, 0)),
            pl.BlockSpec((1, tm, 8), lambda i: (i, 0, 0)),
            pl.BlockSpec((tm, D), lambda i: (i, 0)),
            pl.BlockSpec((tm, D), lambda i: (i, 0)),
            pl.BlockSpec((1, 1, D), lambda i: (_seg_of_tile(i, tm) * 6 + 5, 0, 0)),
        ],
        out_specs=pl.BlockSpec((tm, D), lambda i: (i, 0)),
        out_shape=jax.ShapeDtypeStruct((n_rows, D), F32),
        compiler_params=_cp(("parallel",), vmem_mb=56),
        name="moe_combine",
    )(y_tiles, tabs["slot6"], tabs["w6"], shared, res, mods)


def moe_block(xa, gain, mods, router_w, router_b, wg, wu, wd, sg, su, sd, n_rows):
    h, slot6_t, w6_t, units = norm_route(xa, gain, mods, router_w, router_b, n_rows)
    tabs = dispatch_tables(slot6_t, w6_t, units, n_rows)
    x_tiles = moe_gather(h, tabs["slot6_t"], tabs["nt"])
    y_tiles = moe_ffn(x_tiles, tabs, wg, wu, wd)
    shared = shared_ffn(h, sg.astype(BF16), su.astype(BF16), sd.astype(BF16), n_rows)
    return moe_combine(y_tiles, tabs, shared, xa, mods, n_rows)


def rope_tables():
    t = jnp.arange(SEQ)
    row = (t // GRID_W).astype(F32)
    col = (t % GRID_W).astype(F32)
    n_freq = HEAD_DIM // 4
    inv_freq = ROPE_THETA ** (-jnp.arange(n_freq, dtype=F32) / n_freq)
    ang = jnp.concatenate([row[:, None] * inv_freq, col[:, None] * inv_freq], axis=-1)
    ang = jnp.concatenate([ang, ang], axis=-1)
    sign = jnp.where(jnp.arange(HEAD_DIM) < HEAD_DIM // 2, -1.0, 1.0)
    cos = jnp.concatenate([jnp.cos(ang)] * BATCH + [jnp.ones((BATCH * CTX, HEAD_DIM), F32)], axis=0)
    sin = jnp.concatenate([jnp.sin(ang) * sign] * BATCH + [jnp.zeros((BATCH * CTX, HEAD_DIM), F32)], axis=0)
    return cos, sin


def even_layer(xa, mods, norm1, w_in, na_qnorm, na_knorm, na_rpb, conv_w, conv_b, gate_b, hnorm, w_out):
    h = norm_mod(xa, norm1, mods, 0, NT)
    w = w_in.astype(BF16)
    c0 = 3 * NA_WIDTH
    c1 = c0 + 2 * ML_WIDTH
    c2 = c1 + 2 * ML_WIDTH
    scale = HEAD_DIM ** -0.5
    gain = jnp.concatenate([jnp.tile(na_qnorm.astype(F32) * scale, NA_HEADS), jnp.tile(na_knorm.astype(F32), NA_HEADS),
                            jnp.ones((NA_WIDTH,), F32)]).reshape(1, c0)
    qkv = proj(h, w[:, :c0], BF16, gain=gain, n_norm_cols=2 * NA_WIDTH)
    na_h = neighbourhood_attention(qkv, na_bias_table(na_rpb))

    ml_qk = proj(h, w[:, c0:c1], F32)
    ml_vo = proj(h, w[:, c1:c2], BF16)
    n_gate = w_in.shape[1] - c2
    w_gate = jnp.zeros((D, HEAD_DIM), BF16).at[:, :n_gate].set(w[:, c2:])
    g = proj(h, w_gate, F32, tn=HEAD_DIM)[:, :n_gate]
    col_scale = jnp.concatenate([jnp.ones((ML_WIDTH,), F32), jnp.full((ML_WIDTH,), HEAD_DIM ** -0.5, F32)])
    qk = conv_silu(ml_qk, conv_w.astype(F32), conv_b.astype(F32), col_scale)
    kt = qk[:, ML_WIDTH:].T
    g = g.reshape(NT, 4, ML_HEADS)
    g = jnp.concatenate([g[NX:].reshape(BATCH, CTX, 4, ML_HEADS), g[:NX].reshape(BATCH, SEQ, 4, ML_HEADS)], axis=1)
    g = g.transpose(0, 3, 1, 2).reshape(BATCH * ML_HEADS, CTX + SEQ, 4)
    g_col = jnp.zeros((BATCH * ML_HEADS, CTX + SEQ, HEAD_DIM), F32).at[:, :, :4].set(g)
    g_row = jnp.zeros((BATCH * ML_HEADS, 8, CTX + SEQ), F32).at[:, :4, :].set(g.transpose(0, 2, 1))
    ml_h = mlstm(qk, kt, ml_vo, g_col, g_row, gate_b, hnorm.astype(F32))

    wo = w_out.astype(BF16)
    return out_proj([na_h, ml_h], [wo[:NA_WIDTH], wo[NA_WIDTH:]], xa, mods, 2, NT)


def odd_layer(xa, mods, norm1, w_in, qnorm, knorm, lam, subln, w_out, lambda_init):
    h = norm_mod(xa, norm1, mods, 0, NT)
    scale = HEAD_DIM ** -0.5 * math.log2(math.e)
    n_qk = 2 * DA_HEADS * HEAD_DIM
    gain = jnp.concatenate([jnp.tile(qnorm.astype(F32) * scale, 2 * DA_HEADS), jnp.tile(knorm.astype(F32), 2 * DA_HEADS),
                            jnp.ones((DA_HEADS * DA_VDIM,), F32)]).reshape(1, -1)
    cos, sin = rope_tables()
    qkv = proj(h, w_in.astype(BF16), BF16, gain=gain, n_norm_cols=2 * n_qk, cos=cos, sin=sin)
    o = diff_attention(qkv, lam, subln, lambda_init)
    return out_proj([o], [w_out.astype(BF16)], xa, mods, 2, NX)


def diff_lambda_init(layer):
    return 0.8 - 0.6 * math.exp(-0.3 * layer)


def kernel(x, c, ctx, c_ctx, l0_ada_w, l0_ada_b, l0_norm1, l0_norm2, l0_w_in, l0_na_qnorm, l0_na_knorm, l0_na_rpb, l0_ml_conv_w, l0_ml_conv_b, l0_ml_gate_b, l0_ml_hnorm, l0_w_out, l0_router_w, l0_router_b, l0_exp_gate, l0_exp_up, l0_exp_down, l0_sh_gate, l0_sh_up, l0_sh_down, l1_ada_w, l1_ada_b, l1_norm1, l1_norm2, l1_w_in, l1_qnorm, l1_knorm, l1_lambda, l1_subln, l1_w_out, l1_router_w, l1_router_b, l1_exp_gate, l1_exp_up, l1_exp_down, l1_sh_gate, l1_sh_up, l1_sh_down):
    assert x.shape == (BATCH, SEQ, D) and ctx.shape == (BATCH, CTX, D)
    xa = jnp.concatenate([x.reshape(NX, D), ctx.reshape(BATCH * CTX, D)], axis=0).astype(F32)
    cvec = jnp.zeros((8, D), F32).at[:BATCH].set(c).at[BATCH].set(c_ctx)

    mods0 = adaln(cvec, l0_ada_w, l0_ada_b)
    xa = even_layer(xa, mods0, l0_norm1, l0_w_in, l0_na_qnorm, l0_na_knorm, l0_na_rpb, l0_ml_conv_w, l0_ml_conv_b,
                    l0_ml_gate_b, l0_ml_hnorm, l0_w_out)
    xa = moe_block(xa, l0_norm2, mods0, l0_router_w, l0_router_b, l0_exp_gate, l0_exp_up, l0_exp_down,
                   l0_sh_gate, l0_sh_up, l0_sh_down, NT)

    mods1 = adaln(cvec, l1_ada_w, l1_ada_b)
    xl = odd_layer(xa, mods1, l1_norm1, l1_w_in, l1_qnorm, l1_knorm, l1_lambda, l1_subln, l1_w_out, diff_lambda_init(1))
    xl = moe_block(xl, l1_norm2, mods1, l1_router_w, l1_router_b, l1_exp_gate, l1_exp_up, l1_exp_down,
                   l1_sh_gate, l1_sh_up, l1_sh_down, NX)
    return xl.reshape(BATCH, SEQ, D)
```
